```python
import math
import jax
import jax.numpy as jnp
from jax import lax
import numpy as np

D_MODEL = 1024
BATCH = 2
SEQ = 8192
DEPTH = 4
DEC_BATCH = 128
DEC_SEQ = 8
PAST_LEN = 2048
PAGE_SIZE = 128

N_MIXERS = 2
N_MLSTM = (DEPTH + 1) // 2
N_DIFF = DEPTH // 2
N_DENSE = (DEPTH + 1) // 2
N_MOE = DEPTH // 2
M_HEADS = 4
M_DQK = D_MODEL // (2 * M_HEADS)
M_DV = D_MODEL // M_HEADS
M_CHUNK = 128
M_SPLITS = (M_HEADS * M_DQK, 2 * M_HEADS * M_DQK,
            2 * M_HEADS * M_DQK + M_HEADS * M_DV, 2 * M_HEADS * M_DQK + 2 * M_HEADS * M_DV)
M_PROJ = 2 * M_HEADS * M_DQK + 2 * M_HEADS * M_DV + 2 * M_HEADS
A_HEADS = 8
A_DH = D_MODEL // (2 * A_HEADS)
A_DV = 2 * A_DH
A_QBLOCK = 128
A_PROJ = 3 * D_MODEL
D_FF = 2816
N_EXPERTS = 8
TOP_K = 2
D_FF_E = 2816
EPS = 1e-6

kernel_name = "hybrid_mlstm_diffattn_moe_adaln_step"


def rms_norm(x, g):
    xf = x.astype(jnp.float32)
    y = xf * lax.rsqrt(jnp.mean(xf * xf, axis=-1, keepdims=True) + EPS)
    return (y * g.astype(jnp.float32)).astype(x.dtype)


def mlstm_chunk(carry, xs):
    c_mem, n_vec, m_stab = carry
    q, k, v, ig, lf = xs
    L = q.shape[2]
    b = jnp.cumsum(lf, axis=-1)
    causal = jnp.tril(jnp.ones((L, L), dtype=bool))
    dlog = jnp.where(causal, b[..., :, None] - b[..., None, :] + ig[..., None, :], -jnp.inf)
    g = b + m_stab[..., None]
    m_t = jnp.maximum(g, jnp.max(dlog, axis=-1))
    s = jnp.einsum("bhtd,bhsd->bhts", q, k) * jnp.exp(dlog - m_t[..., None])
    dec = jnp.exp(g - m_t)[..., None]
    num = jnp.einsum("bhts,bhsv->bhtv", s, v) + dec * jnp.einsum("bhtd,bhdv->bhtv", q, c_mem)
    den = jnp.sum(s, axis=-1, keepdims=True) + dec * jnp.einsum("bhtd,bhd->bht", q, n_vec)[..., None]
    h = num / jnp.maximum(jnp.abs(den), jnp.exp(-m_t)[..., None])
    b_end = b[..., -1]
    a = b_end[..., None] - b + ig
    m_new = jnp.maximum(b_end + m_stab, jnp.max(a, axis=-1))
    wk = jnp.exp(a - m_new[..., None])
    decay = jnp.exp(b_end + m_stab - m_new)
    c_new = decay[..., None, None] * c_mem + jnp.einsum("bhs,bhsd,bhsv->bhdv", wk, k, v)
    n_new = decay[..., None] * n_vec + jnp.einsum("bhs,bhsd->bhd", wk, k)
    return (c_new, n_new, m_new), h


def mlstm_mix(h, c0, n0, m0, w_in, b_gate, g_head, w_out):
    Bn, L, _ = h.shape
    f32 = jnp.float32
    p = (h @ w_in).astype(f32)
    q, k, v, o, gates = jnp.split(p, M_SPLITS, axis=-1)
    heads = lambda t, d: t.reshape(Bn, L, M_HEADS, d).transpose(0, 2, 1, 3)
    q = heads(q, M_DQK)
    k = heads(k, M_DQK) * (M_DQK ** -0.5)
    v = heads(v, M_DV)
    gates = gates + b_gate.astype(f32)
    ig = gates[..., :M_HEADS].transpose(0, 2, 1)
    lf = jax.nn.log_sigmoid(gates[..., M_HEADS:]).transpose(0, 2, 1)
    chunk = M_CHUNK if L % M_CHUNK == 0 else L
    nc = L // chunk

    def to_chunks(t):
        return jnp.moveaxis(t.reshape(Bn, M_HEADS, nc, chunk, *t.shape[3:]), 2, 0)

    carry = (c0.astype(f32), n0.astype(f32), m0.astype(f32))
    carry, hs = lax.scan(mlstm_chunk, carry,
                         (to_chunks(q), to_chunks(k), to_chunks(v), to_chunks(ig), to_chunks(lf)))
    hs = jnp.moveaxis(hs, 0, 2).reshape(Bn, M_HEADS, L, M_DV).transpose(0, 2, 1, 3)
    hs = rms_norm(hs, g_head)
    y = hs * jax.nn.sigmoid(o).reshape(Bn, L, M_HEADS, M_DV)
    y = y.reshape(Bn, L, M_HEADS * M_DV).astype(h.dtype)
    return y @ w_out, carry


def attend_blocked(q, k, v, lam):
    Bn, L = q.shape[:2]
    nb = L // A_QBLOCK
    qb = jnp.moveaxis(q.reshape(Bn, nb, A_QBLOCK, A_HEADS, 2, A_DH), 1, 0)
    kpos = jnp.arange(L)
    scale = A_DH ** -0.5

    def one(args):
        qi, i = args
        qpos = i * A_QBLOCK + jnp.arange(A_QBLOCK)
        s = jnp.einsum("bqhcd,bkhcd->bchqk", qi, k).astype(jnp.float32) * scale
        s = jnp.where(qpos[:, None] >= kpos[None, :], s, -jnp.inf)
        pm = jax.nn.softmax(s, axis=-1)
        a = pm[:, 0] - lam * pm[:, 1]
        return jnp.einsum("bhqk,bkhv->bqhv", a.astype(v.dtype), v)

    o = lax.map(one, (qb, jnp.arange(nb)))
    return jnp.moveaxis(o, 0, 1).reshape(Bn, L, A_HEADS, A_DV)


def attend_with_past(q, k, v, k_past, v_past, lam):
    L = q.shape[1]
    P = k_past.shape[1]
    scale = A_DH ** -0.5
    s_past = jnp.einsum("bqhcd,bkhcd->bchqk", q, k_past).astype(jnp.float32) * scale
    s_new = jnp.einsum("bqhcd,bkhcd->bchqk", q, k).astype(jnp.float32) * scale
    s_new = jnp.where(jnp.tril(jnp.ones((L, L), dtype=bool)), s_new, -jnp.inf)
    pm = jax.nn.softmax(jnp.concatenate([s_past, s_new], axis=-1), axis=-1)
    a = (pm[:, 0] - lam * pm[:, 1]).astype(v.dtype)
    return (jnp.einsum("bhqk,bkhv->bqhv", a[..., :P], v_past)
            + jnp.einsum("bhqk,bkhv->bqhv", a[..., P:], v))


def diff_mix(h, k_past, v_past, w_in, g_q, g_k, lam_p, g_sub, w_out, lam_init):
    Bn, L, _ = h.shape
    q, k, v = jnp.split(h @ w_in, 3, axis=-1)
    q = rms_norm(q.reshape(Bn, L, A_HEADS, 2, A_DH), g_q)
    k = rms_norm(k.reshape(Bn, L, A_HEADS, 2, A_DH), g_k)
    v = v.reshape(Bn, L, A_HEADS, A_DV)
    lp = lam_p.astype(jnp.float32)
    lam = jnp.exp(jnp.sum(lp[0] * lp[1])) - jnp.exp(jnp.sum(lp[2] * lp[3])) + lam_init
    if k_past is None:
        o = attend_blocked(q, k, v, lam)
    else:
        o = attend_with_past(q, k, v, k_past, v_past, lam)
    o = rms_norm(o, g_sub) * (1.0 - lam_init)
    return o.reshape(Bn, L, A_HEADS * A_DV) @ w_out, k, v


def swiglu(h, w_gu, w_down):
    g, u = jnp.split(h @ w_gu, 2, axis=-1)
    return (jax.nn.silu(g) * u) @ w_down


def moe_swiglu(h, w_router, b_router, w_gu, w_down):
    Bn, L, D = h.shape
    t = h.reshape(Bn * L, D)
    logits = (t @ w_router).astype(jnp.float32) + b_router.astype(jnp.float32)
    top_v, top_i = lax.top_k(logits, TOP_K)
    gw = jax.nn.softmax(top_v, axis=-1)
    dense_gate = jnp.sum(jax.nn.one_hot(top_i, N_EXPERTS, dtype=jnp.float32) * gw[..., None], axis=1)

    def body(acc, ep):
        wgu, wd, ge = ep
        return acc + ge[:, None].astype(t.dtype) * swiglu(t, wgu, wd), None

    out, _ = lax.scan(body, jnp.zeros_like(t), (w_gu, w_down, dense_gate.T))
    return out.reshape(Bn, L, D)


def trunk(x, c, mlstm_state, paged, weights):
    (w_ada, b_ada, g_norm_mix, g_norm_ffn, w_in_m, b_gate_m, g_head_m, w_out_m,
     w_in_d, g_q_d, g_k_d, lam_d, g_sub_d, w_out_d, w_gu_f, w_down_f,
     w_router, b_router, w_gu_e, w_down_e) = weights
    new_mlstm, new_kv = [], []
    for i in range(DEPTH):
        j = i // N_MIXERS
        mod = jax.nn.silu(c) @ w_ada[i] + b_ada[i]
        sh_a, sc_a, gt_a, sh_f, sc_f, gt_f = jnp.split(mod[:, None, :], 6, axis=-1)
        h = rms_norm(x, g_norm_mix[i]) * (1 + sc_a) + sh_a
        if i % N_MIXERS == 0:
            c0, n0, m0 = mlstm_state[j]
            out, st = mlstm_mix(h, c0, n0, m0, w_in_m[j], b_gate_m[j], g_head_m[j], w_out_m[j])
            new_mlstm.append(st)
        else:
            if paged is None:
                k_past, v_past = None, None
            else:
                cache_k, cache_v, page_table = paged
                nb = page_table.shape[0]
                k_past = cache_k[j][page_table].reshape(nb, -1, A_HEADS, 2, A_DH)
                v_past = cache_v[j][page_table].reshape(nb, -1, A_HEADS, A_DV)
            lam_init = 0.8 - 0.6 * math.exp(-0.3 * i)
            out, k_new, v_new = diff_mix(h, k_past, v_past, w_in_d[j], g_q_d[j], g_k_d[j],
                                         lam_d[j], g_sub_d[j], w_out_d[j], lam_init)
            new_kv.append((k_new, v_new))
        x = x + gt_a * out
        h = rms_norm(x, g_norm_ffn[i]) * (1 + sc_f) + sh_f
        if i % 2 == 0:
            f = swiglu(h, w_gu_f[i // 2], w_down_f[i // 2])
        else:
            f = moe_swiglu(h, w_router[i // 2], b_router[i // 2], w_gu_e[i // 2], w_down_e[i // 2])
        x = x + gt_f * f
    return x, new_mlstm, new_kv


def setup_inputs(seed: int = 0) -> dict:
    key = jax.random.key(seed)
    ks = iter(jax.random.split(key, 40))

    def nrm(shape, scale=1.0):
        return scale * jax.random.normal(next(ks), shape, jnp.float32)

    def gain(shape):
        return 1.0 + 0.02 * nrm(shape)

    D = D_MODEL
    n_pages = PAST_LEN // PAGE_SIZE
    n_used = DEC_BATCH * n_pages
    n_pool = n_used + n_used // 4
    inp = {}
    inp["x_prompt"] = nrm((BATCH, SEQ, D))
    inp["x_sample"] = nrm((DEC_BATCH, DEC_SEQ, D))
    inp["c_prompt"] = nrm((BATCH, D))
    inp["c_sample"] = nrm((DEC_BATCH, D))
    inp["state_C"] = nrm((N_MLSTM, DEC_BATCH, M_HEADS, M_DQK, M_DV), 0.1)
    inp["state_n"] = nrm((N_MLSTM, DEC_BATCH, M_HEADS, M_DQK), 0.1)
    inp["state_m"] = nrm((N_MLSTM, DEC_BATCH, M_HEADS), 0.5)
    inp["cache_k"] = nrm((N_DIFF, n_pool, PAGE_SIZE, A_HEADS, 2, A_DH))
    inp["cache_v"] = nrm((N_DIFF, n_pool, PAGE_SIZE, A_HEADS, A_DV))
    perm = jax.random.permutation(next(ks), n_pool)[:n_used]
    inp["page_table"] = perm.reshape(DEC_BATCH, n_pages).astype(jnp.int32)
    inp["w_ada"] = nrm((DEPTH, D, 6 * D), 0.5 * D ** -0.5)
    inp["b_ada"] = nrm((DEPTH, 6 * D), 0.02)
    inp["g_norm_mix"] = gain((DEPTH, D))
    inp["g_norm_ffn"] = gain((DEPTH, D))
    inp["w_in_m"] = nrm((N_MLSTM, D, M_PROJ), D ** -0.5)
    b_i = nrm((N_MLSTM, M_HEADS), 0.1)
    b_f = jnp.linspace(3.0, 6.0, M_HEADS)[None, :] + nrm((N_MLSTM, M_HEADS), 0.1)
    inp["b_gate_m"] = jnp.concatenate([b_i, b_f], axis=-1)
    inp["g_head_m"] = gain((N_MLSTM, M_HEADS, M_DV))
    inp["w_out_m"] = nrm((N_MLSTM, M_HEADS * M_DV, D), (M_HEADS * M_DV) ** -0.5)
    inp["w_in_d"] = nrm((N_DIFF, D, A_PROJ), D ** -0.5)
    inp["g_q_d"] = gain((N_DIFF, A_DH))
    inp["g_k_d"] = gain((N_DIFF, A_DH))
    inp["lam_d"] = nrm((N_DIFF, 4, A_DH), 0.1)
    inp["g_sub_d"] = gain((N_DIFF, A_DV))
    inp["w_out_d"] = nrm((N_DIFF, A_HEADS * A_DV, D), (A_HEADS * A_DV) ** -0.5)
    inp["w_gu_f"] = nrm((N_DENSE, D, 2 * D_FF), D ** -0.5)
    inp["w_down_f"] = nrm((N_DENSE, D_FF, D), D_FF ** -0.5)
    inp["w_router"] = nrm((N_MOE, D, N_EXPERTS), D ** -0.5)
    inp["b_router"] = nrm((N_MOE, N_EXPERTS), 0.01)
    inp["w_gu_e"] = nrm((N_MOE, N_EXPERTS, D, 2 * D_FF_E), D ** -0.5)
    inp["w_down_e"] = nrm((N_MOE, N_EXPERTS, D_FF_E, D), D_FF_E ** -0.5)
    return inp


def reference(x_prompt, x_sample, c_prompt, c_sample, state_C, state_n, state_m,
              cache_k, cache_v, page_table, w_ada, b_ada, g_norm_mix, g_norm_ffn,
              w_in_m, b_gate_m, g_head_m, w_out_m, w_in_d, g_q_d, g_k_d, lam_d, g_sub_d,
              w_out_d, w_gu_f, w_down_f, w_router, b_router, w_gu_e, w_down_e):
    weights = (w_ada, b_ada, g_norm_mix, g_norm_ffn, w_in_m, b_gate_m, g_head_m, w_out_m,
               w_in_d, g_q_d, g_k_d, lam_d, g_sub_d, w_out_d, w_gu_f, w_down_f,
               w_router, b_router, w_gu_e, w_down_e)
    f32 = jnp.float32
    zero_state = [(jnp.zeros((BATCH, M_HEADS, M_DQK, M_DV), f32),
                   jnp.zeros((BATCH, M_HEADS, M_DQK), f32),
                   jnp.zeros((BATCH, M_HEADS), f32)) for _ in range(N_MLSTM)]
    y_prompt, st_p, kv_p = trunk(x_prompt, c_prompt, zero_state, None, weights)
    carried = [(state_C[j], state_n[j], state_m[j]) for j in range(N_MLSTM)]
    y_sample, st_s, kv_s = trunk(x_sample, c_sample, carried, (cache_k, cache_v, page_table), weights)
    new_C_prompt = jnp.stack([s[0] for s in st_p])
    new_n_prompt = jnp.stack([s[1] for s in st_p])
    new_m_prompt = jnp.stack([s[2] for s in st_p])
    new_C_sample = jnp.stack([s[0] for s in st_s])
    new_n_sample = jnp.stack([s[1] for s in st_s])
    new_m_sample = jnp.stack([s[2] for s in st_s])
    new_k_prompt = jnp.stack([kv[0] for kv in kv_p])
    new_v_prompt = jnp.stack([kv[1] for kv in kv_p])
    new_k_sample = jnp.stack([kv[0] for kv in kv_s])
    new_v_sample = jnp.stack([kv[1] for kv in kv_s])
    return (y_prompt, y_sample, new_C_prompt, new_n_prompt, new_m_prompt,
            new_C_sample, new_n_sample, new_m_sample,
            new_k_prompt, new_v_prompt, new_k_sample, new_v_sample)
```

```python
import functools
import math

import jax
import jax.numpy as jnp
from jax import lax
from jax.experimental import pallas as pl
from jax.experimental.pallas import tpu as pltpu

BF = jnp.bfloat16
F32 = jnp.float32
EPS = 1e-6
M_CHUNK = 128
LANES = 128
SUBLANES = 8
VMEM_LIMIT = 56 * 1024 * 1024
NEG_INF = float("-inf")


def _cparams(sem):
    return pltpu.CompilerParams(dimension_semantics=sem, vmem_limit_bytes=VMEM_LIMIT)


def _tile(n, pref):
    if n <= pref:
        return n
    t = pref - pref % LANES
    while n % t:
        t -= LANES
    return t


def _nt_dot(a, b, **kw):
    return lax.dot_general(a, b, (((1,), (1,)), ((), ())), preferred_element_type=F32, **kw)


def _tn_dot(a, b, **kw):
    return lax.dot_general(a, b, (((0,), (0,)), ((), ())), preferred_element_type=F32, **kw)


def _dot(a, b, **kw):
    return jnp.dot(a, b, preferred_element_type=F32, **kw)


def _ada_kernel(c_ref, w_ref, b_ref, o_ref):
    c = c_ref[...]
    a = (c * jax.nn.sigmoid(c)).astype(BF)
    o_ref[0] = _dot(a, w_ref[0].astype(BF)) + b_ref[0]


def ada_mod(c_all, w_ada, b_ada, tn):
    depth, d, n = w_ada.shape
    bp = c_all.shape[0]
    return pl.pallas_call(
        _ada_kernel,
        grid=(depth, n // tn),
        in_specs=[pl.BlockSpec((bp, d), lambda i, j: (0, 0)),
                  pl.BlockSpec((1, d, tn), lambda i, j: (i, 0, j)),
                  pl.BlockSpec((1, 1, tn), lambda i, j: (i, 0, j))],
        out_specs=pl.BlockSpec((1, bp, tn), lambda i, j: (i, 0, j)),
        out_shape=jax.ShapeDtypeStruct((depth, bp, n), F32),
        compiler_params=_cparams(("arbitrary", "arbitrary")),
        name="ada_mod",
    )(c_all, w_ada, b_ada.reshape(depth, 1, n))


def _mod_spec(rows_per_seq, tm, width, col_block):
    if rows_per_seq >= tm:
        npb = rows_per_seq // tm
        return pl.BlockSpec((1, 1, width), lambda *g: (g[0] // npb, 0, col_block(*g)))
    gb = tm // rows_per_seq
    return pl.BlockSpec((gb, 1, width), lambda *g: (g[0], 0, col_block(*g)))


def _rows(m_ref, tm):
    m = m_ref[...]
    gb, _, w = m.shape
    if gb == 1:
        return m[0]
    return jnp.broadcast_to(m, (gb, tm // gb, w)).reshape(tm, w)


def _modulated(x_ref, g_ref, sc_ref, sh_ref):
    x = x_ref[...]
    tm = x.shape[0]
    y = x * lax.rsqrt(jnp.mean(x * x, axis=-1, keepdims=True) + EPS) * g_ref[...]
    return y * (1.0 + _rows(sc_ref, tm)) + _rows(sh_ref, tm)


def _proj_kernel(x_ref, g_ref, sc_ref, sh_ref, w_ref, cs_ref, *rest):
    outs, h_scr = rest[:-1], rest[-1]

    @pl.when(pl.program_id(1) == 0)
    def _():
        h_scr[...] = _modulated(x_ref, g_ref, sc_ref, sh_ref).astype(BF)

    acc = _dot(h_scr[...], w_ref[...].astype(BF)) * cs_ref[...]
    for o in outs:
        o[...] = acc.astype(o.dtype)


def _segnorm_kernel(x_ref, g_ref, sc_ref, sh_ref, w_ref, gv_ref, seg_ref, *rest, seg):
    outs, h_scr = rest[:-1], rest[-1]

    @pl.when(pl.program_id(1) == 0)
    def _():
        h_scr[...] = _modulated(x_ref, g_ref, sc_ref, sh_ref).astype(BF)

    acc = _dot(h_scr[...], w_ref[...].astype(BF))
    sq = acc * acc
    hi = sq.astype(BF)
    lo = (sq - hi.astype(F32)).astype(BF)
    ssq = _dot(hi, seg_ref[...]) + _dot(lo, seg_ref[...])
    y = acc * lax.rsqrt(ssq * (1.0 / seg) + EPS) * gv_ref[...]
    for o in outs:
        o[...] = y.astype(o.dtype)


def norm_proj(x, rows_per_seq, mod, sc_chunk, sh_chunk, g, w, col_off, n, out_dtypes, tm, tn,
              col_scale=None, seg_gain=None, name="norm_proj"):
    t, d = x.shape
    tn = _tile(math.gcd(n, col_off) if col_off else n, tn)
    x_spec = pl.BlockSpec((tm, d), lambda i, j: (i, 0))
    g_spec = pl.BlockSpec((1, d), lambda i, j: (0, 0))
    sc_spec = _mod_spec(rows_per_seq, tm, d, lambda i, j: sc_chunk)
    sh_spec = _mod_spec(rows_per_seq, tm, d, lambda i, j: sh_chunk)
    off = col_off // tn
    w_spec = pl.BlockSpec((d, tn), lambda i, j: (0, off + j))
    out_specs = [pl.BlockSpec((tm, tn), lambda i, j: (i, j)) for _ in out_dtypes]
    out_shape = [jax.ShapeDtypeStruct((t, n), dt) for dt in out_dtypes]
    if seg_gain is None:
        if col_scale is None:
            col_scale = jnp.ones((1, n), F32)
        kern = _proj_kernel
        extra = [col_scale]
        extra_specs = [pl.BlockSpec((1, tn), lambda i, j: (0, j))]
    else:
        seg = seg_gain.shape[0]
        kern = functools.partial(_segnorm_kernel, seg=seg)
        gv = jnp.tile(seg_gain.astype(F32), tn // seg).reshape(1, tn)
        ids = jnp.arange(tn) // seg
        seg_mat = (ids[:, None] == ids[None, :]).astype(BF)
        extra = [gv, seg_mat]
        extra_specs = [pl.BlockSpec((1, tn), lambda i, j: (0, 0)),
                       pl.BlockSpec((tn, tn), lambda i, j: (0, 0))]
    res = pl.pallas_call(
        kern,
        grid=(t // tm, n // tn),
        in_specs=[x_spec, g_spec, sc_spec, sh_spec, w_spec] + extra_specs,
        out_specs=out_specs,
        out_shape=out_shape,
        scratch_shapes=[pltpu.VMEM((tm, d), BF)],
        compiler_params=_cparams(("parallel", "arbitrary")),
        name=name,
    )(x, g.reshape(1, d), mod, mod, w, *extra)
    return res


def _router_kernel(x_ref, g_ref, sc_ref, sh_ref, w_ref, b_ref, o_ref, *, n_experts):
    h = _modulated(x_ref, g_ref, sc_ref, sh_ref)
    logits = _dot(h, w_ref[...], precision=lax.Precision.HIGHEST) + b_ref[...]
    lane = lax.broadcasted_iota(jnp.int32, logits.shape, 1).astype(F32)
    big = float(LANES)
    l1 = jnp.where(lane < n_experts, logits, NEG_INF)
    m1 = jnp.max(l1, axis=-1, keepdims=True)
    i1 = jnp.min(jnp.where(l1 == m1, lane, big), axis=-1, keepdims=True)
    l2 = jnp.where(lane == i1, NEG_INF, l1)
    m2 = jnp.max(l2, axis=-1, keepdims=True)
    i2 = jnp.min(jnp.where(l2 == m2, lane, big), axis=-1, keepdims=True)
    e2 = jnp.exp(m2 - m1)
    w1 = 1.0 / (1.0 + e2)
    w2 = e2 / (1.0 + e2)
    o_ref[...] = jnp.where(lane == i1, w1, 0.0) + jnp.where(lane == i2, w2, 0.0)


def router_gates(x, rows_per_seq, mod, sc_chunk, sh_chunk, g, w_router, b_router, tm):
    t, d = x.shape
    e = w_router.shape[1]
    w_pad = jnp.pad(w_router, ((0, 0), (0, LANES - e)))
    b_pad = jnp.pad(b_router, (0, LANES - e)).reshape(1, LANES)
    return pl.pallas_call(
        functools.partial(_router_kernel, n_experts=e),
        grid=(t // tm,),
        in_specs=[pl.BlockSpec((tm, d), lambda i: (i, 0)),
                  pl.BlockSpec((1, d), lambda i: (0, 0)),
                  _mod_spec(rows_per_seq, tm, d, lambda i: sc_chunk),
                  _mod_spec(rows_per_seq, tm, d, lambda i: sh_chunk),
                  pl.BlockSpec((d, LANES), lambda i: (0, 0)),
                  pl.BlockSpec((1, LANES), lambda i: (0, 0))],
        out_specs=pl.BlockSpec((tm, LANES), lambda i: (i, 0)),
        out_shape=jax.ShapeDtypeStruct((t, LANES), F32),
        compiler_params=_cparams(("parallel",)),
        name="router",
    )(x, g.reshape(1, d), mod, mod, w_pad, b_pad)


def _swiglu_up_kernel(x_ref, g_ref, sc_ref, sh_ref, wg_ref, wu_ref, o_ref, h_scr):
    @pl.when((pl.program_id(1) == 0) & (pl.program_id(2) == 0))
    def _():
        h_scr[...] = _modulated(x_ref, g_ref, sc_ref, sh_ref).astype(BF)

    h = h_scr[...]
    a = _dot(h, wg_ref[0, 0].astype(BF))
    u = _dot(h, wu_ref[0, 0].astype(BF))
    o_ref[0] = (a * jax.nn.sigmoid(a) * u).astype(o_ref.dtype)


def swiglu_up(x, rows_per_seq, mod, sc_chunk, sh_chunk, g, w_gu, layer, tm, tn):
    t, d = x.shape
    _, e, _, f2 = w_gu.shape
    f = f2 // 2
    tn = _tile(f, tn)
    nf = f // tn
    return pl.pallas_call(
        _swiglu_up_kernel,
        grid=(t // tm, e, nf),
        in_specs=[pl.BlockSpec((tm, d), lambda i, k, j: (i, 0)),
                  pl.BlockSpec((1, d), lambda i, k, j: (0, 0)),
                  _mod_spec(rows_per_seq, tm, d, lambda i, k, j: sc_chunk),
                  _mod_spec(rows_per_seq, tm, d, lambda i, k, j: sh_chunk),
                  pl.BlockSpec((1, 1, d, tn), lambda i, k, j: (layer, k, 0, j)),
                  pl.BlockSpec((1, 1, d, tn), lambda i, k, j: (layer, k, 0, nf + j))],
        out_specs=pl.BlockSpec((1, tm, tn), lambda i, k, j: (k, i, j)),
        out_shape=jax.ShapeDtypeStruct((e, t, f), BF),
        scratch_shapes=[pltpu.VMEM((tm, d), BF)],
        compiler_params=_cparams(("parallel", "arbitrary", "arbitrary")),
        name="swiglu_up",
    )(x, g.reshape(1, d), mod, mod, w_gu, w_gu)


def _down_kernel(y_ref, w_ref, x_ref, gt_ref, *rest, gated):
    if gated:
        dg_ref, o_ref, acc = rest
    else:
        o_ref, acc = rest
    e = pl.program_id(2)

    @pl.when(e == 0)
    def _():
        acc[...] = jnp.zeros_like(acc)

    p = _dot(y_ref[0].astype(BF), w_ref[0, 0].astype(BF))
    if gated:
        dg = dg_ref[...]
        lane = lax.broadcasted_iota(jnp.int32, dg.shape, 1)
        ge = jnp.sum(jnp.where(lane == e, dg, 0.0), axis=-1, keepdims=True)
        p = ge * p
    acc[...] += p

    @pl.when(e == pl.num_programs(2) - 1)
    def _():
        o_ref[...] = x_ref[...] + _rows(gt_ref, x_ref.shape[0]) * acc[...]


def down_residual(y, w, layer, x, rows_per_seq, mod, gt_chunk, tm, tn, dense_gate=None,
                  name="down_residual"):
    e, t, k = y.shape
    d = x.shape[1]
    tn = _tile(d, tn)
    nd = d // tn
    gated = dense_gate is not None
    in_specs = [pl.BlockSpec((1, tm, k), lambda i, j, q: (q, i, 0)),
                pl.BlockSpec((1, 1, k, tn), lambda i, j, q: (layer, q, 0, j)),
                pl.BlockSpec((tm, tn), lambda i, j, q: (i, j)),
                _mod_spec(rows_per_seq, tm, tn, lambda i, j, q: gt_chunk * nd + j)]
    args = [y, w, x, mod]
    if gated:
        in_specs.append(pl.BlockSpec((tm, LANES), lambda i, j, q: (i, 0)))
        args.append(dense_gate)
    return pl.pallas_call(
        functools.partial(_down_kernel, gated=gated),
        grid=(t // tm, nd, e),
        in_specs=in_specs,
        out_specs=pl.BlockSpec((tm, tn), lambda i, j, q: (i, j)),
        out_shape=jax.ShapeDtypeStruct((t, d), F32),
        scratch_shapes=[pltpu.VMEM((tm, tn), F32)],
        compiler_params=_cparams(("parallel", "arbitrary", "arbitrary")),
        name=name,
    )(*args)


def _log_sigmoid(x):
    return jnp.minimum(x, 0.0) - jnp.log1p(jnp.exp(-jnp.abs(x)))


def _mlstm_kernel(qkv_ref, o_ref, gt_ref, bg_ref, gh_ref, c0_ref, n0_ref, m0_ref,
                  y_ref, c_ref, n_ref, m_ref, *, heads, dk, dv, chunk, seqs):
    L = chunk
    hi = lax.Precision.HIGHEST
    cd = BF if L >= 16 else F32

    @pl.when(pl.program_id(1) == 0)
    def _():
        c_ref[...] = c0_ref[...]
        n_ref[...] = n0_ref[...]
        m_ref[...] = m0_ref[...]

    row = lax.broadcasted_iota(jnp.int32, (L, L), 0)
    col = lax.broadcasted_iota(jnp.int32, (L, L), 1)
    causal = row >= col
    tril = causal.astype(F32)
    sel = (lax.broadcasted_iota(jnp.int32, (SUBLANES, LANES), 0)
           == lax.broadcasted_iota(jnp.int32, (SUBLANES, LANES), 1)).astype(F32)

    def one_seq(bi, rows):
        gates = gt_ref[rows, :] + bg_ref[...]
        lf = _log_sigmoid(gates)
        bcum = _dot(tril, lf, precision=hi)
        g_rows = _nt_dot(sel, gates, precision=hi)
        b_rows = _nt_dot(sel, bcum, precision=hi)
        for h in range(heads):
            q = qkv_ref[rows, h * dk:(h + 1) * dk]
            k = qkv_ref[rows, (heads + h) * dk:(heads + h + 1) * dk]
            v = qkv_ref[rows, 2 * heads * dk + h * dv:2 * heads * dk + (h + 1) * dv]
            qf, kf = q.astype(F32), k.astype(F32)
            qc, kc, vc = q.astype(cd), k.astype(cd), v.astype(cd)
            ig_col = gates[:, h:h + 1]
            ig_row = g_rows[h:h + 1, :]
            b_col = bcum[:, heads + h:heads + h + 1]
            b_row = b_rows[heads + h:heads + h + 1, :]
            b_end = bcum[L - 1:L, heads + h:heads + h + 1]
            c_old = c_ref[bi, h]
            n_old = n_ref[bi, h]
            m_old = m_ref[bi, h][:, :1]

            dlog = jnp.where(causal, b_col - b_row + ig_row, NEG_INF)
            gcar = b_col + m_old
            m_t = jnp.maximum(gcar, jnp.max(dlog, axis=-1, keepdims=True))
            s = _nt_dot(qc, kc) * jnp.exp(dlog - m_t)
            dec = jnp.exp(gcar - m_t)
            num = _dot(s.astype(cd), vc) + dec * _dot(qc, c_old.astype(cd))
            den = (jnp.sum(s, axis=-1, keepdims=True)
                   + dec * jnp.sum(qf * n_old, axis=-1, keepdims=True))
            hh = num / jnp.maximum(jnp.abs(den), jnp.exp(-m_t))

            a_row = b_end - b_row + ig_row
            a_col = b_end - b_col + ig_col
            m_new = jnp.maximum(b_end + m_old, jnp.max(a_row, axis=-1, keepdims=True))
            wk = jnp.exp(a_col - m_new)
            decay = jnp.exp(b_end + m_old - m_new)
            kw = kf * wk
            c_ref[bi, h] = decay * c_old + _tn_dot(kw.astype(cd), vc)
            n_ref[bi, h] = decay * n_old + jnp.sum(kw, axis=0, keepdims=True)
            m_ref[bi, h] = jnp.broadcast_to(m_new, (1, LANES))

            hs = (hh * lax.rsqrt(jnp.mean(hh * hh, axis=-1, keepdims=True) + EPS)
                  * gh_ref[:, h * dv:(h + 1) * dv])
            og = o_ref[rows, h * dv:(h + 1) * dv]
            y_ref[rows, h * dv:(h + 1) * dv] = (hs * jax.nn.sigmoid(og)).astype(y_ref.dtype)

    if seqs == 1:
        one_seq(0, slice(None))
    else:
        def body(bi, carry):
            one_seq(bi, pl.ds(pl.multiple_of(bi * L, L), L))
            return carry
        lax.fori_loop(0, seqs, body, 0)


def mlstm_cell(qkv, og, gates, b_gate, g_head, c0, n0, m0, seq_len, seqs_per_step):
    t = qkv.shape[0]
    b, heads, dk, dv = c0.shape
    chunk = M_CHUNK if seq_len % M_CHUNK == 0 else seq_len
    nc = seq_len // chunk
    bt = seqs_per_step
    assert bt == 1 or nc == 1
    rows = bt * chunk
    n4 = n0.reshape(b, heads, 1, dk)
    m4 = jnp.broadcast_to(m0.reshape(b, heads, 1, 1), (b, heads, 1, LANES))
    bg = jnp.pad(b_gate.astype(F32), (0, LANES - 2 * heads)).reshape(1, LANES)
    y_dtype = BF if chunk >= 16 else F32
    tok = lambda i, c: (i * nc + c, 0)
    st = lambda i, c: (i, 0, 0, 0)
    y, c_new, n_new, m_new = pl.pallas_call(
        functools.partial(_mlstm_kernel, heads=heads, dk=dk, dv=dv, chunk=chunk, seqs=bt),
        grid=(b // bt, nc),
        in_specs=[pl.BlockSpec((rows, qkv.shape[1]), tok),
                  pl.BlockSpec((rows, heads * dv), tok),
                  pl.BlockSpec((rows, LANES), tok),
                  pl.BlockSpec((1, LANES), lambda i, c: (0, 0)),
                  pl.BlockSpec((1, heads * dv), lambda i, c: (0, 0)),
                  pl.BlockSpec((bt, heads, dk, dv), st),
                  pl.BlockSpec((bt, heads, 1, dk), st),
                  pl.BlockSpec((bt, heads, 1, LANES), st)],
        out_specs=[pl.BlockSpec((rows, heads * dv), tok),
                   pl.BlockSpec((bt, heads, dk, dv), st),
                   pl.BlockSpec((bt, heads, 1, dk), st),
                   pl.BlockSpec((bt, heads, 1, LANES), st)],
        out_shape=[jax.ShapeDtypeStruct((t, heads * dv), y_dtype),
                   jax.ShapeDtypeStruct((b, heads, dk, dv), F32),
                   jax.ShapeDtypeStruct((b, heads, 1, dk), F32),
                   jax.ShapeDtypeStruct((b, heads, 1, LANES), F32)],
        compiler_params=_cparams(("parallel", "arbitrary")),
        name="mlstm_cell",
    )(qkv, og, gates, bg, g_head.reshape(1, heads * dv).astype(F32), c0, n4, m4)
    return y, c_new, n_new.reshape(b, heads, dk), m_new[:, :, 0, 0]


def _lambda(lam_ref, lam_init):
    lp = lam_ref[...]
    a = jnp.sum(lp[0:1] * lp[1:2], axis=-1, keepdims=True)
    b = jnp.sum(lp[2:3] * lp[3:4], axis=-1, keepdims=True)
    return jnp.exp(a) - jnp.exp(b) + lam_init


def _sub_norm(o, gsub_ref, lam_init):
    y = o * lax.rsqrt(jnp.mean(o * o, axis=-1, keepdims=True) + EPS) * gsub_ref[...]
    return y * (1.0 - lam_init)


def _flash_kernel(q_ref, k_ref, v_ref, lam_ref, gs_ref, o_ref, qm_scr, m_scr, l_scr, acc_scr,
                  *, dh, tq, tk, lam_init):
    qi = pl.program_id(2)
    ki = pl.program_id(3)

    @pl.when(ki == 0)
    def _():
        q = q_ref[...] * (dh ** -0.5)
        lane = lax.broadcasted_iota(jnp.int32, q.shape, 1)
        qm_scr[0] = jnp.where(lane < dh, q, jnp.zeros_like(q))
        qm_scr[1] = jnp.where(lane >= dh, q, jnp.zeros_like(q))
        m_scr[...] = jnp.full_like(m_scr, NEG_INF)
        l_scr[...] = jnp.zeros_like(l_scr)
        acc_scr[...] = jnp.zeros_like(acc_scr)

    @pl.when(ki <= qi)
    def _():
        k = k_ref[...]
        v = v_ref[...]
        row = qi * tq + lax.broadcasted_iota(jnp.int32, (tq, tk), 0)
        col = ki * tk + lax.broadcasted_iota(jnp.int32, (tq, tk), 1)
        keep = row >= col
        for c in range(2):
            s = jnp.where(keep, _nt_dot(qm_scr[c], k), NEG_INF)
            m_prev = m_scr[c]
            m_new = jnp.maximum(m_prev, jnp.max(s, axis=-1, keepdims=True))
            alpha = jnp.exp(m_prev - m_new)
            p = jnp.exp(s - m_new)
            l_scr[c] = alpha * l_scr[c] + jnp.sum(p, axis=-1, keepdims=True)
            acc_scr[c] = alpha * acc_scr[c] + _dot(p.astype(BF), v)
            m_scr[c] = m_new

    @pl.when(ki == qi)
    def _():
        lam = _lambda(lam_ref, lam_init)
        o = acc_scr[0] / l_scr[0] - lam * (acc_scr[1] / l_scr[1])
        o_ref[...] = _sub_norm(o, gs_ref, lam_init).astype(o_ref.dtype)


def flash_diff_attention(q, k, v, lam_p, g_sub, n_seq, heads, lam_init, tq):
    t = q.shape[0]
    dv = v.shape[1] // heads
    dh = q.shape[1] // (2 * heads)
    assert 2 * dh == LANES and dv == LANES
    seq = t // n_seq
    tk = tq
    nq = seq // tq
    return pl.pallas_call(
        functools.partial(_flash_kernel, dh=dh, tq=tq, tk=tk, lam_init=lam_init),
        grid=(n_seq, heads, nq, nq),
        in_specs=[pl.BlockSpec((tq, LANES), lambda b, h, i, j: (b * nq + i, h)),
                  pl.BlockSpec((tk, LANES), lambda b, h, i, j: (b * nq + jnp.minimum(i, j), h)),
                  pl.BlockSpec((tk, LANES), lambda b, h, i, j: (b * nq + jnp.minimum(i, j), h)),
                  pl.BlockSpec(lam_p.shape, lambda b, h, i, j: (0, 0)),
                  pl.BlockSpec((1, dv), lambda b, h, i, j: (0, 0))],
        out_specs=pl.BlockSpec((tq, LANES), lambda b, h, i, j: (b * nq + i, h)),
        out_shape=jax.ShapeDtypeStruct((t, heads * dv), BF),
        scratch_shapes=[pltpu.VMEM((2, tq, LANES), BF),
                        pltpu.VMEM((2, tq, 1), F32),
                        pltpu.VMEM((2, tq, 1), F32),
                        pltpu.VMEM((2, tq, LANES), F32)],
        compiler_params=_cparams(("parallel", "parallel", "arbitrary", "arbitrary")),
        name="flash_diff_attention",
    )(q, k, v, lam_p.astype(F32), g_sub.reshape(1, dv).astype(F32))


def _paged_kernel(pt_ref, q_ref, kn_ref, vn_ref, lam_ref, gs_ref, *rest,
                  heads, dh, n_new, pages, lam_init):
    del pt_ref
    kp_refs = rest[:pages]
    vp_refs = rest[pages:2 * pages]
    o_ref, qbd_scr, m_scr, l_scr, acc_scr = rest[2 * pages:]
    step = pl.program_id(1)
    width = heads * 2 * dh
    nrow = 2 * heads * n_new

    @pl.when(step == 0)
    def _():
        q = q_ref[...] * (dh ** -0.5)
        qt = jnp.concatenate([q] * (2 * heads), axis=0)
        r = lax.broadcasted_iota(jnp.int32, (nrow, width), 0)
        cblk = lax.broadcasted_iota(jnp.int32, (nrow, width), 1) // dh
        r_map = r // (heads * n_new)
        r_head = (r // n_new) % heads
        qbd_scr[...] = jnp.where(cblk == r_head * 2 + r_map, qt, 0.0).astype(BF)
        m_scr[...] = jnp.full_like(m_scr, NEG_INF)
        l_scr[...] = jnp.zeros_like(l_scr)
        acc_scr[...] = jnp.zeros_like(acc_scr)

    def absorb(s, v):
        m_prev = m_scr[...]
        m_new = jnp.maximum(m_prev, jnp.max(s, axis=-1, keepdims=True))
        alpha = jnp.exp(m_prev - m_new)
        p = jnp.exp(s - m_new)
        l_scr[...] = alpha * l_scr[...] + jnp.sum(p, axis=-1, keepdims=True)
        acc_scr[...] = alpha * acc_scr[...] + _dot(p.astype(BF), v)
        m_scr[...] = m_new

    qbd = qbd_scr[...]
    for p in range(pages):
        absorb(_nt_dot(qbd, kp_refs[p][0, 0].astype(BF)), vp_refs[p][0, 0].astype(BF))

    @pl.when(step == pl.num_programs(1) - 1)
    def _():
        pad = jnp.zeros((2 * SUBLANES - n_new, width), F32)
        kn = jnp.concatenate([kn_ref[...], pad], axis=0).astype(BF)
        vn = jnp.concatenate([vn_ref[...], pad], axis=0).astype(BF)
        s = _nt_dot(qbd_scr[...], kn)
        tok = lax.broadcasted_iota(jnp.int32, s.shape, 0) % n_new
        key = lax.broadcasted_iota(jnp.int32, s.shape, 1)
        absorb(jnp.where(key <= tok, s, NEG_INF), vn)

        lam = _lambda(lam_ref, lam_init)
        inv_l = 1.0 / l_scr[...]
        half = heads * n_new
        for h in range(heads):
            r0 = h * n_new
            cols = slice(h * 2 * dh, (h + 1) * 2 * dh)
            o0 = acc_scr[r0:r0 + n_new, cols] * inv_l[r0:r0 + n_new]
            o1 = acc_scr[half + r0:half + r0 + n_new, cols] * inv_l[half + r0:half + r0 + n_new]
            o_ref[:, cols] = _sub_norm(o0 - lam * o1, gs_ref, lam_init)


def paged_diff_attention(q, k_new, v_new, cache_k, cache_v, layer, page_table, lam_p, g_sub,
                         heads, lam_init, pages_per_step):
    t, width = q.shape
    b, n_pages = page_table.shape
    n_new = t // b
    dh = width // (2 * heads)
    page = cache_k.shape[2]
    assert n_new == SUBLANES and 2 * dh == LANES and n_pages % pages_per_step == 0
    pp = pages_per_step
    nrow = 2 * heads * n_new
    tok = pl.BlockSpec((n_new, width), lambda i, s, pt: (i, 0))

    def page_spec(p):
        return pl.BlockSpec((1, 1, page, width), lambda i, s, pt: (layer, pt[i, s * pp + p], 0, 0))

    grid_spec = pltpu.PrefetchScalarGridSpec(
        num_scalar_prefetch=1,
        grid=(b, n_pages // pp),
        in_specs=[tok, tok, tok,
                  pl.BlockSpec(lam_p.shape, lambda i, s, pt: (0, 0)),
                  pl.BlockSpec((1, 2 * dh), lambda i, s, pt: (0, 0))]
                 + [page_spec(p) for p in range(pp)] * 2,
        out_specs=tok,
        scratch_shapes=[pltpu.VMEM((nrow, width), BF),
                        pltpu.VMEM((nrow, 1), F32),
                        pltpu.VMEM((nrow, 1), F32),
                        pltpu.VMEM((nrow, width), F32)],
    )
    return pl.pallas_call(
        functools.partial(_paged_kernel, heads=heads, dh=dh, n_new=n_new, pages=pp,
                          lam_init=lam_init),
        grid_spec=grid_spec,
        out_shape=jax.ShapeDtypeStruct((t, width), F32),
        compiler_params=_cparams(("parallel", "arbitrary")),
        name="paged_diff_attention",
    )(page_table, q, k_new, v_new, lam_p.astype(F32), g_sub.reshape(1, 2 * dh).astype(F32),
      *([cache_k] * pp), *([cache_v] * pp))


def _trunk(x, rows_per_seq, mods, state, paged, weights, cfg):
    (g_norm_mix, g_norm_ffn, w_in_m, b_gate_m, g_head_m, w_out_m,
     w_in_d, g_q_d, g_k_d, lam_d, g_sub_d, w_out_d, w_gu_f, w_down_f,
     w_router, b_router, w_gu_e, w_down_e) = weights
    t, d = x.shape
    n_seq = t // rows_per_seq
    depth = g_norm_mix.shape[0]
    tm = cfg["tm"]
    heads_m = b_gate_m.shape[1] // 2
    dk, dv = state[0].shape[3], state[0].shape[4]
    qk_w = heads_m * dk
    heads_a, dh = cfg["heads_a"], cfg["dh"]
    new_state, new_kv = [], []
    for i in range(depth):
        j = i // 2
        mod = mods[i]
        if i % 2 == 0:
            col_scale = jnp.concatenate([jnp.ones((qk_w,), F32), jnp.full((qk_w,), dk ** -0.5, F32),
                                         jnp.ones((heads_m * dv,), F32)]).reshape(1, -1)
            small = rows_per_seq % M_CHUNK != 0
            (qkv,) = norm_proj(x, rows_per_seq, mod, 1, 0, g_norm_mix[i], w_in_m[j], 0,
                               2 * qk_w + heads_m * dv, [F32 if small else BF], tm, 512,
                               col_scale=col_scale, name="mlstm_qkv")
            (og,) = norm_proj(x, rows_per_seq, mod, 1, 0, g_norm_mix[i], w_in_m[j],
                              2 * qk_w + heads_m * dv, heads_m * dv, [F32], tm, 512, name="mlstm_o")
            w_gate = jnp.pad(w_in_m[j][:, 2 * qk_w + 2 * heads_m * dv:], ((0, 0), (0, LANES - 2 * heads_m)))
            (gates,) = norm_proj(x, rows_per_seq, mod, 1, 0, g_norm_mix[i], w_gate, 0, LANES,
                                 [F32], tm, LANES, name="mlstm_gates")
            c0, n0, m0 = state[0][j], state[1][j], state[2][j]
            y, c1, n1, m1 = mlstm_cell(qkv, og, gates, b_gate_m[j], g_head_m[j], c0, n0, m0,
                                       rows_per_seq, cfg["mlstm_seqs"] if small else 1)
            new_state.append((c1, n1, m1))
            w_out = w_out_m
        else:
            lam_init = 0.8 - 0.6 * math.exp(-0.3 * i)
            (q,) = norm_proj(x, rows_per_seq, mod, 1, 0, g_norm_mix[i], w_in_d[j], 0, d,
                             [BF if paged is None else F32], tm, 256, seg_gain=g_q_d[j], name="attn_q")
            k32, kbf = norm_proj(x, rows_per_seq, mod, 1, 0, g_norm_mix[i], w_in_d[j], d, d,
                                 [F32, BF], tm, 256, seg_gain=g_k_d[j], name="attn_k")
            v32, vbf = norm_proj(x, rows_per_seq, mod, 1, 0, g_norm_mix[i], w_in_d[j], 2 * d, d,
                                 [F32, BF], tm, 512, name="attn_v")
            if paged is None:
                y = flash_diff_attention(q, kbf, vbf, lam_d[j], g_sub_d[j], n_seq, heads_a,
                                         lam_init, cfg["tq"])
            else:
                cache_k, cache_v, page_table = paged
                y = paged_diff_attention(q, k32, v32, cache_k, cache_v, j, page_table, lam_d[j],
                                         g_sub_d[j], heads_a, lam_init, cfg["pages_per_step"])
            new_kv.append((k32, v32))
            w_out = w_out_d
        x = down_residual(y[None], w_out[:, None], j, x, rows_per_seq, mod, 2, tm, 512,
                          name="mixer_out")
        if i % 2 == 0:
            act = swiglu_up(x, rows_per_seq, mod, 4, 3, g_norm_ffn[i], w_gu_f[:, None], j, tm, 256)
            x = down_residual(act, w_down_f[:, None], j, x, rows_per_seq, mod, 5, tm, 512,
                              name="ffn_down")
        else:
            dg = router_gates(x, rows_per_seq, mod, 4, 3, g_norm_ffn[i], w_router[j], b_router[j], tm)
            act = swiglu_up(x, rows_per_seq, mod, 4, 3, g_norm_ffn[i], w_gu_e, j, tm, 256)
            x = down_residual(act, w_down_e, j, x, rows_per_seq, mod, 5, tm, 512, dense_gate=dg,
                              name="moe_down")
    return x, new_state, new_kv


def kernel(x_prompt, x_sample, c_prompt, c_sample, state_C, state_n, state_m, cache_k, cache_v, page_table, w_ada, b_ada, g_norm_mix, g_norm_ffn, w_in_m, b_gate_m, g_head_m, w_out_m, w_in_d, g_q_d, g_k_d, lam_d, g_sub_d, w_out_d, w_gu_f, w_down_f, w_router, b_router, w_gu_e, w_down_e):
    bp, seq, d = x_prompt.shape
    bs, dec_seq, _ = x_sample.shape
    depth = w_ada.shape[0]
    n_ml, _, heads_m, dk, dv = state_C.shape
    n_diff, n_pool, page, heads_a, _, dh = cache_k.shape
    weights = (g_norm_mix, g_norm_ffn, w_in_m, b_gate_m, g_head_m, w_out_m,
               w_in_d, g_q_d, g_k_d, lam_d, g_sub_d, w_out_d, w_gu_f, w_down_f,
               w_router, b_router, w_gu_e, w_down_e)

    n_c = bp + bs
    n_c_pad = -(-n_c // SUBLANES) * SUBLANES
    c_all = jnp.pad(jnp.concatenate([c_prompt, c_sample], axis=0), ((0, n_c_pad - n_c), (0, 0)))
    mod_all = ada_mod(c_all, w_ada, b_ada, tn=min(6 * d, 1536))
    mods_p = [mod_all[i, :bp].reshape(bp, 1, 6 * d) for i in range(depth)]
    mods_s = [mod_all[i, bp:n_c].reshape(bs, 1, 6 * d) for i in range(depth)]

    cfg_p = dict(tm=min(1024, seq), tq=min(512, seq), heads_a=heads_a, dh=dh)
    cfg_s = dict(tm=min(1024, bs * dec_seq), heads_a=heads_a, dh=dh, mlstm_seqs=min(8, bs),
                 pages_per_step=min(4, page_table.shape[1]))

    zeros = (jnp.zeros((n_ml, bp, heads_m, dk, dv), F32), jnp.zeros((n_ml, bp, heads_m, dk), F32),
             jnp.zeros((n_ml, bp, heads_m), F32))
    y_p, st_p, kv_p = _trunk(x_prompt.reshape(bp * seq, d), seq, mods_p, zeros, None, weights, cfg_p)
    paged = (cache_k.reshape(n_diff, n_pool, page, heads_a * 2 * dh),
             cache_v.reshape(n_diff, n_pool, page, heads_a * 2 * dh), page_table)
    y_s, st_s, kv_s = _trunk(x_sample.reshape(bs * dec_seq, d), dec_seq, mods_s,
                             (state_C, state_n, state_m), paged, weights, cfg_s)

    def stack(items, idx, shape):
        return jnp.stack([it[idx] for it in items]).reshape(shape)

    return (y_p.reshape(bp, seq, d), y_s.reshape(bs, dec_seq, d),
            stack(st_p, 0, (n_ml, bp, heads_m, dk, dv)), stack(st_p, 1, (n_ml, bp, heads_m, dk)),
            stack(st_p, 2, (n_ml, bp, heads_m)),
            stack(st_s, 0, (n_ml, bs, heads_m, dk, dv)), stack(st_s, 1, (n_ml, bs, heads_m, dk)),
            stack(st_s, 2, (n_ml, bs, heads_m)),
            stack(kv_p, 0, (n_diff, bp, seq, heads_a, 2, dh)),
            stack(kv_p, 1, (n_diff, bp, seq, heads_a, 2 * dh)),
            stack(kv_s, 0, (n_diff, bs, dec_seq, heads_a, 2, dh)),
            stack(kv_s, 1, (n_diff, bs, dec_seq, heads_a, 2 * dh)))
```

```python
import functools
import math

import jax
import jax.numpy as jnp
from jax import lax
from jax.experimental import pallas as pl
from jax.experimental.pallas import tpu as pltpu

BF = jnp.bfloat16
F32 = jnp.float32
EPS = 1e-6
M_CHUNK = 128
LANES = 128
SUBLANES = 8
VMEM_LIMIT = 56 * 1024 * 1024
NEG_INF = float("-inf")


def _cparams(sem):
    return pltpu.CompilerParams(dimension_semantics=sem, vmem_limit_bytes=VMEM_LIMIT)


def _tile(n, pref):
    if n <= pref:
        return n
    t = pref - pref % LANES
    while n % t:
        t -= LANES
    return t


def _nt_dot(a, b, **kw):
    return lax.dot_general(a, b, (((1,), (1,)), ((), ())), preferred_element_type=F32, **kw)


def _tn_dot(a, b, **kw):
    return lax.dot_general(a, b, (((0,), (0,)), ((), ())), preferred_element_type=F32, **kw)


def _dot(a, b, **kw):
    return jnp.dot(a, b, preferred_element_type=F32, **kw)


def _ada_kernel(c_ref, w_ref, b_ref, o_ref):
    c = c_ref[...]
    a = (c * jax.nn.sigmoid(c)).astype(BF)
    o_ref[0] = _dot(a, w_ref[0].astype(BF)) + b_ref[0]


def ada_mod(c_all, w_ada, b_ada, tn):
    depth, d, n = w_ada.shape
    bp = c_all.shape[0]
    return pl.pallas_call(
        _ada_kernel,
        grid=(depth, n // tn),
        in_specs=[pl.BlockSpec((bp, d), lambda i, j: (0, 0)),
                  pl.BlockSpec((1, d, tn), lambda i, j: (i, 0, j)),
                  pl.BlockSpec((1, 1, tn), lambda i, j: (i, 0, j))],
        out_specs=pl.BlockSpec((1, bp, tn), lambda i, j: (i, 0, j)),
        out_shape=jax.ShapeDtypeStruct((depth, bp, n), F32),
        compiler_params=_cparams(("arbitrary", "arbitrary")),
        name="ada_mod",
    )(c_all, w_ada, b_ada.reshape(depth, 1, n))


def _mod_spec(rows_per_seq, tm, width, col_block):
    if rows_per_seq >= tm:
        npb = rows_per_seq // tm
        return pl.BlockSpec((1, 1, width), lambda *g: (g[0] // npb, 0, col_block(*g)))
    gb = tm // rows_per_seq
    return pl.BlockSpec((gb, 1, width), lambda *g: (g[0], 0, col_block(*g)))


def _rows(m_ref, tm):
    m = m_ref[...]
    gb, _, w = m.shape
    if gb == 1:
        return m[0]
    return jnp.broadcast_to(m, (gb, tm // gb, w)).reshape(tm, w)


def _modulated(x_ref, g_ref, sc_ref, sh_ref):
    x = x_ref[...]
    tm = x.shape[0]
    y = x * lax.rsqrt(jnp.mean(x * x, axis=-1, keepdims=True) + EPS) * g_ref[...]
    return y * (1.0 + _rows(sc_ref, tm)) + _rows(sh_ref, tm)


def _proj_kernel(x_ref, g_ref, sc_ref, sh_ref, w_ref, cs_ref, *rest):
    outs, h_scr = rest[:-1], rest[-1]

    @pl.when(pl.program_id(1) == 0)
    def _():
        h_scr[...] = _modulated(x_ref, g_ref, sc_ref, sh_ref).astype(BF)

    acc = _dot(h_scr[...], w_ref[...].astype(BF)) * cs_ref[...]
    for o in outs:
        o[...] = acc.astype(o.dtype)


def _segnorm_kernel(x_ref, g_ref, sc_ref, sh_ref, w_ref, gv_ref, seg_ref, *rest, seg, transposed):
    outs, h_scr = rest[:-1], rest[-1]

    @pl.when(pl.program_id(1) == 0)
    def _():
        h_scr[...] = _modulated(x_ref, g_ref, sc_ref, sh_ref).astype(BF)

    acc = _dot(h_scr[...], w_ref[...].astype(BF))
    sq = acc * acc
    hi = sq.astype(BF)
    lo = (sq - hi.astype(F32)).astype(BF)
    ssq = _dot(hi, seg_ref[...]) + _dot(lo, seg_ref[...])
    y = acc * lax.rsqrt(ssq * (1.0 / seg) + EPS) * gv_ref[...]
    if transposed:
        yt = y.T
        for o in outs:
            o[0] = yt.astype(o.dtype)
    else:
        for o in outs:
            o[...] = y.astype(o.dtype)


def norm_proj(x, rows_per_seq, mod, sc_chunk, sh_chunk, g, w, col_off, n, out_dtypes, tm, tn,
              col_scale=None, seg_gain=None, transposed=False, name="norm_proj"):
    t, d = x.shape
    tn = _tile(math.gcd(n, col_off) if col_off else n, tn)
    x_spec = pl.BlockSpec((tm, d), lambda i, j: (i, 0))
    g_spec = pl.BlockSpec((1, d), lambda i, j: (0, 0))
    sc_spec = _mod_spec(rows_per_seq, tm, d, lambda i, j: sc_chunk)
    sh_spec = _mod_spec(rows_per_seq, tm, d, lambda i, j: sh_chunk)
    off = col_off // tn
    w_spec = pl.BlockSpec((d, tn), lambda i, j: (0, off + j))
    out_specs = [pl.BlockSpec((tm, tn), lambda i, j: (i, j)) for _ in out_dtypes]
    out_shape = [jax.ShapeDtypeStruct((t, n), dt) for dt in out_dtypes]
    if seg_gain is None:
        if col_scale is None:
            col_scale = jnp.ones((1, n), F32)
        kern = _proj_kernel
        extra = [col_scale]
        extra_specs = [pl.BlockSpec((1, tn), lambda i, j: (0, j))]
    else:
        seg = seg_gain.shape[0]
        kern = functools.partial(_segnorm_kernel, seg=seg, transposed=transposed)
        if transposed:
            npb = rows_per_seq // tm
            out_specs = [pl.BlockSpec((1, tn, tm), lambda i, j: (i // npb, j, i % npb))
                         for _ in out_dtypes]
            out_shape = [jax.ShapeDtypeStruct((t // rows_per_seq, n, rows_per_seq), dt)
                         for dt in out_dtypes]
        gv = jnp.tile(seg_gain.astype(F32), tn // seg).reshape(1, tn)
        ids = jnp.arange(tn) // seg
        seg_mat = (ids[:, None] == ids[None, :]).astype(BF)
        extra = [gv, seg_mat]
        extra_specs = [pl.BlockSpec((1, tn), lambda i, j: (0, 0)),
                       pl.BlockSpec((tn, tn), lambda i, j: (0, 0))]
    res = pl.pallas_call(
        kern,
        grid=(t // tm, n // tn),
        in_specs=[x_spec, g_spec, sc_spec, sh_spec, w_spec] + extra_specs,
        out_specs=out_specs,
        out_shape=out_shape,
        scratch_shapes=[pltpu.VMEM((tm, d), BF)],
        compiler_params=_cparams(("parallel", "arbitrary")),
        name=name,
    )(x, g.reshape(1, d), mod, mod, w, *extra)
    return res


def _router_kernel(x_ref, g_ref, sc_ref, sh_ref, w_ref, b_ref, tri_ref,
                   h_ref, o_ref, pos_ref, cnt_ref, *, n_experts):
    h = _modulated(x_ref, g_ref, sc_ref, sh_ref)
    h_ref[...] = h.astype(h_ref.dtype)
    logits = _dot(h, w_ref[...], precision=lax.Precision.HIGHEST) + b_ref[...]
    lane = lax.broadcasted_iota(jnp.int32, logits.shape, 1).astype(F32)
    big = float(LANES)
    l1 = jnp.where(lane < n_experts, logits, NEG_INF)
    m1 = jnp.max(l1, axis=-1, keepdims=True)
    i1 = jnp.min(jnp.where(l1 == m1, lane, big), axis=-1, keepdims=True)
    l2 = jnp.where(lane == i1, NEG_INF, l1)
    m2 = jnp.max(l2, axis=-1, keepdims=True)
    i2 = jnp.min(jnp.where(l2 == m2, lane, big), axis=-1, keepdims=True)
    e2 = jnp.exp(m2 - m1)
    w1 = 1.0 / (1.0 + e2)
    w2 = e2 / (1.0 + e2)
    gate = jnp.where(lane == i1, w1, 0.0) + jnp.where(lane == i2, w2, 0.0)
    o_ref[...] = gate
    routed = jnp.where(gate > 0.0, 1.0, 0.0)
    pos_ref[...] = _dot(tri_ref[...], routed.astype(BF))
    cnt_ref[0] = jnp.sum(routed, axis=0, keepdims=True)


def router_gates(x, rows_per_seq, mod, sc_chunk, sh_chunk, g, w_router, b_router, tm):
    t, d = x.shape
    e = w_router.shape[1]
    w_pad = jnp.pad(w_router, ((0, 0), (0, LANES - e)))
    b_pad = jnp.pad(b_router, (0, LANES - e)).reshape(1, LANES)
    idx = jnp.arange(tm)
    strict_lower = (idx[:, None] > idx[None, :]).astype(BF)
    tok = pl.BlockSpec((tm, LANES), lambda i: (i, 0))
    return pl.pallas_call(
        functools.partial(_router_kernel, n_experts=e),
        grid=(t // tm,),
        in_specs=[pl.BlockSpec((tm, d), lambda i: (i, 0)),
                  pl.BlockSpec((1, d), lambda i: (0, 0)),
                  _mod_spec(rows_per_seq, tm, d, lambda i: sc_chunk),
                  _mod_spec(rows_per_seq, tm, d, lambda i: sh_chunk),
                  pl.BlockSpec((d, LANES), lambda i: (0, 0)),
                  pl.BlockSpec((1, LANES), lambda i: (0, 0)),
                  pl.BlockSpec((tm, tm), lambda i: (0, 0))],
        out_specs=[pl.BlockSpec((tm, d), lambda i: (i, 0)), tok, tok,
                   pl.BlockSpec((1, 1, LANES), lambda i: (i, 0, 0))],
        out_shape=[jax.ShapeDtypeStruct((t, d), BF),
                   jax.ShapeDtypeStruct((t, LANES), F32),
                   jax.ShapeDtypeStruct((t, LANES), F32),
                   jax.ShapeDtypeStruct((t // tm, 1, LANES), F32)],
        compiler_params=_cparams(("parallel",)),
        name="router",
    )(x, g.reshape(1, d), mod, mod, w_pad, b_pad, strict_lower)


def _moe_kernel(cnt_ref, h_ref, dg_ref, pos_ref, wg_ref, wu_ref, wd_ref, x_ref, gt_ref, o_ref,
                xs_scr, y_scr, *, n_experts, br):
    ti, e, fc = pl.program_id(0), pl.program_id(1), pl.program_id(2)
    last_fc = pl.num_programs(2) - 1
    tm = h_ref.shape[0]

    @pl.when((e == 0) & (fc == 0))
    def _():
        o_ref[...] = jnp.zeros_like(o_ref)

    sel = lax.broadcasted_iota(jnp.int32, (tm, LANES), 1) == e
    ge = jnp.sum(jnp.where(sel, dg_ref[...], 0.0), axis=-1, keepdims=True)
    pos = jnp.sum(jnp.where(sel, pos_ref[...], 0.0), axis=-1, keepdims=True)
    routed = ge > 0.0
    n_blocks = (cnt_ref[ti * n_experts + e] + br - 1) // br

    def block(rb, carry):
        r0 = pl.multiple_of(rb * br, br)
        rows = pl.ds(r0, br)
        slot = (r0 + lax.broadcasted_iota(jnp.int32, (tm, br), 1)).astype(F32)
        onehot = jnp.where((pos == slot) & routed, 1.0, 0.0).astype(BF)

        @pl.when(fc == 0)
        def _():
            xs_scr[rows, :] = _tn_dot(onehot, h_ref[...]).astype(BF)

        xs = xs_scr[rows, :]
        a = _dot(xs, wg_ref[0, 0])
        u = _dot(xs, wu_ref[0, 0])
        y = _dot((a * jax.nn.sigmoid(a) * u).astype(BF), wd_ref[0, 0])

        @pl.when(fc == 0)
        def _():
            y_scr[rows, :] = y

        @pl.when(fc > 0)
        def _():
            y_scr[rows, :] += y

        @pl.when(fc == last_fc)
        def _():
            yt = y_scr[rows, :]
            y_hi = yt.astype(BF)
            y_lo = (yt - y_hi.astype(F32)).astype(BF)
            o_ref[...] += ge * (_dot(onehot, y_hi) + _dot(onehot, y_lo))
        return carry

    lax.fori_loop(0, n_blocks, block, 0)

    @pl.when((e == n_experts - 1) & (fc == last_fc))
    def _():
        o_ref[...] = x_ref[...] + _rows(gt_ref, tm) * o_ref[...]


def moe_ffn(h, dense_gate, rank, counts, w_gu, w_down, layer, x, rows_per_seq, mod, gt_chunk, tm,
            ff_chunks, block_rows):
    t, d = x.shape
    _, e, _, f2 = w_gu.shape
    f = f2 // 2
    fcw = f // ff_chunks
    tok = pl.BlockSpec((tm, LANES), lambda i, k, c, cnt: (i, 0))
    grid_spec = pltpu.PrefetchScalarGridSpec(
        num_scalar_prefetch=1,
        grid=(t // tm, e, ff_chunks),
        in_specs=[pl.BlockSpec((tm, d), lambda i, k, c, cnt: (i, 0)), tok, tok,
                  pl.BlockSpec((1, 1, d, fcw), lambda i, k, c, cnt: (layer, k, 0, c)),
                  pl.BlockSpec((1, 1, d, fcw), lambda i, k, c, cnt: (layer, k, 0, ff_chunks + c)),
                  pl.BlockSpec((1, 1, fcw, d), lambda i, k, c, cnt: (layer, k, c, 0)),
                  pl.BlockSpec((tm, d), lambda i, k, c, cnt: (i, 0)),
                  _mod_spec(rows_per_seq, tm, d, lambda i, k, c, cnt: gt_chunk)],
        out_specs=pl.BlockSpec((tm, d), lambda i, k, c, cnt: (i, 0)),
        scratch_shapes=[pltpu.VMEM((tm, d), BF), pltpu.VMEM((tm, d), F32)],
    )
    return pl.pallas_call(
        functools.partial(_moe_kernel, n_experts=e, br=block_rows),
        grid_spec=grid_spec,
        out_shape=jax.ShapeDtypeStruct((t, d), F32),
        compiler_params=_cparams(("parallel", "arbitrary", "arbitrary")),
        name="moe_ffn",
    )(counts, h, dense_gate, rank, w_gu, w_gu, w_down, x, mod)


def _swiglu_up_kernel(x_ref, g_ref, sc_ref, sh_ref, wg_ref, wu_ref, o_ref, h_scr):
    @pl.when(pl.program_id(1) == 0)
    def _():
        h_scr[...] = _modulated(x_ref, g_ref, sc_ref, sh_ref).astype(BF)

    h = h_scr[...]
    a = _dot(h, wg_ref[0].astype(BF))
    u = _dot(h, wu_ref[0].astype(BF))
    o_ref[...] = (a * jax.nn.sigmoid(a) * u).astype(o_ref.dtype)


def swiglu_up(x, rows_per_seq, mod, sc_chunk, sh_chunk, g, w_gu, layer, tm, tn):
    t, d = x.shape
    f = w_gu.shape[2] // 2
    tn = _tile(f, tn)
    nf = f // tn
    return pl.pallas_call(
        _swiglu_up_kernel,
        grid=(t // tm, nf),
        in_specs=[pl.BlockSpec((tm, d), lambda i, j: (i, 0)),
                  pl.BlockSpec((1, d), lambda i, j: (0, 0)),
                  _mod_spec(rows_per_seq, tm, d, lambda i, j: sc_chunk),
                  _mod_spec(rows_per_seq, tm, d, lambda i, j: sh_chunk),
                  pl.BlockSpec((1, d, tn), lambda i, j: (layer, 0, j)),
                  pl.BlockSpec((1, d, tn), lambda i, j: (layer, 0, nf + j))],
        out_specs=pl.BlockSpec((tm, tn), lambda i, j: (i, j)),
        out_shape=jax.ShapeDtypeStruct((t, f), BF),
        scratch_shapes=[pltpu.VMEM((tm, d), BF)],
        compiler_params=_cparams(("parallel", "arbitrary")),
        name="swiglu_up",
    )(x, g.reshape(1, d), mod, mod, w_gu, w_gu)


def _down_kernel(y_ref, w_ref, x_ref, gt_ref, o_ref):
    p = _dot(y_ref[...].astype(BF), w_ref[0].astype(BF))
    o_ref[...] = x_ref[...] + _rows(gt_ref, x_ref.shape[0]) * p


def down_residual(y, w, layer, x, rows_per_seq, mod, gt_chunk, tm, tn, name="down_residual"):
    t, k = y.shape
    d = x.shape[1]
    tn = _tile(d, tn)
    nd = d // tn
    return pl.pallas_call(
        _down_kernel,
        grid=(t // tm, nd),
        in_specs=[pl.BlockSpec((tm, k), lambda i, j: (i, 0)),
                  pl.BlockSpec((1, k, tn), lambda i, j: (layer, 0, j)),
                  pl.BlockSpec((tm, tn), lambda i, j: (i, j)),
                  _mod_spec(rows_per_seq, tm, tn, lambda i, j: gt_chunk * nd + j)],
        out_specs=pl.BlockSpec((tm, tn), lambda i, j: (i, j)),
        out_shape=jax.ShapeDtypeStruct((t, d), F32),
        compiler_params=_cparams(("parallel", "arbitrary")),
        name=name,
    )(y, w, x, mod)


def _log_sigmoid(x):
    return jnp.minimum(x, 0.0) - jnp.log1p(jnp.exp(-jnp.abs(x)))


def _mlstm_kernel(qkv_ref, o_ref, gt_ref, bg_ref, gh_ref, c0_ref, n0_ref, m0_ref,
                  y_ref, c_ref, n_ref, m_ref, *, heads, dk, dv, chunk, seqs):
    L = chunk
    hi = lax.Precision.HIGHEST
    cd = BF if L >= 16 else F32

    @pl.when(pl.program_id(1) == 0)
    def _():
        c_ref[...] = c0_ref[...]
        n_ref[...] = n0_ref[...]
        m_ref[...] = m0_ref[...]

    row = lax.broadcasted_iota(jnp.int32, (L, L), 0)
    col = lax.broadcasted_iota(jnp.int32, (L, L), 1)
    causal = row >= col
    tril = causal.astype(F32)
    sel = (lax.broadcasted_iota(jnp.int32, (SUBLANES, LANES), 0)
           == lax.broadcasted_iota(jnp.int32, (SUBLANES, LANES), 1)).astype(F32)

    def one_seq(bi, rows):
        gates = gt_ref[rows, :] + bg_ref[...]
        lf = _log_sigmoid(gates)
        bcum = _dot(tril, lf, precision=hi)
        g_rows = _nt_dot(sel, gates, precision=hi)
        b_rows = _nt_dot(sel, bcum, precision=hi)
        for h in range(heads):
            q = qkv_ref[rows, h * dk:(h + 1) * dk]
            k = qkv_ref[rows, (heads + h) * dk:(heads + h + 1) * dk]
            v = qkv_ref[rows, 2 * heads * dk + h * dv:2 * heads * dk + (h + 1) * dv]
            qf, kf = q.astype(F32), k.astype(F32)
            qc, kc, vc = q.astype(cd), k.astype(cd), v.astype(cd)
            ig_col = gates[:, h:h + 1]
            ig_row = g_rows[h:h + 1, :]
            b_col = bcum[:, heads + h:heads + h + 1]
            b_row = b_rows[heads + h:heads + h + 1, :]
            b_end = bcum[L - 1:L, heads + h:heads + h + 1]
            c_old = c_ref[bi, h]
            n_old = n_ref[bi, h]
            m_old = m_ref[bi, h][:, :1]

            dlog = jnp.where(causal, b_col - b_row + ig_row, NEG_INF)
            gcar = b_col + m_old
            m_t = jnp.maximum(gcar, jnp.max(dlog, axis=-1, keepdims=True))
            s = _nt_dot(qc, kc) * jnp.exp(dlog - m_t)
            dec = jnp.exp(gcar - m_t)
            num = _dot(s.astype(cd), vc) + dec * _dot(qc, c_old.astype(cd))
            den = (jnp.sum(s, axis=-1, keepdims=True)
                   + dec * jnp.sum(qf * n_old, axis=-1, keepdims=True))
            hh = num / jnp.maximum(jnp.abs(den), jnp.exp(-m_t))

            a_row = b_end - b_row + ig_row
            a_col = b_end - b_col + ig_col
            m_new = jnp.maximum(b_end + m_old, jnp.max(a_row, axis=-1, keepdims=True))
            wk = jnp.exp(a_col - m_new)
            decay = jnp.exp(b_end + m_old - m_new)
            kw = kf * wk
            c_ref[bi, h] = decay * c_old + _tn_dot(kw.astype(cd), vc)
            n_ref[bi, h] = decay * n_old + jnp.sum(kw, axis=0, keepdims=True)
            m_ref[bi, h] = jnp.broadcast_to(m_new, (1, LANES))

            hs = (hh * lax.rsqrt(jnp.mean(hh * hh, axis=-1, keepdims=True) + EPS)
                  * gh_ref[:, h * dv:(h + 1) * dv])
            og = o_ref[rows, h * dv:(h + 1) * dv]
            y_ref[rows, h * dv:(h + 1) * dv] = (hs * jax.nn.sigmoid(og)).astype(y_ref.dtype)

    if seqs == 1:
        one_seq(0, slice(None))
    else:
        def body(bi, carry):
            one_seq(bi, pl.ds(pl.multiple_of(bi * L, L), L))
            return carry
        lax.fori_loop(0, seqs, body, 0)


def mlstm_cell(qkv, og, gates, b_gate, g_head, c0, n0, m0, seq_len, seqs_per_step):
    t = qkv.shape[0]
    b, heads, dk, dv = c0.shape
    chunk = M_CHUNK if seq_len % M_CHUNK == 0 else seq_len
    nc = seq_len // chunk
    bt = seqs_per_step
    assert bt == 1 or nc == 1
    rows = bt * chunk
    n4 = n0.reshape(b, heads, 1, dk)
    m4 = jnp.broadcast_to(m0.reshape(b, heads, 1, 1), (b, heads, 1, LANES))
    bg = jnp.pad(b_gate.astype(F32), (0, LANES - 2 * heads)).reshape(1, LANES)
    y_dtype = BF if chunk >= 16 else F32
    tok = lambda i, c: (i * nc + c, 0)
    st = lambda i, c: (i, 0, 0, 0)
    y, c_new, n_new, m_new = pl.pallas_call(
        functools.partial(_mlstm_kernel, heads=heads, dk=dk, dv=dv, chunk=chunk, seqs=bt),
        grid=(b // bt, nc),
        in_specs=[pl.BlockSpec((rows, qkv.shape[1]), tok),
                  pl.BlockSpec((rows, heads * dv), tok),
                  pl.BlockSpec((rows, LANES), tok),
                  pl.BlockSpec((1, LANES), lambda i, c: (0, 0)),
                  pl.BlockSpec((1, heads * dv), lambda i, c: (0, 0)),
                  pl.BlockSpec((bt, heads, dk, dv), st),
                  pl.BlockSpec((bt, heads, 1, dk), st),
                  pl.BlockSpec((bt, heads, 1, LANES), st)],
        out_specs=[pl.BlockSpec((rows, heads * dv), tok),
                   pl.BlockSpec((bt, heads, dk, dv), st),
                   pl.BlockSpec((bt, heads, 1, dk), st),
                   pl.BlockSpec((bt, heads, 1, LANES), st)],
        out_shape=[jax.ShapeDtypeStruct((t, heads * dv), y_dtype),
                   jax.ShapeDtypeStruct((b, heads, dk, dv), F32),
                   jax.ShapeDtypeStruct((b, heads, 1, dk), F32),
                   jax.ShapeDtypeStruct((b, heads, 1, LANES), F32)],
        compiler_params=_cparams(("parallel", "arbitrary")),
        name="mlstm_cell",
    )(qkv, og, gates, bg, g_head.reshape(1, heads * dv).astype(F32), c0, n4, m4)
    return y, c_new, n_new.reshape(b, heads, dk), m_new[:, :, 0, 0]


def _lambda(lam_ref, lam_init):
    lp = lam_ref[...]
    a = jnp.sum(lp[0:1] * lp[1:2], axis=-1, keepdims=True)
    b = jnp.sum(lp[2:3] * lp[3:4], axis=-1, keepdims=True)
    return jnp.exp(a) - jnp.exp(b) + lam_init


def _sub_norm(o, gsub_ref, lam_init):
    y = o * lax.rsqrt(jnp.mean(o * o, axis=-1, keepdims=True) + EPS) * gsub_ref[...]
    return y * (1.0 - lam_init)


def _flash_kernel(qi_ref, ki_ref, q_ref, kt_ref, v_ref, bound_ref, lam_ref, gs_ref, o_ref,
                  qm_scr, m_scr, acc_scr, *, dh, tq, tk, lam_init, online):
    pair = pl.program_id(2)
    qi = qi_ref[pair]
    ki = ki_ref[pair]
    dv = LANES

    @pl.when(ki == 0)
    def _():
        q = q_ref[...] * (dh ** -0.5)
        lane = lax.broadcasted_iota(jnp.int32, q.shape, 1)
        qm_scr[0] = jnp.where(lane < dh, q, jnp.zeros_like(q))
        qm_scr[1] = jnp.where(lane >= dh, q, jnp.zeros_like(q))
        m_scr[...] = jnp.full_like(m_scr, NEG_INF)
        acc_scr[...] = jnp.zeros_like(acc_scr)

    def absorb(masked):
        kt = kt_ref[0]
        ones_col = (lax.broadcasted_iota(jnp.int32, (tk, LANES), 1) == 0).astype(BF)
        vx = jnp.concatenate([v_ref[...], ones_col], axis=1)
        if masked:
            row = qi * tq + lax.broadcasted_iota(jnp.int32, (tq, tk), 0)
            col = ki * tk + lax.broadcasted_iota(jnp.int32, (tq, tk), 1)
            keep = row >= col
        for c in range(2):
            s = _dot(qm_scr[c], kt)
            if masked:
                s = jnp.where(keep, s, NEG_INF)
            if online:
                m_prev = m_scr[c]
                m_new = jnp.maximum(m_prev, jnp.max(s, axis=-1, keepdims=True))
                p = jnp.exp(s - m_new)
                acc_scr[c] = jnp.exp(m_prev - m_new) * acc_scr[c] + _dot(p.astype(BF), vx)
                m_scr[c] = m_new
            else:
                p = jnp.exp(s - bound_ref[...])
                acc_scr[c] += _dot(p.astype(BF), vx)

    needs_mask = (ki + 1) * tk - 1 > qi * tq

    @pl.when(needs_mask)
    def _():
        absorb(True)

    @pl.when(jnp.logical_not(needs_mask))
    def _():
        absorb(False)

    @pl.when(ki == (qi * tq + tq - 1) // tk)
    def _():
        lam = _lambda(lam_ref, lam_init)
        a0 = acc_scr[0]
        a1 = acc_scr[1]
        o = a0[:, :dv] / a0[:, dv:dv + 1] - lam * (a1[:, :dv] / a1[:, dv:dv + 1])
        o_ref[...] = _sub_norm(o, gs_ref, lam_init).astype(o_ref.dtype)


SAFE_SCORE_BOUND = 30.0


def flash_diff_attention(q, kt, v, score_bound, lam_p, g_sub, heads, lam_init, tq, tk):
    t = q.shape[0]
    n_seq, _, seq = kt.shape
    dv = v.shape[1] // heads
    dh = q.shape[1] // (2 * heads)
    assert 2 * dh == LANES and dv == LANES
    nq, nk = seq // tq, seq // tk
    pairs = [(i, j) for i in range(nq) for j in range((i * tq + tq - 1) // tk + 1)]
    qi_tab = jnp.asarray([p[0] for p in pairs], jnp.int32)
    ki_tab = jnp.asarray([p[1] for p in pairs], jnp.int32)

    def call(online):
        grid_spec = pltpu.PrefetchScalarGridSpec(
            num_scalar_prefetch=2,
            grid=(n_seq, heads, len(pairs)),
            in_specs=[pl.BlockSpec((tq, LANES), lambda b, h, p, qt, kt_: (b * nq + qt[p], h)),
                      pl.BlockSpec((1, LANES, tk), lambda b, h, p, qt, kt_: (b, h, kt_[p])),
                      pl.BlockSpec((tk, LANES), lambda b, h, p, qt, kt_: (b * nk + kt_[p], h)),
                      pl.BlockSpec((1, 1), lambda b, h, p, qt, kt_: (0, 0)),
                      pl.BlockSpec(lam_p.shape, lambda b, h, p, qt, kt_: (0, 0)),
                      pl.BlockSpec((1, dv), lambda b, h, p, qt, kt_: (0, 0))],
            out_specs=pl.BlockSpec((tq, LANES), lambda b, h, p, qt, kt_: (b * nq + qt[p], h)),
            scratch_shapes=[pltpu.VMEM((2, tq, LANES), BF),
                            pltpu.VMEM((2, tq, 1), F32),
                            pltpu.VMEM((2, tq, dv + LANES), F32)],
        )
        return pl.pallas_call(
            functools.partial(_flash_kernel, dh=dh, tq=tq, tk=tk, lam_init=lam_init, online=online),
            grid_spec=grid_spec,
            out_shape=jax.ShapeDtypeStruct((t, heads * dv), BF),
            compiler_params=_cparams(("parallel", "parallel", "arbitrary")),
            name="flash_online" if online else "flash_bounded",
        )(qi_tab, ki_tab, q, kt, v, score_bound.reshape(1, 1).astype(F32), lam_p.astype(F32),
          g_sub.reshape(1, dv).astype(F32))

    return lax.cond(score_bound < SAFE_SCORE_BOUND, lambda: call(False), lambda: call(True))


def _paged_kernel(pt_ref, q_ref, kn_ref, vn_ref, lam_ref, gs_ref, *rest,
                  heads, dh, n_new, pages, page, lam_init):
    del pt_ref
    kt_refs = rest[:pages]
    v_refs = rest[pages:2 * pages]
    o_ref, s_scr = rest[2 * pages:]
    width = heads * 2 * dh
    nrow = 2 * heads * n_new
    past = pages * page

    q = q_ref[...] * (dh ** -0.5)
    qt = jnp.concatenate([q] * (2 * heads), axis=0)
    r = lax.broadcasted_iota(jnp.int32, (nrow, width), 0)
    cblk = lax.broadcasted_iota(jnp.int32, (nrow, width), 1) // dh
    qbd = jnp.where(cblk == r // n_new, qt, 0.0).astype(BF)

    for p in range(pages):
        s_scr[:, p * page:(p + 1) * page] = _dot(qbd, kt_refs[p][0, 0].astype(BF))
    pad = jnp.zeros((page - n_new, width), F32)
    kn = jnp.concatenate([kn_ref[...], pad], axis=0).astype(BF)
    vn = jnp.concatenate([vn_ref[...], pad], axis=0).astype(BF)
    s_new = _nt_dot(qbd, kn)
    tok = lax.broadcasted_iota(jnp.int32, s_new.shape, 0) % n_new
    key = lax.broadcasted_iota(jnp.int32, s_new.shape, 1)
    s_scr[:, past:] = jnp.where(key <= tok, s_new, NEG_INF)

    s = s_scr[...]
    pr = jnp.exp(s - jnp.max(s, axis=-1, keepdims=True))
    inv_l = 1.0 / jnp.sum(pr, axis=-1, keepdims=True)
    s_scr[...] = pr

    acc = _dot(s_scr[:, past:].astype(BF), vn)
    for p in range(pages):
        v_wide = jnp.concatenate(
            [v_refs[p][0, 0, pl.ds(h, page, stride=heads), :] for h in range(heads)], axis=1)
        acc += _dot(s_scr[:, p * page:(p + 1) * page].astype(BF), v_wide.astype(BF))

    lam = _lambda(lam_ref, lam_init)
    acc = acc * inv_l
    for h in range(heads):
        r0 = h * 2 * n_new
        cols = slice(h * 2 * dh, (h + 1) * 2 * dh)
        o = acc[r0:r0 + n_new, cols] - lam * acc[r0 + n_new:r0 + 2 * n_new, cols]
        o_ref[:, cols] = _sub_norm(o, gs_ref, lam_init)


def paged_diff_attention(q, k_new, v_new, cache_kt, cache_v, layer, page_table, lam_p, g_sub,
                         heads, lam_init):
    t, width = q.shape
    b, n_pages = page_table.shape
    n_new = t // b
    dh = width // (2 * heads)
    page = cache_kt.shape[3]
    assert n_new == SUBLANES and 2 * dh == LANES and page == LANES
    nrow = 2 * heads * n_new
    tok = pl.BlockSpec((n_new, width), lambda i, pt: (i, 0))

    def page_spec(p, shape):
        return pl.BlockSpec((1, 1) + shape, lambda i, pt: (layer, pt[i, p], 0, 0))

    grid_spec = pltpu.PrefetchScalarGridSpec(
        num_scalar_prefetch=1,
        grid=(b,),
        in_specs=[tok, tok, tok,
                  pl.BlockSpec(lam_p.shape, lambda i, pt: (0, 0)),
                  pl.BlockSpec((1, 2 * dh), lambda i, pt: (0, 0))]
                 + [page_spec(p, (width, page)) for p in range(n_pages)]
                 + [page_spec(p, (page * heads, 2 * dh)) for p in range(n_pages)],
        out_specs=tok,
        scratch_shapes=[pltpu.VMEM((nrow, (n_pages + 1) * page), F32)],
    )
    return pl.pallas_call(
        functools.partial(_paged_kernel, heads=heads, dh=dh, n_new=n_new, pages=n_pages,
                          page=page, lam_init=lam_init),
        grid_spec=grid_spec,
        out_shape=jax.ShapeDtypeStruct((t, width), F32),
        compiler_params=_cparams(("parallel",)),
        name="paged_diff_attention",
    )(page_table, q, k_new, v_new, lam_p.astype(F32), g_sub.reshape(1, 2 * dh).astype(F32),
      *([cache_kt] * n_pages), *([cache_v] * n_pages))


def _trunk(x, rows_per_seq, mods, state, paged, weights, cfg):
    (g_norm_mix, g_norm_ffn, w_in_m, b_gate_m, g_head_m, w_out_m,
     w_in_d, g_q_d, g_k_d, lam_d, g_sub_d, w_out_d, w_gu_f, w_down_f,
     w_router, b_router, w_gu_e, w_down_e) = weights
    t, d = x.shape
    n_seq = t // rows_per_seq
    depth = g_norm_mix.shape[0]
    tm = cfg["tm"]
    heads_m = b_gate_m.shape[1] // 2
    dk, dv = state[0].shape[3], state[0].shape[4]
    qk_w = heads_m * dk
    heads_a, dh = cfg["heads_a"], cfg["dh"]
    new_state, new_kv = [], []
    for i in range(depth):
        j = i // 2
        mod = mods[i]
        if i % 2 == 0:
            col_scale = jnp.concatenate([jnp.ones((qk_w,), F32), jnp.full((qk_w,), dk ** -0.5, F32),
                                         jnp.ones((heads_m * dv,), F32)]).reshape(1, -1)
            small = rows_per_seq % M_CHUNK != 0
            (qkv,) = norm_proj(x, rows_per_seq, mod, 1, 0, g_norm_mix[i], w_in_m[j], 0,
                               2 * qk_w + heads_m * dv, [F32 if small else BF], tm, 512,
                               col_scale=col_scale, name="mlstm_qkv")
            (og,) = norm_proj(x, rows_per_seq, mod, 1, 0, g_norm_mix[i], w_in_m[j],
                              2 * qk_w + heads_m * dv, heads_m * dv, [F32], tm, 512, name="mlstm_o")
            w_gate = jnp.pad(w_in_m[j][:, 2 * qk_w + 2 * heads_m * dv:], ((0, 0), (0, LANES - 2 * heads_m)))
            (gates,) = norm_proj(x, rows_per_seq, mod, 1, 0, g_norm_mix[i], w_gate, 0, LANES,
                                 [F32], tm, LANES, name="mlstm_gates")
            c0, n0, m0 = state[0][j], state[1][j], state[2][j]
            y, c1, n1, m1 = mlstm_cell(qkv, og, gates, b_gate_m[j], g_head_m[j], c0, n0, m0,
                                       rows_per_seq, cfg["mlstm_seqs"] if small else 1)
            new_state.append((c1, n1, m1))
            w_out = w_out_m
        else:
            lam_init = 0.8 - 0.6 * math.exp(-0.3 * i)
            (q,) = norm_proj(x, rows_per_seq, mod, 1, 0, g_norm_mix[i], w_in_d[j], 0, d,
                             [BF if paged is None else F32], tm, 256, seg_gain=g_q_d[j], name="attn_q")
            k32, kbf = norm_proj(x, rows_per_seq, mod, 1, 0, g_norm_mix[i], w_in_d[j], d, d,
                                 [F32, BF], tm, 256, seg_gain=g_k_d[j], transposed=paged is None,
                                 name="attn_k")
            v32, vbf = norm_proj(x, rows_per_seq, mod, 1, 0, g_norm_mix[i], w_in_d[j], 2 * d, d,
                                 [F32, BF], tm, 512, name="attn_v")
            if paged is None:
                bound = (dh ** 0.5) * jnp.max(jnp.abs(g_q_d[j])) * jnp.max(jnp.abs(g_k_d[j]))
                y = flash_diff_attention(q, kbf, vbf, bound, lam_d[j], g_sub_d[j], heads_a,
                                         lam_init, cfg["tq"], cfg["tk"])
                k32 = jnp.transpose(k32.reshape(n_seq, heads_a, 2, dh, rows_per_seq), (0, 4, 1, 2, 3))
            else:
                cache_kt, cache_v, page_table = paged
                y = paged_diff_attention(q, k32, v32, cache_kt, cache_v, j, page_table, lam_d[j],
                                         g_sub_d[j], heads_a, lam_init)
            new_kv.append((k32, v32))
            w_out = w_out_d
        x = down_residual(y, w_out, j, x, rows_per_seq, mod, 2, tm, 512, name="mixer_out")
        if i % 2 == 0:
            act = swiglu_up(x, rows_per_seq, mod, 4, 3, g_norm_ffn[i], w_gu_f, j, tm, 256)
            x = down_residual(act, w_down_f, j, x, rows_per_seq, mod, 5, tm, 512, name="ffn_down")
        else:
            n_exp = w_router.shape[2]
            h, dg, rank, counts = router_gates(x, rows_per_seq, mod, 4, 3, g_norm_ffn[i],
                                               w_router[j], b_router[j], tm)
            counts = counts[:, 0, :n_exp].astype(jnp.int32).reshape(-1)
            x = moe_ffn(h, dg, rank, counts, w_gu_e, w_down_e, j, x, rows_per_seq, mod, 5, tm,
                        cfg["moe_ff_chunks"], min(cfg["moe_block_rows"], tm))
    return x, new_state, new_kv


def kernel(x_prompt, x_sample, c_prompt, c_sample, state_C, state_n, state_m, cache_k, cache_v, page_table, w_ada, b_ada, g_norm_mix, g_norm_ffn, w_in_m, b_gate_m, g_head_m, w_out_m, w_in_d, g_q_d, g_k_d, lam_d, g_sub_d, w_out_d, w_gu_f, w_down_f, w_router, b_router, w_gu_e, w_down_e):
    bp, seq, d = x_prompt.shape
    bs, dec_seq, _ = x_sample.shape
    depth = w_ada.shape[0]
    n_ml, _, heads_m, dk, dv = state_C.shape
    n_diff, n_pool, page, heads_a, _, dh = cache_k.shape
    weights = (g_norm_mix, g_norm_ffn, w_in_m, b_gate_m, g_head_m, w_out_m,
               w_in_d, g_q_d, g_k_d, lam_d, g_sub_d, w_out_d, w_gu_f, w_down_f,
               w_router, b_router, w_gu_e.astype(BF), w_down_e.astype(BF))

    n_c = bp + bs
    n_c_pad = -(-n_c // SUBLANES) * SUBLANES
    c_all = jnp.pad(jnp.concatenate([c_prompt, c_sample], axis=0), ((0, n_c_pad - n_c), (0, 0)))
    mod_all = ada_mod(c_all, w_ada, b_ada, tn=min(6 * d, 1536))
    mods_p = [mod_all[i, :bp].reshape(bp, 1, 6 * d) for i in range(depth)]
    mods_s = [mod_all[i, bp:n_c].reshape(bs, 1, 6 * d) for i in range(depth)]

    moe = dict(moe_ff_chunks=2, moe_block_rows=128)
    cfg_p = dict(tm=min(1024, seq), tq=min(512, seq), tk=min(1024, seq), heads_a=heads_a, dh=dh, **moe)
    cfg_s = dict(tm=min(1024, bs * dec_seq), heads_a=heads_a, dh=dh, mlstm_seqs=min(8, bs), **moe)

    zeros = (jnp.zeros((n_ml, bp, heads_m, dk, dv), F32), jnp.zeros((n_ml, bp, heads_m, dk), F32),
             jnp.zeros((n_ml, bp, heads_m), F32))
    y_p, st_p, kv_p = _trunk(x_prompt.reshape(bp * seq, d), seq, mods_p, zeros, None, weights, cfg_p)
    paged = (jnp.transpose(cache_k, (0, 1, 3, 4, 5, 2)).reshape(n_diff, n_pool, heads_a * 2 * dh, page),
             cache_v.reshape(n_diff, n_pool, page * heads_a, 2 * dh), page_table)
    y_s, st_s, kv_s = _trunk(x_sample.reshape(bs * dec_seq, d), dec_seq, mods_s,
                             (state_C, state_n, state_m), paged, weights, cfg_s)

    def stack(items, idx, shape):
        return jnp.stack([it[idx] for it in items]).reshape(shape)

    return (y_p.reshape(bp, seq, d), y_s.reshape(bs, dec_seq, d),
            stack(st_p, 0, (n_ml, bp, heads_m, dk, dv)), stack(st_p, 1, (n_ml, bp, heads_m, dk)),
            stack(st_p, 2, (n_ml, bp, heads_m)),
            stack(st_s, 0, (n_ml, bs, heads_m, dk, dv)), stack(st_s, 1, (n_ml, bs, heads_m, dk)),
            stack(st_s, 2, (n_ml, bs, heads_m)),
            stack(kv_p, 0, (n_diff, bp, seq, heads_a, 2, dh)),
            stack(kv_p, 1, (n_diff, bp, seq, heads_a, 2 * dh)),
            stack(kv_s, 0, (n_diff, bs, dec_seq, heads_a, 2, dh)),
            stack(kv_s, 1, (n_diff, bs, dec_seq, heads_a, 2 * dh)))
```

```python
import functools
import math

import jax
import jax.numpy as jnp
from jax import lax
from jax.experimental import pallas as pl
from jax.experimental.pallas import tpu as pltpu

BF = jnp.bfloat16
F32 = jnp.float32
EPS = 1e-6
M_CHUNK = 128
LANES = 128
SUBLANES = 8
VMEM_LIMIT = 56 * 1024 * 1024
NEG_INF = float("-inf")


def _cparams(sem):
    return pltpu.CompilerParams(dimension_semantics=sem, vmem_limit_bytes=VMEM_LIMIT)


def _tile(n, pref):
    if n <= pref:
        return n
    t = pref - pref % LANES
    while n % t:
        t -= LANES
    return t


def _nt_dot(a, b, **kw):
    return lax.dot_general(a, b, (((1,), (1,)), ((), ())), preferred_element_type=F32, **kw)


def _tn_dot(a, b, **kw):
    return lax.dot_general(a, b, (((0,), (0,)), ((), ())), preferred_element_type=F32, **kw)


def _dot(a, b, **kw):
    return jnp.dot(a, b, preferred_element_type=F32, **kw)


def _ada_kernel(c_ref, w_ref, b_ref, o_ref):
    c = c_ref[...]
    a = (c * jax.nn.sigmoid(c)).astype(BF)
    o_ref[0] = _dot(a, w_ref[0].astype(BF)) + b_ref[0]


def ada_mod(c_all, w_ada, b_ada, tn):
    depth, d, n = w_ada.shape
    bp = c_all.shape[0]
    return pl.pallas_call(
        _ada_kernel,
        grid=(depth, n // tn),
        in_specs=[pl.BlockSpec((bp, d), lambda i, j: (0, 0)),
                  pl.BlockSpec((1, d, tn), lambda i, j: (i, 0, j)),
                  pl.BlockSpec((1, 1, tn), lambda i, j: (i, 0, j))],
        out_specs=pl.BlockSpec((1, bp, tn), lambda i, j: (i, 0, j)),
        out_shape=jax.ShapeDtypeStruct((depth, bp, n), F32),
        compiler_params=_cparams(("arbitrary", "arbitrary")),
        name="ada_mod",
    )(c_all, w_ada, b_ada.reshape(depth, 1, n))


def _mod_spec(rows_per_seq, tm, width, col_block):
    if rows_per_seq >= tm:
        npb = rows_per_seq // tm
        return pl.BlockSpec((1, 1, width), lambda *g: (g[0] // npb, 0, col_block(*g)))
    gb = tm // rows_per_seq
    return pl.BlockSpec((gb, 1, width), lambda *g: (g[0], 0, col_block(*g)))


def _rows(m_ref, tm):
    m = m_ref[...]
    gb, _, w = m.shape
    if gb == 1:
        return m[0]
    return jnp.broadcast_to(m, (gb, tm // gb, w)).reshape(tm, w)


def _modulated(x_ref, g_ref, sc_ref, sh_ref):
    x = x_ref[...]
    tm = x.shape[0]
    y = x * lax.rsqrt(jnp.mean(x * x, axis=-1, keepdims=True) + EPS) * g_ref[...]
    return y * (1.0 + _rows(sc_ref, tm)) + _rows(sh_ref, tm)


def _proj_kernel(x_ref, g_ref, sc_ref, sh_ref, w_ref, cs_ref, *rest):
    outs, h_scr = rest[:-1], rest[-1]

    @pl.when(pl.program_id(1) == 0)
    def _():
        h_scr[...] = _modulated(x_ref, g_ref, sc_ref, sh_ref).astype(BF)

    acc = _dot(h_scr[...], w_ref[...].astype(BF)) * cs_ref[...]
    for o in outs:
        o[...] = acc.astype(o.dtype)


def _segnorm_kernel(x_ref, g_ref, sc_ref, sh_ref, w_ref, gv_ref, seg_ref, *rest, seg, transposed):
    outs, h_scr = rest[:-1], rest[-1]

    @pl.when(pl.program_id(1) == 0)
    def _():
        h_scr[...] = _modulated(x_ref, g_ref, sc_ref, sh_ref).astype(BF)

    acc = _dot(h_scr[...], w_ref[...].astype(BF))
    sq = acc * acc
    hi = sq.astype(BF)
    lo = (sq - hi.astype(F32)).astype(BF)
    ssq = _dot(hi, seg_ref[...]) + _dot(lo, seg_ref[...])
    y = acc * lax.rsqrt(ssq * (1.0 / seg) + EPS) * gv_ref[...]
    if transposed:
        yt = y.T
        for o in outs:
            o[0] = yt.astype(o.dtype)
    else:
        for o in outs:
            o[...] = y.astype(o.dtype)


def norm_proj(x, rows_per_seq, mod, sc_chunk, sh_chunk, g, w, col_off, n, out_dtypes, tm, tn,
              col_scale=None, seg_gain=None, transposed=False, name="norm_proj"):
    t, d = x.shape
    tn = _tile(math.gcd(n, col_off) if col_off else n, tn)
    x_spec = pl.BlockSpec((tm, d), lambda i, j: (i, 0))
    g_spec = pl.BlockSpec((1, d), lambda i, j: (0, 0))
    sc_spec = _mod_spec(rows_per_seq, tm, d, lambda i, j: sc_chunk)
    sh_spec = _mod_spec(rows_per_seq, tm, d, lambda i, j: sh_chunk)
    off = col_off // tn
    w_spec = pl.BlockSpec((d, tn), lambda i, j: (0, off + j))
    out_specs = [pl.BlockSpec((tm, tn), lambda i, j: (i, j)) for _ in out_dtypes]
    out_shape = [jax.ShapeDtypeStruct((t, n), dt) for dt in out_dtypes]
    if seg_gain is None:
        if col_scale is None:
            col_scale = jnp.ones((1, n), F32)
        kern = _proj_kernel
        extra = [col_scale]
        extra_specs = [pl.BlockSpec((1, tn), lambda i, j: (0, j))]
    else:
        seg = seg_gain.shape[0]
        kern = functools.partial(_segnorm_kernel, seg=seg, transposed=transposed)
        if transposed:
            npb = rows_per_seq // tm
            out_specs = [pl.BlockSpec((1, tn, tm), lambda i, j: (i // npb, j, i % npb))
                         for _ in out_dtypes]
            out_shape = [jax.ShapeDtypeStruct((t // rows_per_seq, n, rows_per_seq), dt)
                         for dt in out_dtypes]
        gv = jnp.tile(seg_gain.astype(F32), tn // seg).reshape(1, tn)
        ids = jnp.arange(tn) // seg
        seg_mat = (ids[:, None] == ids[None, :]).astype(BF)
        extra = [gv, seg_mat]
        extra_specs = [pl.BlockSpec((1, tn), lambda i, j: (0, 0)),
                       pl.BlockSpec((tn, tn), lambda i, j: (0, 0))]
    res = pl.pallas_call(
        kern,
        grid=(t // tm, n // tn),
        in_specs=[x_spec, g_spec, sc_spec, sh_spec, w_spec] + extra_specs,
        out_specs=out_specs,
        out_shape=out_shape,
        scratch_shapes=[pltpu.VMEM((tm, d), BF)],
        compiler_params=_cparams(("parallel", "arbitrary")),
        name=name,
    )(x, g.reshape(1, d), mod, mod, w, *extra)
    return res


def _router_kernel(x_ref, g_ref, sc_ref, sh_ref, w_ref, b_ref, tri_ref,
                   h_ref, o_ref, pos_ref, cnt_ref, *, n_experts):
    h = _modulated(x_ref, g_ref, sc_ref, sh_ref)
    h_ref[...] = h.astype(h_ref.dtype)
    logits = _dot(h, w_ref[...], precision=lax.Precision.HIGHEST) + b_ref[...]
    lane = lax.broadcasted_iota(jnp.int32, logits.shape, 1).astype(F32)
    big = float(LANES)
    l1 = jnp.where(lane < n_experts, logits, NEG_INF)
    m1 = jnp.max(l1, axis=-1, keepdims=True)
    i1 = jnp.min(jnp.where(l1 == m1, lane, big), axis=-1, keepdims=True)
    l2 = jnp.where(lane == i1, NEG_INF, l1)
    m2 = jnp.max(l2, axis=-1, keepdims=True)
    i2 = jnp.min(jnp.where(l2 == m2, lane, big), axis=-1, keepdims=True)
    e2 = jnp.exp(m2 - m1)
    w1 = 1.0 / (1.0 + e2)
    w2 = e2 / (1.0 + e2)
    gate = jnp.where(lane == i1, w1, 0.0) + jnp.where(lane == i2, w2, 0.0)
    o_ref[...] = gate
    routed = jnp.where(gate > 0.0, 1.0, 0.0)
    pos_ref[...] = _dot(tri_ref[...], routed.astype(BF))
    cnt_ref[0] = jnp.sum(routed, axis=0, keepdims=True)


def router_gates(x, rows_per_seq, mod, sc_chunk, sh_chunk, g, w_router, b_router, tm):
    t, d = x.shape
    e = w_router.shape[1]
    w_pad = jnp.pad(w_router, ((0, 0), (0, LANES - e)))
    b_pad = jnp.pad(b_router, (0, LANES - e)).reshape(1, LANES)
    idx = jnp.arange(tm)
    strict_lower = (idx[:, None] > idx[None, :]).astype(BF)
    tok = pl.BlockSpec((tm, LANES), lambda i: (i, 0))
    return pl.pallas_call(
        functools.partial(_router_kernel, n_experts=e),
        grid=(t // tm,),
        in_specs=[pl.BlockSpec((tm, d), lambda i: (i, 0)),
                  pl.BlockSpec((1, d), lambda i: (0, 0)),
                  _mod_spec(rows_per_seq, tm, d, lambda i: sc_chunk),
                  _mod_spec(rows_per_seq, tm, d, lambda i: sh_chunk),
                  pl.BlockSpec((d, LANES), lambda i: (0, 0)),
                  pl.BlockSpec((1, LANES), lambda i: (0, 0)),
                  pl.BlockSpec((tm, tm), lambda i: (0, 0))],
        out_specs=[pl.BlockSpec((tm, d), lambda i: (i, 0)), tok, tok,
                   pl.BlockSpec((1, 1, LANES), lambda i: (i, 0, 0))],
        out_shape=[jax.ShapeDtypeStruct((t, d), BF),
                   jax.ShapeDtypeStruct((t, LANES), F32),
                   jax.ShapeDtypeStruct((t, LANES), F32),
                   jax.ShapeDtypeStruct((t // tm, 1, LANES), F32)],
        compiler_params=_cparams(("parallel",)),
        name="router",
    )(x, g.reshape(1, d), mod, mod, w_pad, b_pad, strict_lower)


def _moe_kernel(cnt_ref, h_ref, dg_ref, pos_ref, wg_ref, wu_ref, wd_ref, x_ref, gt_ref, o_ref,
                xs_scr, y_scr, *, n_experts, br):
    ti, e, fc = pl.program_id(0), pl.program_id(1), pl.program_id(2)
    last_fc = pl.num_programs(2) - 1
    tm = h_ref.shape[0]

    @pl.when((e == 0) & (fc == 0))
    def _():
        o_ref[...] = jnp.zeros_like(o_ref)

    sel = lax.broadcasted_iota(jnp.int32, (tm, LANES), 1) == e
    ge = jnp.sum(jnp.where(sel, dg_ref[...], 0.0), axis=-1, keepdims=True)
    pos = jnp.sum(jnp.where(sel, pos_ref[...], 0.0), axis=-1, keepdims=True)
    routed = ge > 0.0
    n_blocks = (cnt_ref[ti * n_experts + e] + br - 1) // br

    def block(rb, carry):
        r0 = pl.multiple_of(rb * br, br)
        rows = pl.ds(r0, br)
        slot = (r0 + lax.broadcasted_iota(jnp.int32, (tm, br), 1)).astype(F32)
        onehot = jnp.where((pos == slot) & routed, 1.0, 0.0).astype(BF)

        @pl.when(fc == 0)
        def _():
            xs_scr[rows, :] = _tn_dot(onehot, h_ref[...]).astype(BF)

        xs = xs_scr[rows, :]
        a = _dot(xs, wg_ref[0, 0])
        u = _dot(xs, wu_ref[0, 0])
        y = _dot((a * jax.nn.sigmoid(a) * u).astype(BF), wd_ref[0, 0])

        @pl.when(fc == 0)
        def _():
            y_scr[rows, :] = y

        @pl.when(fc > 0)
        def _():
            y_scr[rows, :] += y

        @pl.when(fc == last_fc)
        def _():
            o_ref[...] += ge * _dot(onehot, y_scr[rows, :].astype(BF))
        return carry

    lax.fori_loop(0, n_blocks, block, 0)

    @pl.when((e == n_experts - 1) & (fc == last_fc))
    def _():
        o_ref[...] = x_ref[...] + _rows(gt_ref, tm) * o_ref[...]


def moe_ffn(h, dense_gate, rank, counts, w_gu, w_down, layer, x, rows_per_seq, mod, gt_chunk, tm,
            ff_chunks, block_rows):
    t, d = x.shape
    _, e, _, f2 = w_gu.shape
    f = f2 // 2
    fcw = f // ff_chunks
    tok = pl.BlockSpec((tm, LANES), lambda i, k, c, cnt: (i, 0))
    grid_spec = pltpu.PrefetchScalarGridSpec(
        num_scalar_prefetch=1,
        grid=(t // tm, e, ff_chunks),
        in_specs=[pl.BlockSpec((tm, d), lambda i, k, c, cnt: (i, 0)), tok, tok,
                  pl.BlockSpec((1, 1, d, fcw), lambda i, k, c, cnt: (layer, k, 0, c)),
                  pl.BlockSpec((1, 1, d, fcw), lambda i, k, c, cnt: (layer, k, 0, ff_chunks + c)),
                  pl.BlockSpec((1, 1, fcw, d), lambda i, k, c, cnt: (layer, k, c, 0)),
                  pl.BlockSpec((tm, d), lambda i, k, c, cnt: (i, 0)),
                  _mod_spec(rows_per_seq, tm, d, lambda i, k, c, cnt: gt_chunk)],
        out_specs=pl.BlockSpec((tm, d), lambda i, k, c, cnt: (i, 0)),
        scratch_shapes=[pltpu.VMEM((tm, d), BF), pltpu.VMEM((tm, d), F32)],
    )
    return pl.pallas_call(
        functools.partial(_moe_kernel, n_experts=e, br=block_rows),
        grid_spec=grid_spec,
        out_shape=jax.ShapeDtypeStruct((t, d), F32),
        compiler_params=_cparams(("parallel", "arbitrary", "arbitrary")),
        name="moe_ffn",
    )(counts, h, dense_gate, rank, w_gu, w_gu, w_down, x, mod)


def _swiglu_up_kernel(x_ref, g_ref, sc_ref, sh_ref, wg_ref, wu_ref, o_ref, h_scr):
    @pl.when(pl.program_id(1) == 0)
    def _():
        h_scr[...] = _modulated(x_ref, g_ref, sc_ref, sh_ref).astype(BF)

    h = h_scr[...]
    a = _dot(h, wg_ref[0].astype(BF))
    u = _dot(h, wu_ref[0].astype(BF))
    o_ref[...] = (a * jax.nn.sigmoid(a) * u).astype(o_ref.dtype)


def swiglu_up(x, rows_per_seq, mod, sc_chunk, sh_chunk, g, w_gu, layer, tm, tn):
    t, d = x.shape
    f = w_gu.shape[2] // 2
    tn = _tile(f, tn)
    nf = f // tn
    return pl.pallas_call(
        _swiglu_up_kernel,
        grid=(t // tm, nf),
        in_specs=[pl.BlockSpec((tm, d), lambda i, j: (i, 0)),
                  pl.BlockSpec((1, d), lambda i, j: (0, 0)),
                  _mod_spec(rows_per_seq, tm, d, lambda i, j: sc_chunk),
                  _mod_spec(rows_per_seq, tm, d, lambda i, j: sh_chunk),
                  pl.BlockSpec((1, d, tn), lambda i, j: (layer, 0, j)),
                  pl.BlockSpec((1, d, tn), lambda i, j: (layer, 0, nf + j))],
        out_specs=pl.BlockSpec((tm, tn), lambda i, j: (i, j)),
        out_shape=jax.ShapeDtypeStruct((t, f), BF),
        scratch_shapes=[pltpu.VMEM((tm, d), BF)],
        compiler_params=_cparams(("parallel", "arbitrary")),
        name="swiglu_up",
    )(x, g.reshape(1, d), mod, mod, w_gu, w_gu)


def _down_kernel(y_ref, w_ref, x_ref, gt_ref, o_ref):
    p = _dot(y_ref[...].astype(BF), w_ref[0].astype(BF))
    o_ref[...] = x_ref[...] + _rows(gt_ref, x_ref.shape[0]) * p


def down_residual(y, w, layer, x, rows_per_seq, mod, gt_chunk, tm, tn, name="down_residual"):
    t, k = y.shape
    d = x.shape[1]
    tn = _tile(d, tn)
    nd = d // tn
    return pl.pallas_call(
        _down_kernel,
        grid=(t // tm, nd),
        in_specs=[pl.BlockSpec((tm, k), lambda i, j: (i, 0)),
                  pl.BlockSpec((1, k, tn), lambda i, j: (layer, 0, j)),
                  pl.BlockSpec((tm, tn), lambda i, j: (i, j)),
                  _mod_spec(rows_per_seq, tm, tn, lambda i, j: gt_chunk * nd + j)],
        out_specs=pl.BlockSpec((tm, tn), lambda i, j: (i, j)),
        out_shape=jax.ShapeDtypeStruct((t, d), F32),
        compiler_params=_cparams(("parallel", "arbitrary")),
        name=name,
    )(y, w, x, mod)


def _log_sigmoid(x):
    return jnp.minimum(x, 0.0) - jnp.log1p(jnp.exp(-jnp.abs(x)))


def _mlstm_kernel(qkv_ref, o_ref, gt_ref, bg_ref, gh_ref, c0_ref, n0_ref, m0_ref,
                  y_ref, c_ref, n_ref, m_ref, *, heads, dk, dv, chunk, seqs):
    L = chunk
    hi = lax.Precision.HIGHEST
    cd = BF if L >= 16 else F32

    @pl.when(pl.program_id(1) == 0)
    def _():
        c_ref[...] = c0_ref[...]
        n_ref[...] = n0_ref[...]
        m_ref[...] = m0_ref[...]

    row = lax.broadcasted_iota(jnp.int32, (L, L), 0)
    col = lax.broadcasted_iota(jnp.int32, (L, L), 1)
    causal = row >= col
    tril = causal.astype(F32)
    sel = (lax.broadcasted_iota(jnp.int32, (SUBLANES, LANES), 0)
           == lax.broadcasted_iota(jnp.int32, (SUBLANES, LANES), 1)).astype(F32)

    def one_seq(bi, rows):
        gates = gt_ref[rows, :] + bg_ref[...]
        lf = _log_sigmoid(gates)
        bcum = _dot(tril, lf, precision=hi)
        g_rows = _nt_dot(sel, gates, precision=hi)
        b_rows = _nt_dot(sel, bcum, precision=hi)
        for h in range(heads):
            q = qkv_ref[rows, h * dk:(h + 1) * dk]
            k = qkv_ref[rows, (heads + h) * dk:(heads + h + 1) * dk]
            v = qkv_ref[rows, 2 * heads * dk + h * dv:2 * heads * dk + (h + 1) * dv]
            qf, kf = q.astype(F32), k.astype(F32)
            qc, kc, vc = q.astype(cd), k.astype(cd), v.astype(cd)
            ig_col = gates[:, h:h + 1]
            ig_row = g_rows[h:h + 1, :]
            b_col = bcum[:, heads + h:heads + h + 1]
            b_row = b_rows[heads + h:heads + h + 1, :]
            b_end = bcum[L - 1:L, heads + h:heads + h + 1]
            c_old = c_ref[bi, h]
            n_old = n_ref[bi, h]
            m_old = m_ref[bi, h][:, :1]

            dlog = jnp.where(causal, b_col - b_row + ig_row, NEG_INF)
            gcar = b_col + m_old
            m_t = jnp.maximum(gcar, jnp.max(dlog, axis=-1, keepdims=True))
            s = _nt_dot(qc, kc) * jnp.exp(dlog - m_t)
            dec = jnp.exp(gcar - m_t)
            num = _dot(s.astype(cd), vc) + dec * _dot(qc, c_old.astype(cd))
            den = (jnp.sum(s, axis=-1, keepdims=True)
                   + dec * jnp.sum(qf * n_old, axis=-1, keepdims=True))
            hh = num / jnp.maximum(jnp.abs(den), jnp.exp(-m_t))

            a_row = b_end - b_row + ig_row
            a_col = b_end - b_col + ig_col
            m_new = jnp.maximum(b_end + m_old, jnp.max(a_row, axis=-1, keepdims=True))
            wk = jnp.exp(a_col - m_new)
            decay = jnp.exp(b_end + m_old - m_new)
            kw = kf * wk
            c_ref[bi, h] = decay * c_old + _tn_dot(kw.astype(cd), vc)
            n_ref[bi, h] = decay * n_old + jnp.sum(kw, axis=0, keepdims=True)
            m_ref[bi, h] = jnp.broadcast_to(m_new, (1, LANES))

            hs = (hh * lax.rsqrt(jnp.mean(hh * hh, axis=-1, keepdims=True) + EPS)
                  * gh_ref[:, h * dv:(h + 1) * dv])
            og = o_ref[rows, h * dv:(h + 1) * dv]
            y_ref[rows, h * dv:(h + 1) * dv] = (hs * jax.nn.sigmoid(og)).astype(y_ref.dtype)

    if seqs == 1:
        one_seq(0, slice(None))
    else:
        def body(bi, carry):
            one_seq(bi, pl.ds(pl.multiple_of(bi * L, L), L))
            return carry
        lax.fori_loop(0, seqs, body, 0)


def mlstm_cell(qkv, og, gates, b_gate, g_head, c0, n0, m0, seq_len, seqs_per_step):
    t = qkv.shape[0]
    b, heads, dk, dv = c0.shape
    chunk = M_CHUNK if seq_len % M_CHUNK == 0 else seq_len
    nc = seq_len // chunk
    bt = seqs_per_step
    assert bt == 1 or nc == 1
    rows = bt * chunk
    n4 = n0.reshape(b, heads, 1, dk)
    m4 = jnp.broadcast_to(m0.reshape(b, heads, 1, 1), (b, heads, 1, LANES))
    bg = jnp.pad(b_gate.astype(F32), (0, LANES - 2 * heads)).reshape(1, LANES)
    y_dtype = BF if chunk >= 16 else F32
    tok = lambda i, c: (i * nc + c, 0)
    st = lambda i, c: (i, 0, 0, 0)
    y, c_new, n_new, m_new = pl.pallas_call(
        functools.partial(_mlstm_kernel, heads=heads, dk=dk, dv=dv, chunk=chunk, seqs=bt),
        grid=(b // bt, nc),
        in_specs=[pl.BlockSpec((rows, qkv.shape[1]), tok),
                  pl.BlockSpec((rows, heads * dv), tok),
                  pl.BlockSpec((rows, LANES), tok),
                  pl.BlockSpec((1, LANES), lambda i, c: (0, 0)),
                  pl.BlockSpec((1, heads * dv), lambda i, c: (0, 0)),
                  pl.BlockSpec((bt, heads, dk, dv), st),
                  pl.BlockSpec((bt, heads, 1, dk), st),
                  pl.BlockSpec((bt, heads, 1, LANES), st)],
        out_specs=[pl.BlockSpec((rows, heads * dv), tok),
                   pl.BlockSpec((bt, heads, dk, dv), st),
                   pl.BlockSpec((bt, heads, 1, dk), st),
                   pl.BlockSpec((bt, heads, 1, LANES), st)],
        out_shape=[jax.ShapeDtypeStruct((t, heads * dv), y_dtype),
                   jax.ShapeDtypeStruct((b, heads, dk, dv), F32),
                   jax.ShapeDtypeStruct((b, heads, 1, dk), F32),
                   jax.ShapeDtypeStruct((b, heads, 1, LANES), F32)],
        compiler_params=_cparams(("parallel", "arbitrary")),
        name="mlstm_cell",
    )(qkv, og, gates, bg, g_head.reshape(1, heads * dv).astype(F32), c0, n4, m4)
    return y, c_new, n_new.reshape(b, heads, dk), m_new[:, :, 0, 0]


def _lambda(lam_ref, lam_init):
    lp = lam_ref[...]
    a = jnp.sum(lp[0:1] * lp[1:2], axis=-1, keepdims=True)
    b = jnp.sum(lp[2:3] * lp[3:4], axis=-1, keepdims=True)
    return jnp.exp(a) - jnp.exp(b) + lam_init


def _sub_norm(o, gsub_ref, lam_init):
    y = o * lax.rsqrt(jnp.mean(o * o, axis=-1, keepdims=True) + EPS) * gsub_ref[...]
    return y * (1.0 - lam_init)


def _flash_kernel(qi_ref, ki_ref, q_ref, kt_ref, v_ref, bound_ref, lam_ref, gs_ref, o_ref,
                  qm_scr, m_scr, acc_scr, *, dh, tq, tk, lam_init, online):
    pair = pl.program_id(2)
    qi = qi_ref[pair]
    ki = ki_ref[pair]
    dv = LANES

    @pl.when(ki == 0)
    def _():
        q = q_ref[...] * (dh ** -0.5)
        lane = lax.broadcasted_iota(jnp.int32, q.shape, 1)
        qm_scr[0] = jnp.where(lane < dh, q, jnp.zeros_like(q))
        qm_scr[1] = jnp.where(lane >= dh, q, jnp.zeros_like(q))
        m_scr[...] = jnp.full_like(m_scr, NEG_INF)
        acc_scr[...] = jnp.zeros_like(acc_scr)

    def absorb(masked):
        kt = kt_ref[0]
        ones_col = (lax.broadcasted_iota(jnp.int32, (tk, LANES), 1) == 0).astype(BF)
        vx = jnp.concatenate([v_ref[...], ones_col], axis=1)
        if masked:
            row = qi * tq + lax.broadcasted_iota(jnp.int32, (tq, tk), 0)
            col = ki * tk + lax.broadcasted_iota(jnp.int32, (tq, tk), 1)
            keep = row >= col
        for c in range(2):
            s = _dot(qm_scr[c], kt)
            if masked:
                s = jnp.where(keep, s, NEG_INF)
            if online:
                m_prev = m_scr[c]
                m_new = jnp.maximum(m_prev, jnp.max(s, axis=-1, keepdims=True))
                p = jnp.exp(s - m_new)
                acc_scr[c] = jnp.exp(m_prev - m_new) * acc_scr[c] + _dot(p.astype(BF), vx)
                m_scr[c] = m_new
            else:
                p = jnp.exp(s - bound_ref[...])
                acc_scr[c] += _dot(p.astype(BF), vx)

    needs_mask = (ki + 1) * tk - 1 > qi * tq

    @pl.when(needs_mask)
    def _():
        absorb(True)

    @pl.when(jnp.logical_not(needs_mask))
    def _():
        absorb(False)

    @pl.when(ki == (qi * tq + tq - 1) // tk)
    def _():
        lam = _lambda(lam_ref, lam_init)
        a0 = acc_scr[0]
        a1 = acc_scr[1]
        o = a0[:, :dv] / a0[:, dv:dv + 1] - lam * (a1[:, :dv] / a1[:, dv:dv + 1])
        o_ref[...] = _sub_norm(o, gs_ref, lam_init).astype(o_ref.dtype)


SAFE_SCORE_BOUND = 30.0


def flash_diff_attention(q, kt, v, score_bound, lam_p, g_sub, heads, lam_init, tq, tk):
    t = q.shape[0]
    n_seq, _, seq = kt.shape
    dv = v.shape[1] // heads
    dh = q.shape[1] // (2 * heads)
    assert 2 * dh == LANES and dv == LANES
    nq, nk = seq // tq, seq // tk
    pairs = [(i, j) for i in range(nq) for j in range((i * tq + tq - 1) // tk + 1)]
    qi_tab = jnp.asarray([p[0] for p in pairs], jnp.int32)
    ki_tab = jnp.asarray([p[1] for p in pairs], jnp.int32)

    def call(online):
        grid_spec = pltpu.PrefetchScalarGridSpec(
            num_scalar_prefetch=2,
            grid=(n_seq, heads, len(pairs)),
            in_specs=[pl.BlockSpec((tq, LANES), lambda b, h, p, qt, kt_: (b * nq + qt[p], h)),
                      pl.BlockSpec((1, LANES, tk), lambda b, h, p, qt, kt_: (b, h, kt_[p])),
                      pl.BlockSpec((tk, LANES), lambda b, h, p, qt, kt_: (b * nk + kt_[p], h)),
                      pl.BlockSpec((1, 1), lambda b, h, p, qt, kt_: (0, 0)),
                      pl.BlockSpec(lam_p.shape, lambda b, h, p, qt, kt_: (0, 0)),
                      pl.BlockSpec((1, dv), lambda b, h, p, qt, kt_: (0, 0))],
            out_specs=pl.BlockSpec((tq, LANES), lambda b, h, p, qt, kt_: (b * nq + qt[p], h)),
            scratch_shapes=[pltpu.VMEM((2, tq, LANES), BF),
                            pltpu.VMEM((2, tq, 1), F32),
                            pltpu.VMEM((2, tq, dv + LANES), F32)],
        )
        return pl.pallas_call(
            functools.partial(_flash_kernel, dh=dh, tq=tq, tk=tk, lam_init=lam_init, online=online),
            grid_spec=grid_spec,
            out_shape=jax.ShapeDtypeStruct((t, heads * dv), BF),
            compiler_params=_cparams(("parallel", "parallel", "arbitrary")),
            name="flash_online" if online else "flash_bounded",
        )(qi_tab, ki_tab, q, kt, v, score_bound.reshape(1, 1).astype(F32), lam_p.astype(F32),
          g_sub.reshape(1, dv).astype(F32))

    return lax.cond(score_bound < SAFE_SCORE_BOUND, lambda: call(False), lambda: call(True))


def _paged_kernel(pt_ref, q_ref, kn_ref, vn_ref, lam_ref, gs_ref, *rest,
                  heads, dh, n_new, pages, page, lam_init):
    del pt_ref
    kt_refs = rest[:pages]
    v_refs = rest[pages:2 * pages]
    o_ref, s_scr = rest[2 * pages:]
    width = heads * 2 * dh
    nrow = 2 * heads * n_new
    past = pages * page

    q = q_ref[...] * (dh ** -0.5)
    qt = jnp.concatenate([q] * (2 * heads), axis=0)
    r = lax.broadcasted_iota(jnp.int32, (nrow, width), 0)
    cblk = lax.broadcasted_iota(jnp.int32, (nrow, width), 1) // dh
    qbd = jnp.where(cblk == r // n_new, qt, 0.0).astype(BF)

    for p in range(pages):
        s_scr[:, p * page:(p + 1) * page] = _dot(qbd, kt_refs[p][0, 0].astype(BF))
    pad = jnp.zeros((page - n_new, width), F32)
    kn = jnp.concatenate([kn_ref[...], pad], axis=0).astype(BF)
    vn = jnp.concatenate([vn_ref[...], pad], axis=0).astype(BF)
    s_new = _nt_dot(qbd, kn)
    tok = lax.broadcasted_iota(jnp.int32, s_new.shape, 0) % n_new
    key = lax.broadcasted_iota(jnp.int32, s_new.shape, 1)
    s_scr[:, past:] = jnp.where(key <= tok, s_new, NEG_INF)

    s = s_scr[...]
    pr = jnp.exp(s - jnp.max(s, axis=-1, keepdims=True))
    inv_l = 1.0 / jnp.sum(pr, axis=-1, keepdims=True)
    s_scr[...] = pr

    acc = _dot(s_scr[:, past:].astype(BF), vn)
    for p in range(pages):
        v_wide = jnp.concatenate(
            [v_refs[p][0, 0, pl.ds(h, page, stride=heads), :] for h in range(heads)], axis=1)
        acc += _dot(s_scr[:, p * page:(p + 1) * page].astype(BF), v_wide.astype(BF))

    lam = _lambda(lam_ref, lam_init)
    acc = acc * inv_l
    for h in range(heads):
        r0 = h * 2 * n_new
        cols = slice(h * 2 * dh, (h + 1) * 2 * dh)
        o = acc[r0:r0 + n_new, cols] - lam * acc[r0 + n_new:r0 + 2 * n_new, cols]
        o_ref[:, cols] = _sub_norm(o, gs_ref, lam_init)


def paged_diff_attention(q, k_new, v_new, cache_kt, cache_v, layer, page_table, lam_p, g_sub,
                         heads, lam_init):
    t, width = q.shape
    b, n_pages = page_table.shape
    n_new = t // b
    dh = width // (2 * heads)
    page = cache_kt.shape[3]
    assert n_new == SUBLANES and 2 * dh == LANES and page == LANES
    nrow = 2 * heads * n_new
    tok = pl.BlockSpec((n_new, width), lambda i, pt: (i, 0))

    def page_spec(p, shape):
        return pl.BlockSpec((1, 1) + shape, lambda i, pt: (layer, pt[i, p], 0, 0))

    grid_spec = pltpu.PrefetchScalarGridSpec(
        num_scalar_prefetch=1,
        grid=(b,),
        in_specs=[tok, tok, tok,
                  pl.BlockSpec(lam_p.shape, lambda i, pt: (0, 0)),
                  pl.BlockSpec((1, 2 * dh), lambda i, pt: (0, 0))]
                 + [page_spec(p, (width, page)) for p in range(n_pages)]
                 + [page_spec(p, (page * heads, 2 * dh)) for p in range(n_pages)],
        out_specs=tok,
        scratch_shapes=[pltpu.VMEM((nrow, (n_pages + 1) * page), F32)],
    )
    return pl.pallas_call(
        functools.partial(_paged_kernel, heads=heads, dh=dh, n_new=n_new, pages=n_pages,
                          page=page, lam_init=lam_init),
        grid_spec=grid_spec,
        out_shape=jax.ShapeDtypeStruct((t, width), F32),
        compiler_params=_cparams(("parallel",)),
        name="paged_diff_attention",
    )(page_table, q, k_new, v_new, lam_p.astype(F32), g_sub.reshape(1, 2 * dh).astype(F32),
      *([cache_kt] * n_pages), *([cache_v] * n_pages))


def _trunk(x, rows_per_seq, mods, state, paged, weights, cfg):
    (g_norm_mix, g_norm_ffn, w_in_m, b_gate_m, g_head_m, w_out_m,
     w_in_d, g_q_d, g_k_d, lam_d, g_sub_d, w_out_d, w_gu_f, w_down_f,
     w_router, b_router, w_gu_e, w_down_e) = weights
    t, d = x.shape
    n_seq = t // rows_per_seq
    depth = g_norm_mix.shape[0]
    tm = cfg["tm"]
    tn_proj, tn_seg, tn_up = 1024, 256, 1408
    heads_m = b_gate_m.shape[1] // 2
    dk, dv = state[0].shape[3], state[0].shape[4]
    qk_w = heads_m * dk
    heads_a, dh = cfg["heads_a"], cfg["dh"]
    new_state, new_kv = [], []
    for i in range(depth):
        j = i // 2
        mod = mods[i]
        if i % 2 == 0:
            col_scale = jnp.concatenate([jnp.ones((qk_w,), F32), jnp.full((qk_w,), dk ** -0.5, F32),
                                         jnp.ones((heads_m * dv,), F32)]).reshape(1, -1)
            small = rows_per_seq % M_CHUNK != 0
            (qkv,) = norm_proj(x, rows_per_seq, mod, 1, 0, g_norm_mix[i], w_in_m[j], 0,
                               2 * qk_w + heads_m * dv, [F32 if small else BF], tm, tn_proj,
                               col_scale=col_scale, name="mlstm_qkv")
            (og,) = norm_proj(x, rows_per_seq, mod, 1, 0, g_norm_mix[i], w_in_m[j],
                              2 * qk_w + heads_m * dv, heads_m * dv, [F32], tm, tn_proj, name="mlstm_o")
            w_gate = jnp.pad(w_in_m[j][:, 2 * qk_w + 2 * heads_m * dv:], ((0, 0), (0, LANES - 2 * heads_m)))
            (gates,) = norm_proj(x, rows_per_seq, mod, 1, 0, g_norm_mix[i], w_gate, 0, LANES,
                                 [F32], tm, LANES, name="mlstm_gates")
            c0, n0, m0 = state[0][j], state[1][j], state[2][j]
            y, c1, n1, m1 = mlstm_cell(qkv, og, gates, b_gate_m[j], g_head_m[j], c0, n0, m0,
                                       rows_per_seq, cfg["mlstm_seqs"] if small else 1)
            new_state.append((c1, n1, m1))
            w_out = w_out_m
        else:
            lam_init = 0.8 - 0.6 * math.exp(-0.3 * i)
            (q,) = norm_proj(x, rows_per_seq, mod, 1, 0, g_norm_mix[i], w_in_d[j], 0, d,
                             [BF if paged is None else F32], tm, tn_seg, seg_gain=g_q_d[j], name="attn_q")
            k32, kbf = norm_proj(x, rows_per_seq, mod, 1, 0, g_norm_mix[i], w_in_d[j], d, d,
                                 [F32, BF], tm, tn_seg, seg_gain=g_k_d[j], transposed=paged is None,
                                 name="attn_k")
            v32, vbf = norm_proj(x, rows_per_seq, mod, 1, 0, g_norm_mix[i], w_in_d[j], 2 * d, d,
                                 [F32, BF], tm, tn_proj, name="attn_v")
            if paged is None:
                bound = (dh ** 0.5) * jnp.max(jnp.abs(g_q_d[j])) * jnp.max(jnp.abs(g_k_d[j]))
                y = flash_diff_attention(q, kbf, vbf, bound, lam_d[j], g_sub_d[j], heads_a,
                                         lam_init, cfg["tq"], cfg["tk"])
                k32 = jnp.transpose(k32.reshape(n_seq, heads_a, 2, dh, rows_per_seq), (0, 4, 1, 2, 3))
            else:
                cache_kt, cache_v, page_table = paged
                y = paged_diff_attention(q, k32, v32, cache_kt, cache_v, j, page_table, lam_d[j],
                                         g_sub_d[j], heads_a, lam_init)
            new_kv.append((k32, v32))
            w_out = w_out_d
        x = down_residual(y, w_out, j, x, rows_per_seq, mod, 2, tm, tn_proj, name="mixer_out")
        if i % 2 == 0:
            act = swiglu_up(x, rows_per_seq, mod, 4, 3, g_norm_ffn[i], w_gu_f, j, tm, tn_up)
            x = down_residual(act, w_down_f, j, x, rows_per_seq, mod, 5, tm, tn_proj, name="ffn_down")
        else:
            n_exp = w_router.shape[2]
            h, dg, rank, counts = router_gates(x, rows_per_seq, mod, 4, 3, g_norm_ffn[i],
                                               w_router[j], b_router[j], tm)
            counts = counts[:, 0, :n_exp].astype(jnp.int32).reshape(-1)
            x = moe_ffn(h, dg, rank, counts, w_gu_e, w_down_e, j, x, rows_per_seq, mod, 5, tm,
                        cfg["moe_ff_chunks"], min(cfg["moe_block_rows"], tm))
    return x, new_state, new_kv


def kernel(x_prompt, x_sample, c_prompt, c_sample, state_C, state_n, state_m, cache_k, cache_v, page_table, w_ada, b_ada, g_norm_mix, g_norm_ffn, w_in_m, b_gate_m, g_head_m, w_out_m, w_in_d, g_q_d, g_k_d, lam_d, g_sub_d, w_out_d, w_gu_f, w_down_f, w_router, b_router, w_gu_e, w_down_e):
    bp, seq, d = x_prompt.shape
    bs, dec_seq, _ = x_sample.shape
    depth = w_ada.shape[0]
    n_ml, _, heads_m, dk, dv = state_C.shape
    n_diff, n_pool, page, heads_a, _, dh = cache_k.shape
    weights = (g_norm_mix, g_norm_ffn, w_in_m.astype(BF), b_gate_m, g_head_m, w_out_m.astype(BF),
               w_in_d.astype(BF), g_q_d, g_k_d, lam_d, g_sub_d, w_out_d.astype(BF),
               w_gu_f.astype(BF), w_down_f.astype(BF),
               w_router, b_router, w_gu_e.astype(BF), w_down_e.astype(BF))

    n_c = bp + bs
    n_c_pad = -(-n_c // SUBLANES) * SUBLANES
    c_all = jnp.pad(jnp.concatenate([c_prompt, c_sample], axis=0), ((0, n_c_pad - n_c), (0, 0)))
    mod_all = ada_mod(c_all, w_ada, b_ada, tn=min(6 * d, 1536))
    mods_p = [mod_all[i, :bp].reshape(bp, 1, 6 * d) for i in range(depth)]
    mods_s = [mod_all[i, bp:n_c].reshape(bs, 1, 6 * d) for i in range(depth)]

    moe = dict(moe_ff_chunks=2, moe_block_rows=128)
    cfg_p = dict(tm=min(1024, seq), tq=min(1024, seq), tk=min(1024, seq), heads_a=heads_a, dh=dh, **moe)
    cfg_s = dict(tm=min(1024, bs * dec_seq), heads_a=heads_a, dh=dh, mlstm_seqs=min(8, bs), **moe)

    zeros = (jnp.zeros((n_ml, bp, heads_m, dk, dv), F32), jnp.zeros((n_ml, bp, heads_m, dk), F32),
             jnp.zeros((n_ml, bp, heads_m), F32))
    y_p, st_p, kv_p = _trunk(x_prompt.reshape(bp * seq, d), seq, mods_p, zeros, None, weights, cfg_p)
    paged = (jnp.transpose(cache_k, (0, 1, 3, 4, 5, 2)).reshape(n_diff, n_pool, heads_a * 2 * dh, page),
             cache_v.reshape(n_diff, n_pool, page * heads_a, 2 * dh), page_table)
    y_s, st_s, kv_s = _trunk(x_sample.reshape(bs * dec_seq, d), dec_seq, mods_s,
                             (state_C, state_n, state_m), paged, weights, cfg_s)

    def stack(items, idx, shape):
        return jnp.stack([it[idx] for it in items]).reshape(shape)

    return (y_p.reshape(bp, seq, d), y_s.reshape(bs, dec_seq, d),
            stack(st_p, 0, (n_ml, bp, heads_m, dk, dv)), stack(st_p, 1, (n_ml, bp, heads_m, dk)),
            stack(st_p, 2, (n_ml, bp, heads_m)),
            stack(st_s, 0, (n_ml, bs, heads_m, dk, dv)), stack(st_s, 1, (n_ml, bs, heads_m, dk)),
            stack(st_s, 2, (n_ml, bs, heads_m)),
            stack(kv_p, 0, (n_diff, bp, seq, heads_a, 2, dh)),
            stack(kv_p, 1, (n_diff, bp, seq, heads_a, 2 * dh)),
            stack(kv_s, 0, (n_diff, bs, dec_seq, heads_a, 2, dh)),
            stack(kv_s, 1, (n_diff, bs, dec_seq, heads_a, 2 * dh)))
```

```python
import functools
import math

import jax
import jax.numpy as jnp
from jax import lax
from jax.experimental import pallas as pl
from jax.experimental.pallas import tpu as pltpu

BF = jnp.bfloat16
F32 = jnp.float32
EPS = 1e-6
M_CHUNK = 128
LANES = 128
SUBLANES = 8
VMEM_LIMIT = 56 * 1024 * 1024
NEG_INF = float("-inf")


def _cparams(sem):
    return pltpu.CompilerParams(dimension_semantics=sem, vmem_limit_bytes=VMEM_LIMIT)


def _tile(n, pref):
    if n <= pref:
        return n
    t = pref - pref % LANES
    while n % t:
        t -= LANES
    return t


def _nt_dot(a, b, **kw):
    return lax.dot_general(a, b, (((1,), (1,)), ((), ())), preferred_element_type=F32, **kw)


def _tn_dot(a, b, **kw):
    return lax.dot_general(a, b, (((0,), (0,)), ((), ())), preferred_element_type=F32, **kw)


def _dot(a, b, **kw):
    return jnp.dot(a, b, preferred_element_type=F32, **kw)


def _ada_kernel(c_ref, w_ref, b_ref, o_ref):
    c = c_ref[...]
    a = (c * jax.nn.sigmoid(c)).astype(BF)
    o_ref[0] = _dot(a, w_ref[0].astype(BF)) + b_ref[0]


def ada_mod(c_all, w_ada, b_ada, tn):
    depth, d, n = w_ada.shape
    bp = c_all.shape[0]
    return pl.pallas_call(
        _ada_kernel,
        grid=(depth, n // tn),
        in_specs=[pl.BlockSpec((bp, d), lambda i, j: (0, 0)),
                  pl.BlockSpec((1, d, tn), lambda i, j: (i, 0, j)),
                  pl.BlockSpec((1, 1, tn), lambda i, j: (i, 0, j))],
        out_specs=pl.BlockSpec((1, bp, tn), lambda i, j: (i, 0, j)),
        out_shape=jax.ShapeDtypeStruct((depth, bp, n), F32),
        compiler_params=_cparams(("arbitrary", "arbitrary")),
        name="ada_mod",
    )(c_all, w_ada, b_ada.reshape(depth, 1, n))


def _mod_spec(rows_per_seq, tm, width, col_block):
    if rows_per_seq >= tm:
        npb = rows_per_seq // tm
        return pl.BlockSpec((1, 1, width), lambda *g: (g[0] // npb, 0, col_block(*g)))
    gb = tm // rows_per_seq
    return pl.BlockSpec((gb, 1, width), lambda *g: (g[0], 0, col_block(*g)))


def _rows(m_ref, tm):
    m = m_ref[...]
    gb, _, w = m.shape
    if gb == 1:
        return m[0]
    return jnp.broadcast_to(m, (gb, tm // gb, w)).reshape(tm, w)


def _modulated(x_ref, g_ref, sc_ref, sh_ref):
    x = x_ref[...]
    tm = x.shape[0]
    y = x * lax.rsqrt(jnp.mean(x * x, axis=-1, keepdims=True) + EPS) * g_ref[...]
    return y * (1.0 + _rows(sc_ref, tm)) + _rows(sh_ref, tm)


def _proj_kernel(x_ref, g_ref, sc_ref, sh_ref, w_ref, cs_ref, *rest):
    outs, h_scr = rest[:-1], rest[-1]

    @pl.when(pl.program_id(1) == 0)
    def _():
        h_scr[...] = _modulated(x_ref, g_ref, sc_ref, sh_ref).astype(BF)

    acc = _dot(h_scr[...], w_ref[...].astype(BF)) * cs_ref[...]
    for o in outs:
        o[...] = acc.astype(o.dtype)


def _segnorm_kernel(x_ref, g_ref, sc_ref, sh_ref, w_ref, gv_ref, seg_ref, *rest, seg, transposed,
                    n_alias):
    outs, h_scr = rest[n_alias:-1], rest[-1]

    @pl.when(pl.program_id(1) == 0)
    def _():
        h_scr[...] = _modulated(x_ref, g_ref, sc_ref, sh_ref).astype(BF)

    acc = _dot(h_scr[...], w_ref[...].astype(BF))
    ssq = _dot((acc * acc).astype(BF), seg_ref[...])
    y = acc * lax.rsqrt(ssq * (1.0 / seg) + EPS) * gv_ref[...]
    if transposed:
        yt = y.T
        for o in outs:
            o[(0,) * (len(o.shape) - 2)] = yt.astype(o.dtype)
    else:
        for o in outs:
            o[...] = y.astype(o.dtype)


def norm_proj(x, rows_per_seq, mod, sc_chunk, sh_chunk, g, w, col_off, n, out_dtypes, tm, tn,
              col_scale=None, seg_gain=None, transposed=False, stacked=None, name="norm_proj"):
    t, d = x.shape
    prev = []
    tn = _tile(math.gcd(n, col_off) if col_off else n, tn)
    x_spec = pl.BlockSpec((tm, d), lambda i, j: (i, 0))
    g_spec = pl.BlockSpec((1, d), lambda i, j: (0, 0))
    sc_spec = _mod_spec(rows_per_seq, tm, d, lambda i, j: sc_chunk)
    sh_spec = _mod_spec(rows_per_seq, tm, d, lambda i, j: sh_chunk)
    off = col_off // tn
    w_spec = pl.BlockSpec((d, tn), lambda i, j: (0, off + j))
    out_specs = [pl.BlockSpec((tm, tn), lambda i, j: (i, j)) for _ in out_dtypes]
    out_shape = [jax.ShapeDtypeStruct((t, n), dt) for dt in out_dtypes]
    if seg_gain is None:
        if col_scale is None:
            col_scale = jnp.ones((1, n), F32)
        kern = _proj_kernel
        extra = [col_scale]
        extra_specs = [pl.BlockSpec((1, tn), lambda i, j: (0, j))]
    else:
        seg = seg_gain.shape[0]
        if transposed:
            npb = rows_per_seq // tm
            out_specs = [pl.BlockSpec((1, tn, tm), lambda i, j: (i // npb, j, i % npb))
                         for _ in out_dtypes]
            out_shape = [jax.ShapeDtypeStruct((t // rows_per_seq, n, rows_per_seq), dt)
                         for dt in out_dtypes]
            if stacked is not None:
                layer, n_layers, previous = stacked
                out_specs[0] = pl.BlockSpec((1, 1, tn, tm), lambda i, j: (layer, i // npb, j, i % npb))
                out_shape[0] = jax.ShapeDtypeStruct((n_layers,) + out_shape[0].shape, out_dtypes[0])
                prev = [] if previous is None else [previous]
        kern = functools.partial(_segnorm_kernel, seg=seg, transposed=transposed, n_alias=len(prev))
        gv = jnp.tile(seg_gain.astype(F32), tn // seg).reshape(1, tn)
        ids = jnp.arange(tn) // seg
        seg_mat = (ids[:, None] == ids[None, :]).astype(BF)
        extra = [gv, seg_mat]
        extra_specs = [pl.BlockSpec((1, tn), lambda i, j: (0, 0)),
                       pl.BlockSpec((tn, tn), lambda i, j: (0, 0))]
    res = pl.pallas_call(
        kern,
        grid=(t // tm, n // tn),
        in_specs=[x_spec, g_spec, sc_spec, sh_spec, w_spec] + extra_specs
                 + [pl.BlockSpec(memory_space=pl.ANY)] * len(prev),
        out_specs=out_specs,
        out_shape=out_shape,
        scratch_shapes=[pltpu.VMEM((tm, d), BF)],
        input_output_aliases={5 + len(extra) + i: i for i in range(len(prev))},
        compiler_params=_cparams(("parallel", "arbitrary")),
        name=name,
    )(x, g.reshape(1, d), mod, mod, w, *extra, *prev)
    return res


def _router_kernel(x_ref, g_ref, sc_ref, sh_ref, w_ref, b_ref, tri_ref,
                   h_ref, o_ref, pos_ref, cnt_ref, *, n_experts):
    h = _modulated(x_ref, g_ref, sc_ref, sh_ref)
    h_ref[...] = h.astype(h_ref.dtype)
    logits = _dot(h, w_ref[...], precision=lax.Precision.HIGHEST) + b_ref[...]
    lane = lax.broadcasted_iota(jnp.int32, logits.shape, 1).astype(F32)
    big = float(LANES)
    l1 = jnp.where(lane < n_experts, logits, NEG_INF)
    m1 = jnp.max(l1, axis=-1, keepdims=True)
    i1 = jnp.min(jnp.where(l1 == m1, lane, big), axis=-1, keepdims=True)
    l2 = jnp.where(lane == i1, NEG_INF, l1)
    m2 = jnp.max(l2, axis=-1, keepdims=True)
    i2 = jnp.min(jnp.where(l2 == m2, lane, big), axis=-1, keepdims=True)
    e2 = jnp.exp(m2 - m1)
    w1 = 1.0 / (1.0 + e2)
    w2 = e2 / (1.0 + e2)
    gate = jnp.where(lane == i1, w1, 0.0) + jnp.where(lane == i2, w2, 0.0)
    o_ref[...] = gate
    routed = jnp.where(gate > 0.0, 1.0, 0.0)
    pos_ref[...] = _dot(tri_ref[...], routed.astype(BF))
    cnt_ref[0] = jnp.sum(routed, axis=0, keepdims=True)


def router_gates(x, rows_per_seq, mod, sc_chunk, sh_chunk, g, w_router, b_router, tm):
    t, d = x.shape
    e = w_router.shape[1]
    w_pad = jnp.pad(w_router, ((0, 0), (0, LANES - e)))
    b_pad = jnp.pad(b_router, (0, LANES - e)).reshape(1, LANES)
    idx = jnp.arange(tm)
    strict_lower = (idx[:, None] > idx[None, :]).astype(BF)
    tok = pl.BlockSpec((tm, LANES), lambda i: (i, 0))
    return pl.pallas_call(
        functools.partial(_router_kernel, n_experts=e),
        grid=(t // tm,),
        in_specs=[pl.BlockSpec((tm, d), lambda i: (i, 0)),
                  pl.BlockSpec((1, d), lambda i: (0, 0)),
                  _mod_spec(rows_per_seq, tm, d, lambda i: sc_chunk),
                  _mod_spec(rows_per_seq, tm, d, lambda i: sh_chunk),
                  pl.BlockSpec((d, LANES), lambda i: (0, 0)),
                  pl.BlockSpec((1, LANES), lambda i: (0, 0)),
                  pl.BlockSpec((tm, tm), lambda i: (0, 0))],
        out_specs=[pl.BlockSpec((tm, d), lambda i: (i, 0)), tok, tok,
                   pl.BlockSpec((1, 1, LANES), lambda i: (i, 0, 0))],
        out_shape=[jax.ShapeDtypeStruct((t, d), BF),
                   jax.ShapeDtypeStruct((t, LANES), F32),
                   jax.ShapeDtypeStruct((t, LANES), F32),
                   jax.ShapeDtypeStruct((t // tm, 1, LANES), F32)],
        compiler_params=_cparams(("parallel",)),
        name="router",
    )(x, g.reshape(1, d), mod, mod, w_pad, b_pad, strict_lower)


def _moe_kernel(cnt_ref, h_ref, dg_ref, pos_ref, wg_ref, wu_ref, wd_ref, x_ref, gt_ref, o_ref,
                xs_scr, y_scr, *, n_experts, br):
    ti, e, fc = pl.program_id(0), pl.program_id(1), pl.program_id(2)
    last_fc = pl.num_programs(2) - 1
    tm = h_ref.shape[0]

    @pl.when((e == 0) & (fc == 0))
    def _():
        o_ref[...] = jnp.zeros_like(o_ref)

    sel = lax.broadcasted_iota(jnp.int32, (tm, LANES), 1) == e
    ge = jnp.sum(jnp.where(sel, dg_ref[...], 0.0), axis=-1, keepdims=True)
    pos = jnp.sum(jnp.where(sel, pos_ref[...], 0.0), axis=-1, keepdims=True)
    routed = ge > 0.0
    n_blocks = (cnt_ref[ti * n_experts + e] + br - 1) // br

    def block(rb, carry):
        r0 = pl.multiple_of(rb * br, br)
        rows = pl.ds(r0, br)
        slot = (r0 + lax.broadcasted_iota(jnp.int32, (tm, br), 1)).astype(F32)
        onehot = jnp.where((pos == slot) & routed, 1.0, 0.0).astype(BF)

        @pl.when(fc == 0)
        def _():
            xs_scr[rows, :] = _tn_dot(onehot, h_ref[...]).astype(BF)

        xs = xs_scr[rows, :]
        a = _dot(xs, wg_ref[0, 0])
        u = _dot(xs, wu_ref[0, 0])
        y = _dot((a * jax.nn.sigmoid(a) * u).astype(BF), wd_ref[0, 0])

        @pl.when(fc == 0)
        def _():
            y_scr[rows, :] = y

        @pl.when(fc > 0)
        def _():
            y_scr[rows, :] += y

        @pl.when(fc == last_fc)
        def _():
            o_ref[...] += ge * _dot(onehot, y_scr[rows, :].astype(BF))
        return carry

    lax.fori_loop(0, n_blocks, block, 0)

    @pl.when((e == n_experts - 1) & (fc == last_fc))
    def _():
        o_ref[...] = x_ref[...] + _rows(gt_ref, tm) * o_ref[...]


def moe_ffn(h, dense_gate, rank, counts, w_gu, w_down, layer, x, rows_per_seq, mod, gt_chunk, tm,
            ff_chunks, block_rows):
    t, d = x.shape
    _, e, _, f2 = w_gu.shape
    f = f2 // 2
    fcw = f // ff_chunks
    tok = pl.BlockSpec((tm, LANES), lambda i, k, c, cnt: (i, 0))
    grid_spec = pltpu.PrefetchScalarGridSpec(
        num_scalar_prefetch=1,
        grid=(t // tm, e, ff_chunks),
        in_specs=[pl.BlockSpec((tm, d), lambda i, k, c, cnt: (i, 0)), tok, tok,
                  pl.BlockSpec((1, 1, d, fcw), lambda i, k, c, cnt: (layer, k, 0, c)),
                  pl.BlockSpec((1, 1, d, fcw), lambda i, k, c, cnt: (layer, k, 0, ff_chunks + c)),
                  pl.BlockSpec((1, 1, fcw, d), lambda i, k, c, cnt: (layer, k, c, 0)),
                  pl.BlockSpec((tm, d), lambda i, k, c, cnt: (i, 0)),
                  _mod_spec(rows_per_seq, tm, d, lambda i, k, c, cnt: gt_chunk)],
        out_specs=pl.BlockSpec((tm, d), lambda i, k, c, cnt: (i, 0)),
        scratch_shapes=[pltpu.VMEM((tm, d), BF), pltpu.VMEM((tm, d), F32)],
    )
    return pl.pallas_call(
        functools.partial(_moe_kernel, n_experts=e, br=block_rows),
        grid_spec=grid_spec,
        out_shape=jax.ShapeDtypeStruct((t, d), F32),
        compiler_params=_cparams(("parallel", "arbitrary", "arbitrary")),
        name="moe_ffn",
    )(counts, h, dense_gate, rank, w_gu, w_gu, w_down, x, mod)


def _swiglu_up_kernel(x_ref, g_ref, sc_ref, sh_ref, wg_ref, wu_ref, o_ref, h_scr):
    @pl.when(pl.program_id(1) == 0)
    def _():
        h_scr[...] = _modulated(x_ref, g_ref, sc_ref, sh_ref).astype(BF)

    h = h_scr[...]
    a = _dot(h, wg_ref[0].astype(BF))
    u = _dot(h, wu_ref[0].astype(BF))
    o_ref[...] = (a * jax.nn.sigmoid(a) * u).astype(o_ref.dtype)


def swiglu_up(x, rows_per_seq, mod, sc_chunk, sh_chunk, g, w_gu, layer, tm, tn):
    t, d = x.shape
    f = w_gu.shape[2] // 2
    tn = _tile(f, tn)
    nf = f // tn
    return pl.pallas_call(
        _swiglu_up_kernel,
        grid=(t // tm, nf),
        in_specs=[pl.BlockSpec((tm, d), lambda i, j: (i, 0)),
                  pl.BlockSpec((1, d), lambda i, j: (0, 0)),
                  _mod_spec(rows_per_seq, tm, d, lambda i, j: sc_chunk),
                  _mod_spec(rows_per_seq, tm, d, lambda i, j: sh_chunk),
                  pl.BlockSpec((1, d, tn), lambda i, j: (layer, 0, j)),
                  pl.BlockSpec((1, d, tn), lambda i, j: (layer, 0, nf + j))],
        out_specs=pl.BlockSpec((tm, tn), lambda i, j: (i, j)),
        out_shape=jax.ShapeDtypeStruct((t, f), BF),
        scratch_shapes=[pltpu.VMEM((tm, d), BF)],
        compiler_params=_cparams(("parallel", "arbitrary")),
        name="swiglu_up",
    )(x, g.reshape(1, d), mod, mod, w_gu, w_gu)


def _down_kernel(y_ref, w_ref, x_ref, gt_ref, o_ref):
    p = _dot(y_ref[...].astype(BF), w_ref[0].astype(BF))
    o_ref[...] = x_ref[...] + _rows(gt_ref, x_ref.shape[0]) * p


def down_residual(y, w, layer, x, rows_per_seq, mod, gt_chunk, tm, tn, name="down_residual"):
    t, k = y.shape
    d = x.shape[1]
    tn = _tile(d, tn)
    nd = d // tn
    return pl.pallas_call(
        _down_kernel,
        grid=(t // tm, nd),
        in_specs=[pl.BlockSpec((tm, k), lambda i, j: (i, 0)),
                  pl.BlockSpec((1, k, tn), lambda i, j: (layer, 0, j)),
                  pl.BlockSpec((tm, tn), lambda i, j: (i, j)),
                  _mod_spec(rows_per_seq, tm, tn, lambda i, j: gt_chunk * nd + j)],
        out_specs=pl.BlockSpec((tm, tn), lambda i, j: (i, j)),
        out_shape=jax.ShapeDtypeStruct((t, d), F32),
        compiler_params=_cparams(("parallel", "arbitrary")),
        name=name,
    )(y, w, x, mod)


def _log_sigmoid(x):
    return jnp.minimum(x, 0.0) - jnp.log1p(jnp.exp(-jnp.abs(x)))


def _mlstm_kernel(qkv_ref, o_ref, gt_ref, bg_ref, gh_ref, c0_ref, n0_ref, m0_ref, *rest,
                  heads, dk, dv, chunk, seqs, unroll):
    y_ref, c_all, n_all, m_all = rest[-4:]
    c_ref, n_ref, m_ref = c_all.at[0], n_all.at[0], m_all.at[0]
    L = chunk
    hi = lax.Precision.HIGHEST
    cd = BF if L >= 16 else F32

    @pl.when(pl.program_id(1) == 0)
    def _():
        c_ref[...] = c0_ref[...]
        n_ref[...] = n0_ref[...]
        m_ref[...] = m0_ref[...]

    row = lax.broadcasted_iota(jnp.int32, (L, L), 0)
    col = lax.broadcasted_iota(jnp.int32, (L, L), 1)
    causal = row >= col
    tril = causal.astype(F32)
    sel = (lax.broadcasted_iota(jnp.int32, (SUBLANES, LANES), 0)
           == lax.broadcasted_iota(jnp.int32, (SUBLANES, LANES), 1)).astype(F32)

    def one_seq(bi):
        gates = gt_ref[bi] + bg_ref[...]
        lf = _log_sigmoid(gates)
        bcum = _dot(tril, lf, precision=hi)
        g_rows = _nt_dot(sel, gates, precision=hi)
        b_rows = _nt_dot(sel, bcum, precision=hi)
        for h in range(heads):
            q = qkv_ref[bi, :, h * dk:(h + 1) * dk]
            k = qkv_ref[bi, :, (heads + h) * dk:(heads + h + 1) * dk]
            v = qkv_ref[bi, :, 2 * heads * dk + h * dv:2 * heads * dk + (h + 1) * dv]
            qf, kf = q.astype(F32), k.astype(F32)
            qc, kc, vc = q.astype(cd), k.astype(cd), v.astype(cd)
            ig_col = gates[:, h:h + 1]
            ig_row = g_rows[h:h + 1, :]
            b_col = bcum[:, heads + h:heads + h + 1]
            b_row = b_rows[heads + h:heads + h + 1, :]
            b_end = bcum[L - 1:L, heads + h:heads + h + 1]
            c_old = c_ref[bi, h]
            n_old = n_ref[bi, h]
            m_old = m_ref[bi, h][:, :1]

            dlog = jnp.where(causal, b_col - b_row + ig_row, NEG_INF)
            gcar = b_col + m_old
            m_t = jnp.maximum(gcar, jnp.max(dlog, axis=-1, keepdims=True))
            s = _nt_dot(qc, kc) * jnp.exp(dlog - m_t)
            dec = jnp.exp(gcar - m_t)
            num = _dot(s.astype(cd), vc) + dec * _dot(qc, c_old.astype(cd))
            den = (jnp.sum(s, axis=-1, keepdims=True)
                   + dec * jnp.sum(qf * n_old, axis=-1, keepdims=True))
            hh = num / jnp.maximum(jnp.abs(den), jnp.exp(-m_t))

            a_row = b_end - b_row + ig_row
            a_col = b_end - b_col + ig_col
            m_new = jnp.maximum(b_end + m_old, jnp.max(a_row, axis=-1, keepdims=True))
            wk = jnp.exp(a_col - m_new)
            decay = jnp.exp(b_end + m_old - m_new)
            kw = kf * wk
            c_ref[bi, h] = decay * c_old + _tn_dot(kw.astype(cd), vc)
            n_ref[bi, h] = decay * n_old + jnp.sum(kw, axis=0, keepdims=True)
            m_ref[bi, h] = jnp.broadcast_to(m_new, (1, LANES))

            hs = (hh * lax.rsqrt(jnp.mean(hh * hh, axis=-1, keepdims=True) + EPS)
                  * gh_ref[:, h * dv:(h + 1) * dv])
            og = o_ref[bi, :, h * dv:(h + 1) * dv]
            y_ref[bi, :, h * dv:(h + 1) * dv] = (hs * jax.nn.sigmoid(og)).astype(y_ref.dtype)

    if unroll:
        for bi in range(seqs):
            one_seq(bi)
    else:
        def body(bi, carry):
            one_seq(bi)
            return carry
        lax.fori_loop(0, seqs, body, 0)


def mlstm_cell(qkv, og, gates, b_gate, g_head, c0, n0, m0, seq_len, seqs_per_step, layer, stacked):
    t = qkv.shape[0]
    b, heads, dk, dv = c0.shape
    chunk = M_CHUNK if seq_len % M_CHUNK == 0 else seq_len
    nc = seq_len // chunk
    bt = seqs_per_step
    n4 = n0.reshape(b, heads, 1, dk)
    m4 = jnp.broadcast_to(m0.reshape(b, heads, 1, 1), (b, heads, 1, LANES))
    bg = jnp.pad(b_gate.astype(F32), (0, LANES - 2 * heads)).reshape(1, LANES)
    y_dtype = BF if chunk >= 16 else F32
    tok = lambda i, c: (i, c, 0)
    st = lambda i, c: (i, 0, 0, 0)
    st_out = lambda i, c: (layer, i, 0, 0, 0)
    first = isinstance(stacked, int)
    n_layers = stacked if first else stacked[0].shape[0]
    prev = [] if first else list(stacked)
    n_in = 8
    y, c_all, n_all, m_all = pl.pallas_call(
        functools.partial(_mlstm_kernel, heads=heads, dk=dk, dv=dv, chunk=chunk, seqs=bt,
                          unroll=bt <= 2),
        grid=(b // bt, nc),
        in_specs=[pl.BlockSpec((bt, chunk, qkv.shape[1]), tok),
                  pl.BlockSpec((bt, chunk, heads * dv), tok),
                  pl.BlockSpec((bt, chunk, LANES), tok),
                  pl.BlockSpec((1, LANES), lambda i, c: (0, 0)),
                  pl.BlockSpec((1, heads * dv), lambda i, c: (0, 0)),
                  pl.BlockSpec((bt, heads, dk, dv), st),
                  pl.BlockSpec((bt, heads, 1, dk), st),
                  pl.BlockSpec((bt, heads, 1, LANES), st)]
                 + [pl.BlockSpec(memory_space=pl.ANY)] * len(prev),
        out_specs=[pl.BlockSpec((bt, chunk, heads * dv), tok),
                   pl.BlockSpec((1, bt, heads, dk, dv), st_out),
                   pl.BlockSpec((1, bt, heads, 1, dk), st_out),
                   pl.BlockSpec((1, bt, heads, 1, LANES), st_out)],
        out_shape=[jax.ShapeDtypeStruct((b, seq_len, heads * dv), y_dtype),
                   jax.ShapeDtypeStruct((n_layers, b, heads, dk, dv), F32),
                   jax.ShapeDtypeStruct((n_layers, b, heads, 1, dk), F32),
                   jax.ShapeDtypeStruct((n_layers, b, heads, 1, LANES), F32)],
        input_output_aliases={n_in + i: 1 + i for i in range(len(prev))},
        compiler_params=_cparams(("parallel", "arbitrary")),
        name="mlstm_cell",
    )(qkv.reshape(b, seq_len, -1), og.reshape(b, seq_len, -1), gates.reshape(b, seq_len, LANES),
      bg, g_head.reshape(1, heads * dv).astype(F32), c0, n4, m4, *prev)
    return y.reshape(t, heads * dv), (c_all, n_all, m_all)


def _lambda(lam_ref, lam_init):
    lp = lam_ref[...]
    a = jnp.sum(lp[0:1] * lp[1:2], axis=-1, keepdims=True)
    b = jnp.sum(lp[2:3] * lp[3:4], axis=-1, keepdims=True)
    return jnp.exp(a) - jnp.exp(b) + lam_init


def _sub_norm(o, gsub_ref, lam_init):
    y = o * lax.rsqrt(jnp.mean(o * o, axis=-1, keepdims=True) + EPS) * gsub_ref[...]
    return y * (1.0 - lam_init)


def _flash_kernel(qi_ref, ki_ref, q_ref, kt_ref, v_ref, bound_ref, lam_ref, gs_ref, o_ref,
                  qm_scr, m_scr, acc_scr, *, dh, tq, tk, lam_init, online):
    pair = pl.program_id(2)
    qi = qi_ref[pair]
    ki = ki_ref[pair]
    dv = LANES

    @pl.when(ki == 0)
    def _():
        q = q_ref[...] * (dh ** -0.5)
        lane = lax.broadcasted_iota(jnp.int32, q.shape, 1)
        qm_scr[0] = jnp.where(lane < dh, q, jnp.zeros_like(q))
        qm_scr[1] = jnp.where(lane >= dh, q, jnp.zeros_like(q))
        m_scr[...] = jnp.full_like(m_scr, NEG_INF)
        acc_scr[...] = jnp.zeros_like(acc_scr)

    def absorb(masked):
        kt = kt_ref[0]
        ones_col = (lax.broadcasted_iota(jnp.int32, (tk, LANES), 1) == 0).astype(BF)
        vx = jnp.concatenate([v_ref[...], ones_col], axis=1)
        if masked:
            row = qi * tq + lax.broadcasted_iota(jnp.int32, (tq, tk), 0)
            col = ki * tk + lax.broadcasted_iota(jnp.int32, (tq, tk), 1)
            keep = row >= col
        for c in range(2):
            s = _dot(qm_scr[c], kt)
            if masked:
                s = jnp.where(keep, s, NEG_INF)
            if online:
                m_prev = m_scr[c]
                m_new = jnp.maximum(m_prev, jnp.max(s, axis=-1, keepdims=True))
                p = jnp.exp(s - m_new)
                acc_scr[c] = jnp.exp(m_prev - m_new) * acc_scr[c] + _dot(p.astype(BF), vx)
                m_scr[c] = m_new
            else:
                p = jnp.exp(s - bound_ref[...])
                acc_scr[c] += _dot(p.astype(BF), vx)

    needs_mask = (ki + 1) * tk - 1 > qi * tq

    @pl.when(needs_mask)
    def _():
        absorb(True)

    @pl.when(jnp.logical_not(needs_mask))
    def _():
        absorb(False)

    @pl.when(ki == (qi * tq + tq - 1) // tk)
    def _():
        lam = _lambda(lam_ref, lam_init)
        a0 = acc_scr[0]
        a1 = acc_scr[1]
        o = a0[:, :dv] / a0[:, dv:dv + 1] - lam * (a1[:, :dv] / a1[:, dv:dv + 1])
        o_ref[...] = _sub_norm(o, gs_ref, lam_init).astype(o_ref.dtype)


SAFE_SCORE_BOUND = 30.0


def flash_diff_attention(q, kt, v, score_bound, lam_p, g_sub, heads, lam_init, tq, tk):
    t = q.shape[0]
    n_seq, _, seq = kt.shape
    dv = v.shape[1] // heads
    dh = q.shape[1] // (2 * heads)
    assert 2 * dh == LANES and dv == LANES
    nq, nk = seq // tq, seq // tk
    pairs = [(i, j) for i in range(nq) for j in range((i * tq + tq - 1) // tk + 1)]
    qi_tab = jnp.asarray([p[0] for p in pairs], jnp.int32)
    ki_tab = jnp.asarray([p[1] for p in pairs], jnp.int32)

    def call(online):
        grid_spec = pltpu.PrefetchScalarGridSpec(
            num_scalar_prefetch=2,
            grid=(n_seq, heads, len(pairs)),
            in_specs=[pl.BlockSpec((tq, LANES), lambda b, h, p, qt, kt_: (b * nq + qt[p], h)),
                      pl.BlockSpec((1, LANES, tk), lambda b, h, p, qt, kt_: (b, h, kt_[p])),
                      pl.BlockSpec((tk, LANES), lambda b, h, p, qt, kt_: (b * nk + kt_[p], h)),
                      pl.BlockSpec((1, 1), lambda b, h, p, qt, kt_: (0, 0)),
                      pl.BlockSpec(lam_p.shape, lambda b, h, p, qt, kt_: (0, 0)),
                      pl.BlockSpec((1, dv), lambda b, h, p, qt, kt_: (0, 0))],
            out_specs=pl.BlockSpec((tq, LANES), lambda b, h, p, qt, kt_: (b * nq + qt[p], h)),
            scratch_shapes=[pltpu.VMEM((2, tq, LANES), BF),
                            pltpu.VMEM((2, tq, 1), F32),
                            pltpu.VMEM((2, tq, dv + LANES), F32)],
        )
        return pl.pallas_call(
            functools.partial(_flash_kernel, dh=dh, tq=tq, tk=tk, lam_init=lam_init, online=online),
            grid_spec=grid_spec,
            out_shape=jax.ShapeDtypeStruct((t, heads * dv), BF),
            compiler_params=_cparams(("parallel", "parallel", "arbitrary")),
            name="flash_online" if online else "flash_bounded",
        )(qi_tab, ki_tab, q, kt, v, score_bound.reshape(1, 1).astype(F32), lam_p.astype(F32),
          g_sub.reshape(1, dv).astype(F32))

    return lax.cond(score_bound < SAFE_SCORE_BOUND, lambda: call(False), lambda: call(True))


def _paged_kernel(pt_ref, q_ref, kn_ref, vn_ref, lam_ref, gs_ref, *rest,
                  heads, dh, n_new, pages, page, lam_init):
    del pt_ref
    kt_refs = rest[:pages]
    v_refs = rest[pages:2 * pages]
    o_ref, s_scr = rest[2 * pages:]
    width = heads * 2 * dh
    nrow = 2 * heads * n_new
    past = pages * page

    q = q_ref[...] * (dh ** -0.5)
    qt = jnp.concatenate([q] * (2 * heads), axis=0)
    r = lax.broadcasted_iota(jnp.int32, (nrow, width), 0)
    cblk = lax.broadcasted_iota(jnp.int32, (nrow, width), 1) // dh
    qbd = jnp.where(cblk == r // n_new, qt, 0.0).astype(BF)

    pad = jnp.zeros((page - n_new, width), F32)
    kn = jnp.concatenate([kn_ref[...], pad], axis=0).astype(BF)
    vn = jnp.concatenate([vn_ref[...], pad], axis=0).astype(BF)

    def softmax_part(page_ids, with_new):
        c0 = page_ids[0] * page
        c1 = (page_ids[-1] + 1) * page
        for p in page_ids:
            s_scr[:, p * page:(p + 1) * page] = _dot(qbd, kt_refs[p][0, 0].astype(BF))
        if with_new:
            s_new = _nt_dot(qbd, kn)
            tok = lax.broadcasted_iota(jnp.int32, s_new.shape, 0) % n_new
            key = lax.broadcasted_iota(jnp.int32, s_new.shape, 1)
            s_scr[:, past:] = jnp.where(key <= tok, s_new, NEG_INF)
            c1 = past + page
        s = s_scr[:, c0:c1]
        m = jnp.max(s, axis=-1, keepdims=True)
        pr = jnp.exp(s - m)
        l = jnp.sum(pr, axis=-1, keepdims=True)
        s_scr[:, c0:c1] = pr
        acc = _dot(s_scr[:, past:].astype(BF), vn) if with_new else jnp.zeros((nrow, width), F32)
        for p in page_ids:
            v_wide = jnp.concatenate(
                [v_refs[p][0, 0, pl.ds(h, page, stride=heads), :] for h in range(heads)], axis=1)
            acc += _dot(s_scr[:, p * page:(p + 1) * page].astype(BF), v_wide.astype(BF))
        return acc, l, m

    half = max(pages // 2, 1)
    acc, l, m = softmax_part(list(range(half, pages)), True) if half < pages else (None, None, None)
    acc_a, l_a, m_a = softmax_part(list(range(half)), half == pages)
    if acc is None:
        acc, l = acc_a, l_a
    else:
        m_all = jnp.maximum(m, m_a)
        w, w_a = jnp.exp(m - m_all), jnp.exp(m_a - m_all)
        acc = w * acc + w_a * acc_a
        l = w * l + w_a * l_a

    lam = _lambda(lam_ref, lam_init)
    acc = acc * (1.0 / l)
    for h in range(heads):
        r0 = h * 2 * n_new
        cols = slice(h * 2 * dh, (h + 1) * 2 * dh)
        o = acc[r0:r0 + n_new, cols] - lam * acc[r0 + n_new:r0 + 2 * n_new, cols]
        o_ref[:, cols] = _sub_norm(o, gs_ref, lam_init)


def paged_diff_attention(q, k_new, v_new, cache_kt, cache_v, layer, page_table, lam_p, g_sub,
                         heads, lam_init):
    t, width = q.shape
    b, n_pages = page_table.shape
    n_new = t // b
    dh = width // (2 * heads)
    page = cache_kt.shape[3]
    assert n_new == SUBLANES and 2 * dh == LANES and page == LANES
    nrow = 2 * heads * n_new
    tok = pl.BlockSpec((n_new, width), lambda i, pt: (i, 0))

    def page_spec(p, shape):
        return pl.BlockSpec((1, 1) + shape, lambda i, pt: (layer, pt[i, p], 0, 0))

    grid_spec = pltpu.PrefetchScalarGridSpec(
        num_scalar_prefetch=1,
        grid=(b,),
        in_specs=[tok, tok, tok,
                  pl.BlockSpec(lam_p.shape, lambda i, pt: (0, 0)),
                  pl.BlockSpec((1, 2 * dh), lambda i, pt: (0, 0))]
                 + [page_spec(p, (width, page)) for p in range(n_pages)]
                 + [page_spec(p, (page * heads, 2 * dh)) for p in range(n_pages)],
        out_specs=tok,
        scratch_shapes=[pltpu.VMEM((nrow, (n_pages + 1) * page), F32)],
    )
    return pl.pallas_call(
        functools.partial(_paged_kernel, heads=heads, dh=dh, n_new=n_new, pages=n_pages,
                          page=page, lam_init=lam_init),
        grid_spec=grid_spec,
        out_shape=jax.ShapeDtypeStruct((t, width), F32),
        compiler_params=_cparams(("parallel",)),
        name="paged_diff_attention",
    )(page_table, q, k_new, v_new, lam_p.astype(F32), g_sub.reshape(1, 2 * dh).astype(F32),
      *([cache_kt] * n_pages), *([cache_v] * n_pages))


def _trunk(x, rows_per_seq, mods, state, paged, weights, cfg):
    (g_norm_mix, g_norm_ffn, w_in_m, b_gate_m, g_head_m, w_out_m,
     w_in_d, g_q_d, g_k_d, lam_d, g_sub_d, w_out_d, w_gu_f, w_down_f,
     w_router, b_router, w_gu_e, w_down_e) = weights
    t, d = x.shape
    n_seq = t // rows_per_seq
    depth = g_norm_mix.shape[0]
    tm = cfg["tm"]
    tn_proj, tn_seg, tn_up = 1024, 256, 1408
    heads_m = b_gate_m.shape[1] // 2
    dk, dv = state[0].shape[3], state[0].shape[4]
    qk_w = heads_m * dk
    heads_a, dh = cfg["heads_a"], cfg["dh"]
    new_state, new_kv = state[0].shape[0], []
    kt_stack = None
    for i in range(depth):
        j = i // 2
        mod = mods[i]
        if i % 2 == 0:
            col_scale = jnp.concatenate([jnp.ones((qk_w,), F32), jnp.full((qk_w,), dk ** -0.5, F32),
                                         jnp.ones((heads_m * dv,), F32)]).reshape(1, -1)
            small = rows_per_seq % M_CHUNK != 0
            (qkv,) = norm_proj(x, rows_per_seq, mod, 1, 0, g_norm_mix[i], w_in_m[j], 0,
                               2 * qk_w + heads_m * dv, [F32 if small else BF], tm, tn_proj,
                               col_scale=col_scale, name="mlstm_qkv")
            (og,) = norm_proj(x, rows_per_seq, mod, 1, 0, g_norm_mix[i], w_in_m[j],
                              2 * qk_w + heads_m * dv, heads_m * dv, [F32], tm, tn_proj, name="mlstm_o")
            w_gate = jnp.pad(w_in_m[j][:, 2 * qk_w + 2 * heads_m * dv:], ((0, 0), (0, LANES - 2 * heads_m)))
            (gates,) = norm_proj(x, rows_per_seq, mod, 1, 0, g_norm_mix[i], w_gate, 0, LANES,
                                 [F32], tm, LANES, name="mlstm_gates")
            c0, n0, m0 = state[0][j], state[1][j], state[2][j]
            y, new_state = mlstm_cell(qkv, og, gates, b_gate_m[j], g_head_m[j], c0, n0, m0,
                                      rows_per_seq, cfg["mlstm_seqs"], j, new_state)
            w_out = w_out_m
        else:
            lam_init = 0.8 - 0.6 * math.exp(-0.3 * i)
            (q,) = norm_proj(x, rows_per_seq, mod, 1, 0, g_norm_mix[i], w_in_d[j], 0, d,
                             [BF if paged is None else F32], tm, tn_seg, seg_gain=g_q_d[j], name="attn_q")
            k32, kbf = norm_proj(x, rows_per_seq, mod, 1, 0, g_norm_mix[i], w_in_d[j], d, d,
                                 [F32, BF], tm, tn_seg, seg_gain=g_k_d[j], transposed=paged is None,
                                 stacked=(j, depth // 2, kt_stack), name="attn_k")
            v32, vbf = norm_proj(x, rows_per_seq, mod, 1, 0, g_norm_mix[i], w_in_d[j], 2 * d, d,
                                 [F32, BF], tm, tn_proj, name="attn_v")
            if paged is None:
                bound = (dh ** 0.5) * jnp.max(jnp.abs(g_q_d[j])) * jnp.max(jnp.abs(g_k_d[j]))
                y = flash_diff_attention(q, kbf, vbf, bound, lam_d[j], g_sub_d[j], heads_a,
                                         lam_init, cfg["tq"], cfg["tk"])
                kt_stack = k32
            else:
                cache_kt, cache_v, page_table = paged
                y = paged_diff_attention(q, k32, v32, cache_kt, cache_v, j, page_table, lam_d[j],
                                         g_sub_d[j], heads_a, lam_init)
            new_kv.append((k32, v32))
            w_out = w_out_d
        x = down_residual(y, w_out, j, x, rows_per_seq, mod, 2, tm, tn_proj, name="mixer_out")
        if i % 2 == 0:
            act = swiglu_up(x, rows_per_seq, mod, 4, 3, g_norm_ffn[i], w_gu_f, j, tm, tn_up)
            x = down_residual(act, w_down_f, j, x, rows_per_seq, mod, 5, tm, tn_proj, name="ffn_down")
        else:
            n_exp = w_router.shape[2]
            h, dg, rank, counts = router_gates(x, rows_per_seq, mod, 4, 3, g_norm_ffn[i],
                                               w_router[j], b_router[j], tm)
            counts = counts[:, 0, :n_exp].astype(jnp.int32).reshape(-1)
            x = moe_ffn(h, dg, rank, counts, w_gu_e, w_down_e, j, x, rows_per_seq, mod, 5, tm,
                        cfg["moe_ff_chunks"], min(cfg["moe_block_rows"], tm))
    return x, new_state, new_kv, kt_stack


def kernel(x_prompt, x_sample, c_prompt, c_sample, state_C, state_n, state_m, cache_k, cache_v, page_table, w_ada, b_ada, g_norm_mix, g_norm_ffn, w_in_m, b_gate_m, g_head_m, w_out_m, w_in_d, g_q_d, g_k_d, lam_d, g_sub_d, w_out_d, w_gu_f, w_down_f, w_router, b_router, w_gu_e, w_down_e):
    bp, seq, d = x_prompt.shape
    bs, dec_seq, _ = x_sample.shape
    depth = w_ada.shape[0]
    n_ml, _, heads_m, dk, dv = state_C.shape
    n_diff, n_pool, page, heads_a, _, dh = cache_k.shape
    weights = (g_norm_mix, g_norm_ffn, w_in_m.astype(BF), b_gate_m, g_head_m, w_out_m.astype(BF),
               w_in_d.astype(BF), g_q_d, g_k_d, lam_d, g_sub_d, w_out_d.astype(BF),
               w_gu_f.astype(BF), w_down_f.astype(BF),
               w_router, b_router, w_gu_e.astype(BF), w_down_e.astype(BF))

    n_c = bp + bs
    n_c_pad = -(-n_c // SUBLANES) * SUBLANES
    c_all = jnp.pad(jnp.concatenate([c_prompt, c_sample], axis=0), ((0, n_c_pad - n_c), (0, 0)))
    mod_all = ada_mod(c_all, w_ada, b_ada, tn=min(6 * d, 1536))
    mods_p = [mod_all[i, :bp].reshape(bp, 1, 6 * d) for i in range(depth)]
    mods_s = [mod_all[i, bp:n_c].reshape(bs, 1, 6 * d) for i in range(depth)]

    moe = dict(moe_ff_chunks=2, moe_block_rows=128)
    cfg_p = dict(tm=min(1024, seq), tq=min(1024, seq), tk=min(1024, seq), heads_a=heads_a, dh=dh,
                 mlstm_seqs=min(2, bp), **moe)
    cfg_s = dict(tm=min(1024, bs * dec_seq), heads_a=heads_a, dh=dh, mlstm_seqs=min(8, bs), **moe)

    zeros = (jnp.zeros((n_ml, bp, heads_m, dk, dv), F32), jnp.zeros((n_ml, bp, heads_m, dk), F32),
             jnp.zeros((n_ml, bp, heads_m), F32))
    y_p, st_p, kv_p, kt_p = _trunk(x_prompt.reshape(bp * seq, d), seq, mods_p, zeros, None, weights, cfg_p)
    paged = (jnp.transpose(cache_k, (0, 1, 3, 4, 5, 2)).reshape(n_diff, n_pool, heads_a * 2 * dh, page),
             cache_v.reshape(n_diff, n_pool, page * heads_a, 2 * dh), page_table)
    y_s, st_s, kv_s, _ = _trunk(x_sample.reshape(bs * dec_seq, d), dec_seq, mods_s,
                             (state_C, state_n, state_m), paged, weights, cfg_s)

    def stack(items, idx, shape):
        return jnp.stack([it[idx] for it in items]).reshape(shape)

    return (y_p.reshape(bp, seq, d), y_s.reshape(bs, dec_seq, d),
            st_p[0], st_p[1].reshape(n_ml, bp, heads_m, dk), st_p[2][:, :, :, 0, 0],
            st_s[0], st_s[1].reshape(n_ml, bs, heads_m, dk), st_s[2][:, :, :, 0, 0],
            jnp.transpose(kt_p.reshape(n_diff, bp, heads_a, 2, dh, seq), (0, 1, 5, 2, 3, 4)),
            stack(kv_p, 1, (n_diff, bp, seq, heads_a, 2 * dh)),
            stack(kv_s, 0, (n_diff, bs, dec_seq, heads_a, 2, dh)),
            stack(kv_s, 1, (n_diff, bs, dec_seq, heads_a, 2 * dh)))
```

```python
import functools
import math

import jax
import jax.numpy as jnp
from jax import lax
from jax.experimental import pallas as pl
from jax.experimental.pallas import tpu as pltpu

BF = jnp.bfloat16
F32 = jnp.float32
EPS = 1e-6
M_CHUNK = 128
LANES = 128
SUBLANES = 8
VMEM_LIMIT = 56 * 1024 * 1024
NEG_INF = float("-inf")


def _cparams(sem):
    return pltpu.CompilerParams(dimension_semantics=sem, vmem_limit_bytes=VMEM_LIMIT)


def _tile(n, pref):
    if n <= pref:
        return n
    t = pref - pref % LANES
    while n % t:
        t -= LANES
    return t


def _nt_dot(a, b, **kw):
    return lax.dot_general(a, b, (((1,), (1,)), ((), ())), preferred_element_type=F32, **kw)


def _tn_dot(a, b, **kw):
    return lax.dot_general(a, b, (((0,), (0,)), ((), ())), preferred_element_type=F32, **kw)


def _dot(a, b, **kw):
    return jnp.dot(a, b, preferred_element_type=F32, **kw)


def _ada_kernel(c_ref, w_ref, b_ref, o_ref):
    c = c_ref[...]
    a = (c * jax.nn.sigmoid(c)).astype(BF)
    o_ref[0] = _dot(a, w_ref[0].astype(BF)) + b_ref[0]


def ada_mod(c_all, w_ada, b_ada, tn):
    depth, d, n = w_ada.shape
    bp = c_all.shape[0]
    return pl.pallas_call(
        _ada_kernel,
        grid=(depth, n // tn),
        in_specs=[pl.BlockSpec((bp, d), lambda i, j: (0, 0)),
                  pl.BlockSpec((1, d, tn), lambda i, j: (i, 0, j)),
                  pl.BlockSpec((1, 1, tn), lambda i, j: (i, 0, j))],
        out_specs=pl.BlockSpec((1, bp, tn), lambda i, j: (i, 0, j)),
        out_shape=jax.ShapeDtypeStruct((depth, bp, n), F32),
        compiler_params=_cparams(("arbitrary", "arbitrary")),
        name="ada_mod",
    )(c_all, w_ada, b_ada.reshape(depth, 1, n))


def _mod_spec(rows_per_seq, tm, width, col_block):
    if rows_per_seq >= tm:
        npb = rows_per_seq // tm
        return pl.BlockSpec((1, 1, width), lambda *g: (g[0] // npb, 0, col_block(*g)))
    gb = tm // rows_per_seq
    return pl.BlockSpec((gb, 1, width), lambda *g: (g[0], 0, col_block(*g)))


def _rows(m_ref, tm):
    m = m_ref[...]
    gb, _, w = m.shape
    if gb == 1:
        return m[0]
    return jnp.broadcast_to(m, (gb, tm // gb, w)).reshape(tm, w)


def _modulated(x_ref, g_ref, sc_ref, sh_ref):
    x = x_ref[...]
    tm = x.shape[0]
    y = x * lax.rsqrt(jnp.mean(x * x, axis=-1, keepdims=True) + EPS) * g_ref[...]
    return y * (1.0 + _rows(sc_ref, tm)) + _rows(sh_ref, tm)


def _store_tiles(outs, y, transposed):
    yt = y.T if any(transposed) else None
    for o, tr in zip(outs, transposed):
        if tr:
            o[(0,) * (len(o.shape) - 2)] = yt.astype(o.dtype)
        else:
            o[...] = y.astype(o.dtype)


def _proj_kernel(x_ref, g_ref, sc_ref, sh_ref, w_ref, cs_ref, *rest, transposed, n_alias):
    outs, h_scr = rest[n_alias:-1], rest[-1]

    @pl.when(pl.program_id(1) == 0)
    def _():
        h_scr[...] = _modulated(x_ref, g_ref, sc_ref, sh_ref).astype(BF)

    acc = _dot(h_scr[...], w_ref[...].astype(BF)) * cs_ref[...]
    _store_tiles(outs, acc, transposed)


def _segnorm_kernel(x_ref, g_ref, sc_ref, sh_ref, w_ref, gv_ref, seg_ref, *rest, seg, transposed,
                    n_alias):
    outs, h_scr = rest[n_alias:-1], rest[-1]

    @pl.when(pl.program_id(1) == 0)
    def _():
        h_scr[...] = _modulated(x_ref, g_ref, sc_ref, sh_ref).astype(BF)

    acc = _dot(h_scr[...], w_ref[...].astype(BF))
    ssq = _dot((acc * acc).astype(BF), seg_ref[...])
    y = acc * lax.rsqrt(ssq * (1.0 / seg) + EPS) * gv_ref[...]
    _store_tiles(outs, y, transposed)


def norm_proj(x, rows_per_seq, mod, sc_chunk, sh_chunk, g, w, col_off, n, out_dtypes, tm, tn,
              col_scale=None, seg_gain=None, transposed=False, stacked=None, name="norm_proj"):
    t, d = x.shape
    prev = []
    if transposed is False:
        transposed = (False,) * len(out_dtypes)
    tn = _tile(math.gcd(n, col_off) if col_off else n, tn)
    x_spec = pl.BlockSpec((tm, d), lambda i, j: (i, 0))
    g_spec = pl.BlockSpec((1, d), lambda i, j: (0, 0))
    sc_spec = _mod_spec(rows_per_seq, tm, d, lambda i, j: sc_chunk)
    sh_spec = _mod_spec(rows_per_seq, tm, d, lambda i, j: sh_chunk)
    off = col_off // tn
    w_spec = pl.BlockSpec((d, tn), lambda i, j: (0, off + j))
    npb = max(rows_per_seq // tm, 1)
    out_specs, out_shape = [], []
    for dt, tr in zip(out_dtypes, transposed):
        if tr:
            out_specs.append(pl.BlockSpec((1, tn, tm), lambda i, j: (i // npb, j, i % npb)))
            out_shape.append(jax.ShapeDtypeStruct((t // rows_per_seq, n, rows_per_seq), dt))
        else:
            out_specs.append(pl.BlockSpec((tm, tn), lambda i, j: (i, j)))
            out_shape.append(jax.ShapeDtypeStruct((t, n), dt))
    if stacked is not None and transposed[0]:
        layer, n_layers, previous = stacked
        out_specs[0] = pl.BlockSpec((1, 1, tn, tm), lambda i, j: (layer, i // npb, j, i % npb))
        out_shape[0] = jax.ShapeDtypeStruct((n_layers,) + out_shape[0].shape, out_dtypes[0])
        prev = [] if previous is None else [previous]
    if seg_gain is None:
        if col_scale is None:
            col_scale = jnp.ones((1, n), F32)
        kern = functools.partial(_proj_kernel, transposed=transposed, n_alias=len(prev))
        extra = [col_scale]
        extra_specs = [pl.BlockSpec((1, tn), lambda i, j: (0, j))]
    else:
        seg = seg_gain.shape[0]
        kern = functools.partial(_segnorm_kernel, seg=seg, transposed=transposed, n_alias=len(prev))
        gv = jnp.tile(seg_gain.astype(F32), tn // seg).reshape(1, tn)
        ids = jnp.arange(tn) // seg
        seg_mat = (ids[:, None] == ids[None, :]).astype(BF)
        extra = [gv, seg_mat]
        extra_specs = [pl.BlockSpec((1, tn), lambda i, j: (0, 0)),
                       pl.BlockSpec((tn, tn), lambda i, j: (0, 0))]
    res = pl.pallas_call(
        kern,
        grid=(t // tm, n // tn),
        in_specs=[x_spec, g_spec, sc_spec, sh_spec, w_spec] + extra_specs
                 + [pl.BlockSpec(memory_space=pl.ANY)] * len(prev),
        out_specs=out_specs,
        out_shape=out_shape,
        scratch_shapes=[pltpu.VMEM((tm, d), BF)],
        input_output_aliases={5 + len(extra) + i: i for i in range(len(prev))},
        compiler_params=_cparams(("parallel", "arbitrary")),
        name=name,
    )(x, g.reshape(1, d), mod, mod, w, *extra, *prev)
    return res


def _router_kernel(x_ref, g_ref, sc_ref, sh_ref, w_ref, b_ref, tri_ref,
                   h_ref, o_ref, pos_ref, cnt_ref, *, n_experts):
    h = _modulated(x_ref, g_ref, sc_ref, sh_ref)
    h_ref[...] = h.astype(h_ref.dtype)
    logits = _dot(h, w_ref[...], precision=lax.Precision.HIGHEST) + b_ref[...]
    lane = lax.broadcasted_iota(jnp.int32, logits.shape, 1).astype(F32)
    big = float(LANES)
    l1 = jnp.where(lane < n_experts, logits, NEG_INF)
    m1 = jnp.max(l1, axis=-1, keepdims=True)
    i1 = jnp.min(jnp.where(l1 == m1, lane, big), axis=-1, keepdims=True)
    l2 = jnp.where(lane == i1, NEG_INF, l1)
    m2 = jnp.max(l2, axis=-1, keepdims=True)
    i2 = jnp.min(jnp.where(l2 == m2, lane, big), axis=-1, keepdims=True)
    e2 = jnp.exp(m2 - m1)
    w1 = 1.0 / (1.0 + e2)
    w2 = e2 / (1.0 + e2)
    gate = jnp.where(lane == i1, w1, 0.0) + jnp.where(lane == i2, w2, 0.0)
    o_ref[...] = gate
    routed = jnp.where(gate > 0.0, 1.0, 0.0)
    pos_ref[...] = _dot(tri_ref[...], routed.astype(BF))
    cnt_ref[0] = jnp.sum(routed, axis=0, keepdims=True)


def router_gates(x, rows_per_seq, mod, sc_chunk, sh_chunk, g, w_router, b_router, tm):
    t, d = x.shape
    e = w_router.shape[1]
    w_pad = jnp.pad(w_router, ((0, 0), (0, LANES - e)))
    b_pad = jnp.pad(b_router, (0, LANES - e)).reshape(1, LANES)
    idx = jnp.arange(tm)
    strict_lower = (idx[:, None] > idx[None, :]).astype(BF)
    tok = pl.BlockSpec((tm, LANES), lambda i: (i, 0))
    return pl.pallas_call(
        functools.partial(_router_kernel, n_experts=e),
        grid=(t // tm,),
        in_specs=[pl.BlockSpec((tm, d), lambda i: (i, 0)),
                  pl.BlockSpec((1, d), lambda i: (0, 0)),
                  _mod_spec(rows_per_seq, tm, d, lambda i: sc_chunk),
                  _mod_spec(rows_per_seq, tm, d, lambda i: sh_chunk),
                  pl.BlockSpec((d, LANES), lambda i: (0, 0)),
                  pl.BlockSpec((1, LANES), lambda i: (0, 0)),
                  pl.BlockSpec((tm, tm), lambda i: (0, 0))],
        out_specs=[pl.BlockSpec((tm, d), lambda i: (i, 0)), tok, tok,
                   pl.BlockSpec((1, 1, LANES), lambda i: (i, 0, 0))],
        out_shape=[jax.ShapeDtypeStruct((t, d), BF),
                   jax.ShapeDtypeStruct((t, LANES), F32),
                   jax.ShapeDtypeStruct((t, LANES), F32),
                   jax.ShapeDtypeStruct((t // tm, 1, LANES), F32)],
        compiler_params=_cparams(("parallel",)),
        name="router",
    )(x, g.reshape(1, d), mod, mod, w_pad, b_pad, strict_lower)


def _moe_kernel(cnt_ref, h_ref, dg_ref, pos_ref, wg_ref, wu_ref, wd_ref, x_ref, gt_ref, o_ref,
                xs_scr, y_scr, *, n_experts, br):
    ti, e, fc = pl.program_id(0), pl.program_id(1), pl.program_id(2)
    last_fc = pl.num_programs(2) - 1
    tm = h_ref.shape[0]

    @pl.when((e == 0) & (fc == 0))
    def _():
        o_ref[...] = jnp.zeros_like(o_ref)

    sel = lax.broadcasted_iota(jnp.int32, (tm, LANES), 1) == e
    ge = jnp.sum(jnp.where(sel, dg_ref[...], 0.0), axis=-1, keepdims=True)
    pos = jnp.sum(jnp.where(sel, pos_ref[...], 0.0), axis=-1, keepdims=True)
    routed = ge > 0.0
    n_blocks = (cnt_ref[ti * n_experts + e] + br - 1) // br

    def block(rb, carry):
        r0 = pl.multiple_of(rb * br, br)
        rows = pl.ds(r0, br)
        slot = (r0 + lax.broadcasted_iota(jnp.int32, (tm, br), 1)).astype(F32)
        onehot = jnp.where((pos == slot) & routed, 1.0, 0.0).astype(BF)

        @pl.when(fc == 0)
        def _():
            xs_scr[rows, :] = _tn_dot(onehot, h_ref[...]).astype(BF)

        xs = xs_scr[rows, :]
        a = _dot(xs, wg_ref[0, 0])
        u = _dot(xs, wu_ref[0, 0])
        y = _dot((a * jax.nn.sigmoid(a) * u).astype(BF), wd_ref[0, 0])

        @pl.when(fc == 0)
        def _():
            y_scr[rows, :] = y

        @pl.when(fc > 0)
        def _():
            y_scr[rows, :] += y

        @pl.when(fc == last_fc)
        def _():
            o_ref[...] += ge * _dot(onehot, y_scr[rows, :].astype(BF))
        return carry

    lax.fori_loop(0, n_blocks, block, 0)

    @pl.when((e == n_experts - 1) & (fc == last_fc))
    def _():
        o_ref[...] = x_ref[...] + _rows(gt_ref, tm) * o_ref[...]


def moe_ffn(h, dense_gate, rank, counts, w_gu, w_down, layer, x, rows_per_seq, mod, gt_chunk, tm,
            ff_chunks, block_rows):
    t, d = x.shape
    _, e, _, f2 = w_gu.shape
    f = f2 // 2
    fcw = f // ff_chunks
    tok = pl.BlockSpec((tm, LANES), lambda i, k, c, cnt: (i, 0))
    grid_spec = pltpu.PrefetchScalarGridSpec(
        num_scalar_prefetch=1,
        grid=(t // tm, e, ff_chunks),
        in_specs=[pl.BlockSpec((tm, d), lambda i, k, c, cnt: (i, 0)), tok, tok,
                  pl.BlockSpec((1, 1, d, fcw), lambda i, k, c, cnt: (layer, k, 0, c)),
                  pl.BlockSpec((1, 1, d, fcw), lambda i, k, c, cnt: (layer, k, 0, ff_chunks + c)),
                  pl.BlockSpec((1, 1, fcw, d), lambda i, k, c, cnt: (layer, k, c, 0)),
                  pl.BlockSpec((tm, d), lambda i, k, c, cnt: (i, 0)),
                  _mod_spec(rows_per_seq, tm, d, lambda i, k, c, cnt: gt_chunk)],
        out_specs=pl.BlockSpec((tm, d), lambda i, k, c, cnt: (i, 0)),
        scratch_shapes=[pltpu.VMEM((tm, d), BF), pltpu.VMEM((tm, d), F32)],
    )
    return pl.pallas_call(
        functools.partial(_moe_kernel, n_experts=e, br=block_rows),
        grid_spec=grid_spec,
        out_shape=jax.ShapeDtypeStruct((t, d), F32),
        compiler_params=_cparams(("parallel", "arbitrary", "arbitrary")),
        name="moe_ffn",
    )(counts, h, dense_gate, rank, w_gu, w_gu, w_down, x, mod)


def _swiglu_up_kernel(x_ref, g_ref, sc_ref, sh_ref, wg_ref, wu_ref, o_ref, h_scr):
    @pl.when(pl.program_id(1) == 0)
    def _():
        h_scr[...] = _modulated(x_ref, g_ref, sc_ref, sh_ref).astype(BF)

    h = h_scr[...]
    a = _dot(h, wg_ref[0].astype(BF))
    u = _dot(h, wu_ref[0].astype(BF))
    o_ref[...] = (a * jax.nn.sigmoid(a) * u).astype(o_ref.dtype)


def swiglu_up(x, rows_per_seq, mod, sc_chunk, sh_chunk, g, w_gu, layer, tm, tn):
    t, d = x.shape
    f = w_gu.shape[2] // 2
    tn = _tile(f, tn)
    nf = f // tn
    return pl.pallas_call(
        _swiglu_up_kernel,
        grid=(t // tm, nf),
        in_specs=[pl.BlockSpec((tm, d), lambda i, j: (i, 0)),
                  pl.BlockSpec((1, d), lambda i, j: (0, 0)),
                  _mod_spec(rows_per_seq, tm, d, lambda i, j: sc_chunk),
                  _mod_spec(rows_per_seq, tm, d, lambda i, j: sh_chunk),
                  pl.BlockSpec((1, d, tn), lambda i, j: (layer, 0, j)),
                  pl.BlockSpec((1, d, tn), lambda i, j: (layer, 0, nf + j))],
        out_specs=pl.BlockSpec((tm, tn), lambda i, j: (i, j)),
        out_shape=jax.ShapeDtypeStruct((t, f), BF),
        scratch_shapes=[pltpu.VMEM((tm, d), BF)],
        compiler_params=_cparams(("parallel", "arbitrary")),
        name="swiglu_up",
    )(x, g.reshape(1, d), mod, mod, w_gu, w_gu)


def _down_kernel(y_ref, w_ref, x_ref, gt_ref, o_ref):
    p = _dot(y_ref[...].astype(BF), w_ref[0].astype(BF))
    o_ref[...] = x_ref[...] + _rows(gt_ref, x_ref.shape[0]) * p


def down_residual(y, w, layer, x, rows_per_seq, mod, gt_chunk, tm, tn, name="down_residual"):
    t, k = y.shape
    d = x.shape[1]
    tn = _tile(d, tn)
    nd = d // tn
    return pl.pallas_call(
        _down_kernel,
        grid=(t // tm, nd),
        in_specs=[pl.BlockSpec((tm, k), lambda i, j: (i, 0)),
                  pl.BlockSpec((1, k, tn), lambda i, j: (layer, 0, j)),
                  pl.BlockSpec((tm, tn), lambda i, j: (i, j)),
                  _mod_spec(rows_per_seq, tm, tn, lambda i, j: gt_chunk * nd + j)],
        out_specs=pl.BlockSpec((tm, tn), lambda i, j: (i, j)),
        out_shape=jax.ShapeDtypeStruct((t, d), F32),
        compiler_params=_cparams(("parallel", "arbitrary")),
        name=name,
    )(y, w, x, mod)


def _log_sigmoid(x):
    return jnp.minimum(x, 0.0) - jnp.log1p(jnp.exp(-jnp.abs(x)))


def _mlstm_kernel(qkv_ref, o_ref, gt_ref, bg_ref, gh_ref, c0_ref, n0_ref, m0_ref, *rest,
                  heads, dk, dv, chunk, seqs, unroll):
    y_ref, c_all, n_all, m_all = rest[-4:]
    c_ref, n_ref, m_ref = c_all.at[0], n_all.at[0], m_all.at[0]
    L = chunk
    hi = lax.Precision.HIGHEST
    cd = BF if L >= 16 else F32

    @pl.when(pl.program_id(1) == 0)
    def _():
        c_ref[...] = c0_ref[...]
        n_ref[...] = n0_ref[...]
        m_ref[...] = m0_ref[...]

    row = lax.broadcasted_iota(jnp.int32, (L, L), 0)
    col = lax.broadcasted_iota(jnp.int32, (L, L), 1)
    causal = row >= col
    tril = causal.astype(F32)
    sel = (lax.broadcasted_iota(jnp.int32, (SUBLANES, LANES), 0)
           == lax.broadcasted_iota(jnp.int32, (SUBLANES, LANES), 1)).astype(F32)

    def one_seq(bi):
        gates = gt_ref[bi] + bg_ref[...]
        lf = _log_sigmoid(gates)
        bcum = _dot(tril, lf, precision=hi)
        g_rows = _nt_dot(sel, gates, precision=hi)
        b_rows = _nt_dot(sel, bcum, precision=hi)
        for h in range(heads):
            q = qkv_ref[bi, :, h * dk:(h + 1) * dk]
            k = qkv_ref[bi, :, (heads + h) * dk:(heads + h + 1) * dk]
            v = qkv_ref[bi, :, 2 * heads * dk + h * dv:2 * heads * dk + (h + 1) * dv]
            qf, kf = q.astype(F32), k.astype(F32)
            qc, kc, vc = q.astype(cd), k.astype(cd), v.astype(cd)
            ig_col = gates[:, h:h + 1]
            ig_row = g_rows[h:h + 1, :]
            b_col = bcum[:, heads + h:heads + h + 1]
            b_row = b_rows[heads + h:heads + h + 1, :]
            b_end = bcum[L - 1:L, heads + h:heads + h + 1]
            c_old = c_ref[bi, h]
            n_old = n_ref[bi, h]
            m_old = m_ref[bi, h][:, :1]

            dlog = jnp.where(causal, b_col - b_row + ig_row, NEG_INF)
            gcar = b_col + m_old
            m_t = jnp.maximum(gcar, jnp.max(dlog, axis=-1, keepdims=True))
            s = _nt_dot(qc, kc) * jnp.exp(dlog - m_t)
            dec = jnp.exp(gcar - m_t)
            num = _dot(s.astype(cd), vc) + dec * _dot(qc, c_old.astype(cd))
            den = (jnp.sum(s, axis=-1, keepdims=True)
                   + dec * jnp.sum(qf * n_old, axis=-1, keepdims=True))
            hh = num / jnp.maximum(jnp.abs(den), jnp.exp(-m_t))

            a_row = b_end - b_row + ig_row
            a_col = b_end - b_col + ig_col
            m_new = jnp.maximum(b_end + m_old, jnp.max(a_row, axis=-1, keepdims=True))
            wk = jnp.exp(a_col - m_new)
            decay = jnp.exp(b_end + m_old - m_new)
            kw = kf * wk
            c_ref[bi, h] = decay * c_old + _tn_dot(kw.astype(cd), vc)
            n_ref[bi, h] = decay * n_old + jnp.sum(kw, axis=0, keepdims=True)
            m_ref[bi, h] = jnp.broadcast_to(m_new, (1, LANES))

            hs = (hh * lax.rsqrt(jnp.mean(hh * hh, axis=-1, keepdims=True) + EPS)
                  * gh_ref[:, h * dv:(h + 1) * dv])
            og = o_ref[bi, :, h * dv:(h + 1) * dv]
            y_ref[bi, :, h * dv:(h + 1) * dv] = (hs * jax.nn.sigmoid(og)).astype(y_ref.dtype)

    if unroll:
        for bi in range(seqs):
            one_seq(bi)
    else:
        def body(bi, carry):
            one_seq(bi)
            return carry
        lax.fori_loop(0, seqs, body, 0)


def mlstm_cell(qkv, og, gates, b_gate, g_head, c0, n0, m0, seq_len, seqs_per_step, layer, stacked):
    t = qkv.shape[0]
    b, heads, dk, dv = c0.shape
    chunk = M_CHUNK if seq_len % M_CHUNK == 0 else seq_len
    nc = seq_len // chunk
    bt = seqs_per_step
    n4 = n0.reshape(b, heads, 1, dk)
    m4 = jnp.broadcast_to(m0.reshape(b, heads, 1, 1), (b, heads, 1, LANES))
    bg = jnp.pad(b_gate.astype(F32), (0, LANES - 2 * heads)).reshape(1, LANES)
    y_dtype = BF if chunk >= 16 else F32
    tok = lambda i, c: (i, c, 0)
    st = lambda i, c: (i, 0, 0, 0)
    st_out = lambda i, c: (layer, i, 0, 0, 0)
    first = isinstance(stacked, int)
    n_layers = stacked if first else stacked[0].shape[0]
    prev = [] if first else list(stacked)
    n_in = 8
    y, c_all, n_all, m_all = pl.pallas_call(
        functools.partial(_mlstm_kernel, heads=heads, dk=dk, dv=dv, chunk=chunk, seqs=bt,
                          unroll=bt <= 2),
        grid=(b // bt, nc),
        in_specs=[pl.BlockSpec((bt, chunk, qkv.shape[1]), tok),
                  pl.BlockSpec((bt, chunk, heads * dv), tok),
                  pl.BlockSpec((bt, chunk, LANES), tok),
                  pl.BlockSpec((1, LANES), lambda i, c: (0, 0)),
                  pl.BlockSpec((1, heads * dv), lambda i, c: (0, 0)),
                  pl.BlockSpec((bt, heads, dk, dv), st),
                  pl.BlockSpec((bt, heads, 1, dk), st),
                  pl.BlockSpec((bt, heads, 1, LANES), st)]
                 + [pl.BlockSpec(memory_space=pl.ANY)] * len(prev),
        out_specs=[pl.BlockSpec((bt, chunk, heads * dv), tok),
                   pl.BlockSpec((1, bt, heads, dk, dv), st_out),
                   pl.BlockSpec((1, bt, heads, 1, dk), st_out),
                   pl.BlockSpec((1, bt, heads, 1, LANES), st_out)],
        out_shape=[jax.ShapeDtypeStruct((b, seq_len, heads * dv), y_dtype),
                   jax.ShapeDtypeStruct((n_layers, b, heads, dk, dv), F32),
                   jax.ShapeDtypeStruct((n_layers, b, heads, 1, dk), F32),
                   jax.ShapeDtypeStruct((n_layers, b, heads, 1, LANES), F32)],
        input_output_aliases={n_in + i: 1 + i for i in range(len(prev))},
        compiler_params=_cparams(("parallel", "arbitrary")),
        name="mlstm_cell",
    )(qkv.reshape(b, seq_len, -1), og.reshape(b, seq_len, -1), gates.reshape(b, seq_len, LANES),
      bg, g_head.reshape(1, heads * dv).astype(F32), c0, n4, m4, *prev)
    return y.reshape(t, heads * dv), (c_all, n_all, m_all)


def _lambda(lam_ref, lam_init):
    lp = lam_ref[...]
    a = jnp.sum(lp[0:1] * lp[1:2], axis=-1, keepdims=True)
    b = jnp.sum(lp[2:3] * lp[3:4], axis=-1, keepdims=True)
    return jnp.exp(a) - jnp.exp(b) + lam_init


def _sub_norm(o, gsub_ref, lam_init):
    y = o * lax.rsqrt(jnp.mean(o * o, axis=-1, keepdims=True) + EPS) * gsub_ref[...]
    return y * (1.0 - lam_init)


def _flash_kernel(qi_ref, ki_ref, qt_ref, k_ref, vt_ref, bound_ref, lam_ref, gs_ref, o_ref,
                  qm_scr, m_scr, acc_scr, *, dh, tq, tk, lam_init, online):
    pair = pl.program_id(2)
    qi = qi_ref[pair]
    ki = ki_ref[pair]
    dv = LANES
    ext = 2 * SUBLANES

    @pl.when(ki == 0)
    def _():
        q = qt_ref[0] * (dh ** -0.5)
        row = lax.broadcasted_iota(jnp.int32, q.shape, 0)
        qm_scr[0] = jnp.where(row < dh, q, jnp.zeros_like(q))
        qm_scr[1] = jnp.where(row >= dh, q, jnp.zeros_like(q))
        m_scr[...] = jnp.full_like(m_scr, NEG_INF)
        acc_scr[...] = jnp.zeros_like(acc_scr)

    def absorb(masked):
        k = k_ref[...]
        ones_row = (lax.broadcasted_iota(jnp.int32, (ext, tk), 0) == 0).astype(BF)
        vx = jnp.concatenate([vt_ref[0], ones_row], axis=0)
        if masked:
            key = ki * tk + lax.broadcasted_iota(jnp.int32, (tk, tq), 0)
            qry = qi * tq + lax.broadcasted_iota(jnp.int32, (tk, tq), 1)
            keep = qry >= key
        for c in range(2):
            s = _dot(k, qm_scr[c])
            if masked:
                s = jnp.where(keep, s, NEG_INF)
            if online:
                m_prev = m_scr[c]
                m_new = jnp.maximum(m_prev, jnp.max(s, axis=0, keepdims=True))
                p = jnp.exp(s - m_new)
                acc_scr[c] = jnp.exp(m_prev - m_new) * acc_scr[c] + _dot(vx, p.astype(BF))
                m_scr[c] = m_new
            else:
                p = jnp.exp(s - bound_ref[...])
                acc_scr[c] += _dot(vx, p.astype(BF))

    needs_mask = (ki + 1) * tk - 1 > qi * tq

    @pl.when(needs_mask)
    def _():
        absorb(True)

    @pl.when(jnp.logical_not(needs_mask))
    def _():
        absorb(False)

    @pl.when(ki == (qi * tq + tq - 1) // tk)
    def _():
        lam = _lambda(lam_ref, lam_init)
        a0 = acc_scr[0]
        a1 = acc_scr[1]
        o = a0[:dv] / a0[dv:dv + 1] - lam * (a1[:dv] / a1[dv:dv + 1])
        y = o * lax.rsqrt(jnp.mean(o * o, axis=0, keepdims=True) + EPS) * gs_ref[...]
        o_ref[...] = (y * (1.0 - lam_init)).T.astype(o_ref.dtype)


SAFE_SCORE_BOUND = 30.0


def flash_diff_attention(qt, k, vt, score_bound, lam_p, g_sub, heads, lam_init, tq, tk):
    t = k.shape[0]
    n_seq, _, seq = qt.shape
    dv = vt.shape[1] // heads
    dh = k.shape[1] // (2 * heads)
    assert 2 * dh == LANES and dv == LANES
    nq, nk = seq // tq, seq // tk
    pairs = [(i, j) for i in range(nq) for j in range((i * tq + tq - 1) // tk + 1)]
    qi_tab = jnp.asarray([p[0] for p in pairs], jnp.int32)
    ki_tab = jnp.asarray([p[1] for p in pairs], jnp.int32)

    def call(online):
        grid_spec = pltpu.PrefetchScalarGridSpec(
            num_scalar_prefetch=2,
            grid=(n_seq, heads, len(pairs)),
            in_specs=[pl.BlockSpec((1, LANES, tq), lambda b, h, p, qi, ki: (b, h, qi[p])),
                      pl.BlockSpec((tk, LANES), lambda b, h, p, qi, ki: (b * nk + ki[p], h)),
                      pl.BlockSpec((1, LANES, tk), lambda b, h, p, qi, ki: (b, h, ki[p])),
                      pl.BlockSpec((1, 1), lambda b, h, p, qi, ki: (0, 0)),
                      pl.BlockSpec(lam_p.shape, lambda b, h, p, qi, ki: (0, 0)),
                      pl.BlockSpec((dv, 1), lambda b, h, p, qi, ki: (0, 0))],
            out_specs=pl.BlockSpec((tq, LANES), lambda b, h, p, qi, ki: (b * nq + qi[p], h)),
            scratch_shapes=[pltpu.VMEM((2, LANES, tq), BF),
                            pltpu.VMEM((2, 1, tq), F32),
                            pltpu.VMEM((2, dv + 2 * SUBLANES, tq), F32)],
        )
        return pl.pallas_call(
            functools.partial(_flash_kernel, dh=dh, tq=tq, tk=tk, lam_init=lam_init, online=online),
            grid_spec=grid_spec,
            out_shape=jax.ShapeDtypeStruct((t, heads * dv), BF),
            compiler_params=_cparams(("parallel", "parallel", "arbitrary")),
            name="flash_online" if online else "flash_bounded",
        )(qi_tab, ki_tab, qt, k, vt, score_bound.reshape(1, 1).astype(F32), lam_p.astype(F32),
          g_sub.reshape(dv, 1).astype(F32))

    return lax.cond(score_bound < SAFE_SCORE_BOUND, lambda: call(False), lambda: call(True))


def _paged_kernel(pt_ref, q_ref, kn_ref, vn_ref, lam_ref, gs_ref, *rest,
                  heads, dh, n_new, pages, page, lam_init):
    del pt_ref
    kt_refs = rest[:pages]
    v_refs = rest[pages:2 * pages]
    o_ref, s_scr = rest[2 * pages:]
    width = heads * 2 * dh
    nrow = 2 * heads * n_new
    past = pages * page

    q = q_ref[...] * (dh ** -0.5)
    qt = jnp.concatenate([q] * (2 * heads), axis=0)
    r = lax.broadcasted_iota(jnp.int32, (nrow, width), 0)
    cblk = lax.broadcasted_iota(jnp.int32, (nrow, width), 1) // dh
    qbd = jnp.where(cblk == r // n_new, qt, 0.0).astype(BF)

    pad = jnp.zeros((page - n_new, width), F32)
    kn = jnp.concatenate([kn_ref[...], pad], axis=0).astype(BF)
    vn = jnp.concatenate([vn_ref[...], pad], axis=0).astype(BF)

    def softmax_part(page_ids, with_new):
        c0 = page_ids[0] * page
        c1 = (page_ids[-1] + 1) * page
        for p in page_ids:
            s_scr[:, p * page:(p + 1) * page] = _dot(qbd, kt_refs[p][0, 0].astype(BF))
        if with_new:
            s_new = _nt_dot(qbd, kn)
            tok = lax.broadcasted_iota(jnp.int32, s_new.shape, 0) % n_new
            key = lax.broadcasted_iota(jnp.int32, s_new.shape, 1)
            s_scr[:, past:] = jnp.where(key <= tok, s_new, NEG_INF)
            c1 = past + page
        s = s_scr[:, c0:c1]
        m = jnp.max(s, axis=-1, keepdims=True)
        pr = jnp.exp(s - m)
        l = jnp.sum(pr, axis=-1, keepdims=True)
        s_scr[:, c0:c1] = pr
        acc = _dot(s_scr[:, past:].astype(BF), vn) if with_new else jnp.zeros((nrow, width), F32)
        for p in page_ids:
            v_wide = jnp.concatenate(
                [v_refs[p][0, 0, pl.ds(h, page, stride=heads), :] for h in range(heads)], axis=1)
            acc += _dot(s_scr[:, p * page:(p + 1) * page].astype(BF), v_wide.astype(BF))
        return acc, l, m

    half = max(pages // 2, 1)
    acc, l, m = softmax_part(list(range(half, pages)), True) if half < pages else (None, None, None)
    acc_a, l_a, m_a = softmax_part(list(range(half)), half == pages)
    if acc is None:
        acc, l = acc_a, l_a
    else:
        m_all = jnp.maximum(m, m_a)
        w, w_a = jnp.exp(m - m_all), jnp.exp(m_a - m_all)
        acc = w * acc + w_a * acc_a
        l = w * l + w_a * l_a

    lam = _lambda(lam_ref, lam_init)
    acc = acc * (1.0 / l)
    for h in range(heads):
        r0 = h * 2 * n_new
        cols = slice(h * 2 * dh, (h + 1) * 2 * dh)
        o = acc[r0:r0 + n_new, cols] - lam * acc[r0 + n_new:r0 + 2 * n_new, cols]
        o_ref[:, cols] = _sub_norm(o, gs_ref, lam_init)


def paged_diff_attention(q, k_new, v_new, cache_kt, cache_v, layer, page_table, lam_p, g_sub,
                         heads, lam_init):
    t, width = q.shape
    b, n_pages = page_table.shape
    n_new = t // b
    dh = width // (2 * heads)
    page = cache_kt.shape[3]
    assert n_new == SUBLANES and 2 * dh == LANES and page == LANES
    nrow = 2 * heads * n_new
    tok = pl.BlockSpec((n_new, width), lambda i, pt: (i, 0))

    def page_spec(p, shape):
        return pl.BlockSpec((1, 1) + shape, lambda i, pt: (layer, pt[i, p], 0, 0))

    grid_spec = pltpu.PrefetchScalarGridSpec(
        num_scalar_prefetch=1,
        grid=(b,),
        in_specs=[tok, tok, tok,
                  pl.BlockSpec(lam_p.shape, lambda i, pt: (0, 0)),
                  pl.BlockSpec((1, 2 * dh), lambda i, pt: (0, 0))]
                 + [page_spec(p, (width, page)) for p in range(n_pages)]
                 + [page_spec(p, (page * heads, 2 * dh)) for p in range(n_pages)],
        out_specs=tok,
        scratch_shapes=[pltpu.VMEM((nrow, (n_pages + 1) * page), F32)],
    )
    return pl.pallas_call(
        functools.partial(_paged_kernel, heads=heads, dh=dh, n_new=n_new, pages=n_pages,
                          page=page, lam_init=lam_init),
        grid_spec=grid_spec,
        out_shape=jax.ShapeDtypeStruct((t, width), F32),
        compiler_params=_cparams(("parallel",)),
        name="paged_diff_attention",
    )(page_table, q, k_new, v_new, lam_p.astype(F32), g_sub.reshape(1, 2 * dh).astype(F32),
      *([cache_kt] * n_pages), *([cache_v] * n_pages))


def _trunk(x, rows_per_seq, mods, state, paged, weights, cfg):
    (g_norm_mix, g_norm_ffn, w_in_m, b_gate_m, g_head_m, w_out_m,
     w_in_d, g_q_d, g_k_d, lam_d, g_sub_d, w_out_d, w_gu_f, w_down_f,
     w_router, b_router, w_gu_e, w_down_e) = weights
    t, d = x.shape
    n_seq = t // rows_per_seq
    depth = g_norm_mix.shape[0]
    tm = cfg["tm"]
    tn_proj, tn_seg, tn_up = 1024, 256, 1408
    heads_m = b_gate_m.shape[1] // 2
    dk, dv = state[0].shape[3], state[0].shape[4]
    qk_w = heads_m * dk
    heads_a, dh = cfg["heads_a"], cfg["dh"]
    new_state, new_kv = state[0].shape[0], []
    kt_stack = None
    for i in range(depth):
        j = i // 2
        mod = mods[i]
        if i % 2 == 0:
            col_scale = jnp.concatenate([jnp.ones((qk_w,), F32), jnp.full((qk_w,), dk ** -0.5, F32),
                                         jnp.ones((heads_m * dv,), F32)]).reshape(1, -1)
            small = rows_per_seq % M_CHUNK != 0
            (qkv,) = norm_proj(x, rows_per_seq, mod, 1, 0, g_norm_mix[i], w_in_m[j], 0,
                               2 * qk_w + heads_m * dv, [F32 if small else BF], tm, tn_proj,
                               col_scale=col_scale, name="mlstm_qkv")
            (og,) = norm_proj(x, rows_per_seq, mod, 1, 0, g_norm_mix[i], w_in_m[j],
                              2 * qk_w + heads_m * dv, heads_m * dv, [F32], tm, tn_proj, name="mlstm_o")
            w_gate = jnp.pad(w_in_m[j][:, 2 * qk_w + 2 * heads_m * dv:], ((0, 0), (0, LANES - 2 * heads_m)))
            (gates,) = norm_proj(x, rows_per_seq, mod, 1, 0, g_norm_mix[i], w_gate, 0, LANES,
                                 [F32], tm, LANES, name="mlstm_gates")
            c0, n0, m0 = state[0][j], state[1][j], state[2][j]
            y, new_state = mlstm_cell(qkv, og, gates, b_gate_m[j], g_head_m[j], c0, n0, m0,
                                      rows_per_seq, cfg["mlstm_seqs"], j, new_state)
            w_out = w_out_m
        else:
            lam_init = 0.8 - 0.6 * math.exp(-0.3 * i)
            prompt = paged is None
            (q,) = norm_proj(x, rows_per_seq, mod, 1, 0, g_norm_mix[i], w_in_d[j], 0, d,
                             [BF if prompt else F32], tm, tn_seg, seg_gain=g_q_d[j],
                             transposed=(prompt,), name="attn_q")
            k32, kbf = norm_proj(x, rows_per_seq, mod, 1, 0, g_norm_mix[i], w_in_d[j], d, d,
                                 [F32, BF], tm, tn_seg, seg_gain=g_k_d[j], transposed=(prompt, False),
                                 stacked=(j, depth // 2, kt_stack), name="attn_k")
            v32, vbf = norm_proj(x, rows_per_seq, mod, 1, 0, g_norm_mix[i], w_in_d[j], 2 * d, d,
                                 [F32, BF], tm, tn_proj, transposed=(False, prompt), name="attn_v")
            if paged is None:
                bound = (dh ** 0.5) * jnp.max(jnp.abs(g_q_d[j])) * jnp.max(jnp.abs(g_k_d[j]))
                y = flash_diff_attention(q, kbf, vbf, bound, lam_d[j], g_sub_d[j], heads_a,
                                         lam_init, cfg["tq"], cfg["tk"])
                kt_stack = k32
            else:
                cache_kt, cache_v, page_table = paged
                y = paged_diff_attention(q, k32, v32, cache_kt, cache_v, j, page_table, lam_d[j],
                                         g_sub_d[j], heads_a, lam_init)
            new_kv.append((k32, v32))
            w_out = w_out_d
        x = down_residual(y, w_out, j, x, rows_per_seq, mod, 2, tm, tn_proj, name="mixer_out")
        if i % 2 == 0:
            act = swiglu_up(x, rows_per_seq, mod, 4, 3, g_norm_ffn[i], w_gu_f, j, tm, tn_up)
            x = down_residual(act, w_down_f, j, x, rows_per_seq, mod, 5, tm, tn_proj, name="ffn_down")
        else:
            n_exp = w_router.shape[2]
            h, dg, rank, counts = router_gates(x, rows_per_seq, mod, 4, 3, g_norm_ffn[i],
                                               w_router[j], b_router[j], tm)
            counts = counts[:, 0, :n_exp].astype(jnp.int32).reshape(-1)
            x = moe_ffn(h, dg, rank, counts, w_gu_e, w_down_e, j, x, rows_per_seq, mod, 5, tm,
                        cfg["moe_ff_chunks"], min(cfg["moe_block_rows"], tm))
    return x, new_state, new_kv, kt_stack


def kernel(x_prompt, x_sample, c_prompt, c_sample, state_C, state_n, state_m, cache_k, cache_v, page_table, w_ada, b_ada, g_norm_mix, g_norm_ffn, w_in_m, b_gate_m, g_head_m, w_out_m, w_in_d, g_q_d, g_k_d, lam_d, g_sub_d, w_out_d, w_gu_f, w_down_f, w_router, b_router, w_gu_e, w_down_e):
    bp, seq, d = x_prompt.shape
    bs, dec_seq, _ = x_sample.shape
    depth = w_ada.shape[0]
    n_ml, _, heads_m, dk, dv = state_C.shape
    n_diff, n_pool, page, heads_a, _, dh = cache_k.shape
    weights = (g_norm_mix, g_norm_ffn, w_in_m.astype(BF), b_gate_m, g_head_m, w_out_m.astype(BF),
               w_in_d.astype(BF), g_q_d, g_k_d, lam_d, g_sub_d, w_out_d.astype(BF),
               w_gu_f.astype(BF), w_down_f.astype(BF),
               w_router, b_router, w_gu_e.astype(BF), w_down_e.astype(BF))

    n_c = bp + bs
    n_c_pad = -(-n_c // SUBLANES) * SUBLANES
    c_all = jnp.pad(jnp.concatenate([c_prompt, c_sample], axis=0), ((0, n_c_pad - n_c), (0, 0)))
    mod_all = ada_mod(c_all, w_ada, b_ada, tn=min(6 * d, 1536))
    mods_p = [mod_all[i, :bp].reshape(bp, 1, 6 * d) for i in range(depth)]
    mods_s = [mod_all[i, bp:n_c].reshape(bs, 1, 6 * d) for i in range(depth)]

    moe = dict(moe_ff_chunks=2, moe_block_rows=128)
    cfg_p = dict(tm=min(1024, seq), tq=min(1024, seq), tk=min(1024, seq), heads_a=heads_a, dh=dh,
                 mlstm_seqs=min(2, bp), **moe)
    cfg_s = dict(tm=min(1024, bs * dec_seq), heads_a=heads_a, dh=dh, mlstm_seqs=min(8, bs), **moe)

    zeros = (jnp.zeros((n_ml, bp, heads_m, dk, dv), F32), jnp.zeros((n_ml, bp, heads_m, dk), F32),
             jnp.zeros((n_ml, bp, heads_m), F32))
    y_p, st_p, kv_p, kt_p = _trunk(x_prompt.reshape(bp * seq, d), seq, mods_p, zeros, None, weights, cfg_p)
    paged = (jnp.transpose(cache_k, (0, 1, 3, 4, 5, 2)).reshape(n_diff, n_pool, heads_a * 2 * dh, page),
             cache_v.reshape(n_diff, n_pool, page * heads_a, 2 * dh), page_table)
    y_s, st_s, kv_s, _ = _trunk(x_sample.reshape(bs * dec_seq, d), dec_seq, mods_s,
                             (state_C, state_n, state_m), paged, weights, cfg_s)

    def stack(items, idx, shape):
        return jnp.stack([it[idx] for it in items]).reshape(shape)

    return (y_p.reshape(bp, seq, d), y_s.reshape(bs, dec_seq, d),
            st_p[0], st_p[1].reshape(n_ml, bp, heads_m, dk), st_p[2][:, :, :, 0, 0],
            st_s[0], st_s[1].reshape(n_ml, bs, heads_m, dk), st_s[2][:, :, :, 0, 0],
            jnp.transpose(kt_p.reshape(n_diff, bp, heads_a, 2, dh, seq), (0, 1, 5, 2, 3, 4)),
            stack(kv_p, 1, (n_diff, bp, seq, heads_a, 2 * dh)),
            stack(kv_s, 0, (n_diff, bs, dec_seq, heads_a, 2, dh)),
            stack(kv_s, 1, (n_diff, bs, dec_seq, heads_a, 2 * dh)))
```

```python
import functools
import math

import jax
import jax.numpy as jnp
from jax import lax
from jax.experimental import pallas as pl
from jax.experimental.pallas import tpu as pltpu

BF = jnp.bfloat16
F32 = jnp.float32
EPS = 1e-6
M_CHUNK = 128
LANES = 128
SUBLANES = 8
VMEM_LIMIT = 56 * 1024 * 1024
NEG_INF = float("-inf")


def _cparams(sem):
    return pltpu.CompilerParams(dimension_semantics=sem, vmem_limit_bytes=VMEM_LIMIT)


def _tile(n, pref):
    if n <= pref:
        return n
    t = pref - pref % LANES
    while n % t:
        t -= LANES
    return t


def _nt_dot(a, b, **kw):
    return lax.dot_general(a, b, (((1,), (1,)), ((), ())), preferred_element_type=F32, **kw)


def _tn_dot(a, b, **kw):
    return lax.dot_general(a, b, (((0,), (0,)), ((), ())), preferred_element_type=F32, **kw)


def _dot(a, b, **kw):
    return jnp.dot(a, b, preferred_element_type=F32, **kw)


def _ada_kernel(c_ref, w_ref, b_ref, o_ref):
    c = c_ref[...]
    a = (c * jax.nn.sigmoid(c)).astype(BF)
    o_ref[0] = _dot(a, w_ref[0].astype(BF)) + b_ref[0]


def ada_mod(c_all, w_ada, b_ada, tn):
    depth, d, n = w_ada.shape
    bp = c_all.shape[0]
    return pl.pallas_call(
        _ada_kernel,
        grid=(depth, n // tn),
        in_specs=[pl.BlockSpec((bp, d), lambda i, j: (0, 0)),
                  pl.BlockSpec((1, d, tn), lambda i, j: (i, 0, j)),
                  pl.BlockSpec((1, 1, tn), lambda i, j: (i, 0, j))],
        out_specs=pl.BlockSpec((1, bp, tn), lambda i, j: (i, 0, j)),
        out_shape=jax.ShapeDtypeStruct((depth, bp, n), F32),
        compiler_params=_cparams(("arbitrary", "arbitrary")),
        name="ada_mod",
    )(c_all, w_ada, b_ada.reshape(depth, 1, n))


def _mod_spec(rows_per_seq, tm, width, col_block):
    if rows_per_seq >= tm:
        npb = rows_per_seq // tm
        return pl.BlockSpec((1, 1, width), lambda *g: (g[0] // npb, 0, col_block(*g)))
    gb = tm // rows_per_seq
    return pl.BlockSpec((gb, 1, width), lambda *g: (g[0], 0, col_block(*g)))


def _rows(m_ref, tm):
    m = m_ref[...]
    gb, _, w = m.shape
    if gb == 1:
        return m[0]
    return jnp.broadcast_to(m, (gb, tm // gb, w)).reshape(tm, w)


def _modulated(x_ref, g_ref, sc_ref, sh_ref):
    x = x_ref[...]
    tm = x.shape[0]
    y = x * lax.rsqrt(jnp.mean(x * x, axis=-1, keepdims=True) + EPS) * g_ref[...]
    return y * (1.0 + _rows(sc_ref, tm)) + _rows(sh_ref, tm)


def _store_tiles(outs, y, transposed):
    yt = y.T if any(transposed) else None
    for o, tr in zip(outs, transposed):
        if tr:
            o[(0,) * (len(o.shape) - 2)] = yt.astype(o.dtype)
        else:
            o[...] = y.astype(o.dtype)


def _proj_kernel(x_ref, g_ref, sc_ref, sh_ref, w_ref, cs_ref, *rest, transposed, n_alias):
    outs, h_scr = rest[n_alias:-1], rest[-1]

    @pl.when(pl.program_id(1) == 0)
    def _():
        h_scr[...] = _modulated(x_ref, g_ref, sc_ref, sh_ref).astype(BF)

    acc = _dot(h_scr[...], w_ref[...].astype(BF)) * cs_ref[...]
    _store_tiles(outs, acc, transposed)


def _segnorm_kernel(x_ref, g_ref, sc_ref, sh_ref, w_ref, gv_ref, seg_ref, *rest, seg, transposed,
                    n_alias):
    outs, h_scr = rest[n_alias:-1], rest[-1]

    @pl.when(pl.program_id(1) == 0)
    def _():
        h_scr[...] = _modulated(x_ref, g_ref, sc_ref, sh_ref).astype(BF)

    acc = _dot(h_scr[...], w_ref[...].astype(BF))
    ssq = _dot((acc * acc).astype(BF), seg_ref[...])
    y = acc * lax.rsqrt(ssq * (1.0 / seg) + EPS) * gv_ref[...]
    _store_tiles(outs, y, transposed)


def norm_proj(x, rows_per_seq, mod, sc_chunk, sh_chunk, g, w, col_off, n, out_dtypes, tm, tn,
              col_scale=None, seg_gain=None, transposed=False, stacked=None, name="norm_proj"):
    t, d = x.shape
    prev = []
    if transposed is False:
        transposed = (False,) * len(out_dtypes)
    tn = _tile(math.gcd(n, col_off) if col_off else n, tn)
    x_spec = pl.BlockSpec((tm, d), lambda i, j: (i, 0))
    g_spec = pl.BlockSpec((1, d), lambda i, j: (0, 0))
    sc_spec = _mod_spec(rows_per_seq, tm, d, lambda i, j: sc_chunk)
    sh_spec = _mod_spec(rows_per_seq, tm, d, lambda i, j: sh_chunk)
    off = col_off // tn
    w_spec = pl.BlockSpec((d, tn), lambda i, j: (0, off + j))
    npb = max(rows_per_seq // tm, 1)
    out_specs, out_shape = [], []
    for dt, tr in zip(out_dtypes, transposed):
        if tr:
            out_specs.append(pl.BlockSpec((1, tn, tm), lambda i, j: (i // npb, j, i % npb)))
            out_shape.append(jax.ShapeDtypeStruct((t // rows_per_seq, n, rows_per_seq), dt))
        else:
            out_specs.append(pl.BlockSpec((tm, tn), lambda i, j: (i, j)))
            out_shape.append(jax.ShapeDtypeStruct((t, n), dt))
    if stacked is not None and transposed[0]:
        layer, n_layers, previous = stacked
        out_specs[0] = pl.BlockSpec((1, 1, tn, tm), lambda i, j: (layer, i // npb, j, i % npb))
        out_shape[0] = jax.ShapeDtypeStruct((n_layers,) + out_shape[0].shape, out_dtypes[0])
        prev = [] if previous is None else [previous]
    if seg_gain is None:
        if col_scale is None:
            col_scale = jnp.ones((1, n), F32)
        kern = functools.partial(_proj_kernel, transposed=transposed, n_alias=len(prev))
        extra = [col_scale]
        extra_specs = [pl.BlockSpec((1, tn), lambda i, j: (0, j))]
    else:
        seg = seg_gain.shape[0]
        kern = functools.partial(_segnorm_kernel, seg=seg, transposed=transposed, n_alias=len(prev))
        gv = jnp.tile(seg_gain.astype(F32), tn // seg).reshape(1, tn)
        ids = jnp.arange(tn) // seg
        seg_mat = (ids[:, None] == ids[None, :]).astype(BF)
        extra = [gv, seg_mat]
        extra_specs = [pl.BlockSpec((1, tn), lambda i, j: (0, 0)),
                       pl.BlockSpec((tn, tn), lambda i, j: (0, 0))]
    res = pl.pallas_call(
        kern,
        grid=(t // tm, n // tn),
        in_specs=[x_spec, g_spec, sc_spec, sh_spec, w_spec] + extra_specs
                 + [pl.BlockSpec(memory_space=pl.ANY)] * len(prev),
        out_specs=out_specs,
        out_shape=out_shape,
        scratch_shapes=[pltpu.VMEM((tm, d), BF)],
        input_output_aliases={5 + len(extra) + i: i for i in range(len(prev))},
        compiler_params=_cparams(("parallel", "arbitrary")),
        name=name,
    )(x, g.reshape(1, d), mod, mod, w, *extra, *prev)
    return res


def _router_kernel(x_ref, g_ref, sc_ref, sh_ref, w_ref, b_ref, tri_ref,
                   h_ref, o_ref, pos_ref, cnt_ref, *, n_experts):
    h = _modulated(x_ref, g_ref, sc_ref, sh_ref)
    h_ref[...] = h.astype(h_ref.dtype)
    logits = _dot(h, w_ref[...], precision=lax.Precision.HIGHEST) + b_ref[...]
    lane = lax.broadcasted_iota(jnp.int32, logits.shape, 1).astype(F32)
    big = float(LANES)
    l1 = jnp.where(lane < n_experts, logits, NEG_INF)
    m1 = jnp.max(l1, axis=-1, keepdims=True)
    i1 = jnp.min(jnp.where(l1 == m1, lane, big), axis=-1, keepdims=True)
    l2 = jnp.where(lane == i1, NEG_INF, l1)
    m2 = jnp.max(l2, axis=-1, keepdims=True)
    i2 = jnp.min(jnp.where(l2 == m2, lane, big), axis=-1, keepdims=True)
    e2 = jnp.exp(m2 - m1)
    w1 = 1.0 / (1.0 + e2)
    w2 = e2 / (1.0 + e2)
    gate = jnp.where(lane == i1, w1, 0.0) + jnp.where(lane == i2, w2, 0.0)
    o_ref[...] = gate
    routed = jnp.where(gate > 0.0, 1.0, 0.0)
    pos_ref[...] = _dot(tri_ref[...], routed.astype(BF))
    cnt_ref[0] = jnp.sum(routed, axis=0, keepdims=True)


def router_gates(x, rows_per_seq, mod, sc_chunk, sh_chunk, g, w_router, b_router, tm):
    t, d = x.shape
    e = w_router.shape[1]
    w_pad = jnp.pad(w_router, ((0, 0), (0, LANES - e)))
    b_pad = jnp.pad(b_router, (0, LANES - e)).reshape(1, LANES)
    idx = jnp.arange(tm)
    strict_lower = (idx[:, None] > idx[None, :]).astype(BF)
    tok = pl.BlockSpec((tm, LANES), lambda i: (i, 0))
    return pl.pallas_call(
        functools.partial(_router_kernel, n_experts=e),
        grid=(t // tm,),
        in_specs=[pl.BlockSpec((tm, d), lambda i: (i, 0)),
                  pl.BlockSpec((1, d), lambda i: (0, 0)),
                  _mod_spec(rows_per_seq, tm, d, lambda i: sc_chunk),
                  _mod_spec(rows_per_seq, tm, d, lambda i: sh_chunk),
                  pl.BlockSpec((d, LANES), lambda i: (0, 0)),
                  pl.BlockSpec((1, LANES), lambda i: (0, 0)),
                  pl.BlockSpec((tm, tm), lambda i: (0, 0))],
        out_specs=[pl.BlockSpec((tm, d), lambda i: (i, 0)), tok, tok,
                   pl.BlockSpec((1, 1, LANES), lambda i: (i, 0, 0))],
        out_shape=[jax.ShapeDtypeStruct((t, d), BF),
                   jax.ShapeDtypeStruct((t, LANES), F32),
                   jax.ShapeDtypeStruct((t, LANES), F32),
                   jax.ShapeDtypeStruct((t // tm, 1, LANES), F32)],
        compiler_params=_cparams(("parallel",)),
        name="router",
    )(x, g.reshape(1, d), mod, mod, w_pad, b_pad, strict_lower)


def _moe_kernel(cnt_ref, h_ref, dg_ref, pos_ref, wg_ref, wu_ref, wd_ref, x_ref, gt_ref, o_ref,
                xs_scr, y_scr, *, n_experts, br):
    ti, e, fc = pl.program_id(0), pl.program_id(1), pl.program_id(2)
    last_fc = pl.num_programs(2) - 1
    tm = h_ref.shape[0]

    @pl.when((e == 0) & (fc == 0))
    def _():
        o_ref[...] = jnp.zeros_like(o_ref)

    sel = lax.broadcasted_iota(jnp.int32, (tm, LANES), 1) == e
    ge = jnp.sum(jnp.where(sel, dg_ref[...], 0.0), axis=-1, keepdims=True)
    pos = jnp.sum(jnp.where(sel, pos_ref[...], 0.0), axis=-1, keepdims=True)
    routed = ge > 0.0
    n_blocks = (cnt_ref[ti * n_experts + e] + br - 1) // br

    def block(r0, rows_in_block):
        rows = pl.ds(r0, rows_in_block)
        slot = (r0 + lax.broadcasted_iota(jnp.int32, (tm, rows_in_block), 1)).astype(F32)
        onehot = jnp.where((pos == slot) & routed, 1.0, 0.0).astype(BF)

        @pl.when(fc == 0)
        def _():
            xs_scr[rows, :] = _tn_dot(onehot, h_ref[...]).astype(BF)

        xs = xs_scr[rows, :]
        a = _dot(xs, wg_ref[0, 0])
        u = _dot(xs, wu_ref[0, 0])
        y = _dot((a * jax.nn.sigmoid(a) * u).astype(BF), wd_ref[0, 0])

        @pl.when(fc == 0)
        def _():
            y_scr[rows, :] = y

        @pl.when(fc > 0)
        def _():
            y_scr[rows, :] += y

        @pl.when(fc == last_fc)
        def _():
            o_ref[...] += ge * _dot(onehot, y_scr[rows, :].astype(BF))

    def pair(i, carry):
        block(pl.multiple_of(i * 2 * br, 2 * br), 2 * br)
        return carry

    lax.fori_loop(0, n_blocks // 2, pair, 0)

    @pl.when(n_blocks % 2 == 1)
    def _():
        block(pl.multiple_of((n_blocks - 1) * br, br), br)

    @pl.when((e == n_experts - 1) & (fc == last_fc))
    def _():
        o_ref[...] = x_ref[...] + _rows(gt_ref, tm) * o_ref[...]


def moe_ffn(h, dense_gate, rank, counts, w_gu, w_down, layer, x, rows_per_seq, mod, gt_chunk, tm,
            ff_chunks, block_rows):
    t, d = x.shape
    _, e, _, f2 = w_gu.shape
    f = f2 // 2
    fcw = f // ff_chunks
    tok = pl.BlockSpec((tm, LANES), lambda i, k, c, cnt: (i, 0))
    grid_spec = pltpu.PrefetchScalarGridSpec(
        num_scalar_prefetch=1,
        grid=(t // tm, e, ff_chunks),
        in_specs=[pl.BlockSpec((tm, d), lambda i, k, c, cnt: (i, 0)), tok, tok,
                  pl.BlockSpec((1, 1, d, fcw), lambda i, k, c, cnt: (layer, k, 0, c)),
                  pl.BlockSpec((1, 1, d, fcw), lambda i, k, c, cnt: (layer, k, 0, ff_chunks + c)),
                  pl.BlockSpec((1, 1, fcw, d), lambda i, k, c, cnt: (layer, k, c, 0)),
                  pl.BlockSpec((tm, d), lambda i, k, c, cnt: (i, 0)),
                  _mod_spec(rows_per_seq, tm, d, lambda i, k, c, cnt: gt_chunk)],
        out_specs=pl.BlockSpec((tm, d), lambda i, k, c, cnt: (i, 0)),
        scratch_shapes=[pltpu.VMEM((tm, d), BF), pltpu.VMEM((tm, d), F32)],
    )
    return pl.pallas_call(
        functools.partial(_moe_kernel, n_experts=e, br=block_rows),
        grid_spec=grid_spec,
        out_shape=jax.ShapeDtypeStruct((t, d), F32),
        compiler_params=_cparams(("parallel", "arbitrary", "arbitrary")),
        name="moe_ffn",
    )(counts, h, dense_gate, rank, w_gu, w_gu, w_down, x, mod)


def _swiglu_up_kernel(x_ref, g_ref, sc_ref, sh_ref, wg_ref, wu_ref, o_ref, h_scr):
    @pl.when(pl.program_id(1) == 0)
    def _():
        h_scr[...] = _modulated(x_ref, g_ref, sc_ref, sh_ref).astype(BF)

    h = h_scr[...]
    a = _dot(h, wg_ref[0].astype(BF))
    u = _dot(h, wu_ref[0].astype(BF))
    o_ref[...] = (a * jax.nn.sigmoid(a) * u).astype(o_ref.dtype)


def swiglu_up(x, rows_per_seq, mod, sc_chunk, sh_chunk, g, w_gu, layer, tm, tn):
    t, d = x.shape
    f = w_gu.shape[2] // 2
    tn = _tile(f, tn)
    nf = f // tn
    return pl.pallas_call(
        _swiglu_up_kernel,
        grid=(t // tm, nf),
        in_specs=[pl.BlockSpec((tm, d), lambda i, j: (i, 0)),
                  pl.BlockSpec((1, d), lambda i, j: (0, 0)),
                  _mod_spec(rows_per_seq, tm, d, lambda i, j: sc_chunk),
                  _mod_spec(rows_per_seq, tm, d, lambda i, j: sh_chunk),
                  pl.BlockSpec((1, d, tn), lambda i, j: (layer, 0, j)),
                  pl.BlockSpec((1, d, tn), lambda i, j: (layer, 0, nf + j))],
        out_specs=pl.BlockSpec((tm, tn), lambda i, j: (i, j)),
        out_shape=jax.ShapeDtypeStruct((t, f), BF),
        scratch_shapes=[pltpu.VMEM((tm, d), BF)],
        compiler_params=_cparams(("parallel", "arbitrary")),
        name="swiglu_up",
    )(x, g.reshape(1, d), mod, mod, w_gu, w_gu)


def _down_kernel(y_ref, w_ref, x_ref, gt_ref, o_ref):
    p = _dot(y_ref[...].astype(BF), w_ref[0].astype(BF))
    o_ref[...] = x_ref[...] + _rows(gt_ref, x_ref.shape[0]) * p


def down_residual(y, w, layer, x, rows_per_seq, mod, gt_chunk, tm, tn, name="down_residual"):
    t, k = y.shape
    d = x.shape[1]
    tn = _tile(d, tn)
    nd = d // tn
    return pl.pallas_call(
        _down_kernel,
        grid=(t // tm, nd),
        in_specs=[pl.BlockSpec((tm, k), lambda i, j: (i, 0)),
                  pl.BlockSpec((1, k, tn), lambda i, j: (layer, 0, j)),
                  pl.BlockSpec((tm, tn), lambda i, j: (i, j)),
                  _mod_spec(rows_per_seq, tm, tn, lambda i, j: gt_chunk * nd + j)],
        out_specs=pl.BlockSpec((tm, tn), lambda i, j: (i, j)),
        out_shape=jax.ShapeDtypeStruct((t, d), F32),
        compiler_params=_cparams(("parallel", "arbitrary")),
        name=name,
    )(y, w, x, mod)


def _log_sigmoid(x):
    return jnp.minimum(x, 0.0) - jnp.log1p(jnp.exp(-jnp.abs(x)))


def _mlstm_kernel(qkv_ref, o_ref, gt_ref, bg_ref, gh_ref, c0_ref, n0_ref, m0_ref, *rest,
                  heads, dk, dv, chunk, seqs, unroll):
    y_ref, c_all, n_all, m_all = rest[-4:]
    c_ref, n_ref, m_ref = c_all.at[0], n_all.at[0], m_all.at[0]
    L = chunk
    hi = lax.Precision.HIGHEST
    cd = BF if L >= 16 else F32

    @pl.when(pl.program_id(1) == 0)
    def _():
        c_ref[...] = c0_ref[0]
        n_ref[...] = n0_ref[0]
        m_ref[...] = m0_ref[0]

    row = lax.broadcasted_iota(jnp.int32, (L, L), 0)
    col = lax.broadcasted_iota(jnp.int32, (L, L), 1)
    causal = row >= col
    tril = causal.astype(F32)
    sel = (lax.broadcasted_iota(jnp.int32, (SUBLANES, LANES), 0)
           == lax.broadcasted_iota(jnp.int32, (SUBLANES, LANES), 1)).astype(F32)

    def one_seq(bi):
        gates = gt_ref[bi] + bg_ref[...]
        lf = _log_sigmoid(gates)
        bcum = _dot(tril, lf, precision=hi)
        g_rows = _nt_dot(sel, gates, precision=hi)
        b_rows = _nt_dot(sel, bcum, precision=hi)
        for h in range(heads):
            q = qkv_ref[bi, :, h * dk:(h + 1) * dk]
            k = qkv_ref[bi, :, (heads + h) * dk:(heads + h + 1) * dk]
            v = qkv_ref[bi, :, 2 * heads * dk + h * dv:2 * heads * dk + (h + 1) * dv]
            qf, kf = q.astype(F32), k.astype(F32)
            qc, kc, vc = q.astype(cd), k.astype(cd), v.astype(cd)
            ig_col = gates[:, h:h + 1]
            ig_row = g_rows[h:h + 1, :]
            b_col = bcum[:, heads + h:heads + h + 1]
            b_row = b_rows[heads + h:heads + h + 1, :]
            b_end = bcum[L - 1:L, heads + h:heads + h + 1]
            c_old = c_ref[bi, h]
            n_old = n_ref[bi, h]
            m_old = m_ref[bi, h][:, :1]

            dlog = jnp.where(causal, b_col - b_row + ig_row, NEG_INF)
            gcar = b_col + m_old
            m_t = jnp.maximum(gcar, jnp.max(dlog, axis=-1, keepdims=True))
            s = _nt_dot(qc, kc) * jnp.exp(dlog - m_t)
            dec = jnp.exp(gcar - m_t)
            num = _dot(s.astype(cd), vc) + dec * _dot(qc, c_old.astype(cd))
            den = (jnp.sum(s, axis=-1, keepdims=True)
                   + dec * jnp.sum(qf * n_old, axis=-1, keepdims=True))
            hh = num / jnp.maximum(jnp.abs(den), jnp.exp(-m_t))

            a_row = b_end - b_row + ig_row
            a_col = b_end - b_col + ig_col
            m_new = jnp.maximum(b_end + m_old, jnp.max(a_row, axis=-1, keepdims=True))
            wk = jnp.exp(a_col - m_new)
            decay = jnp.exp(b_end + m_old - m_new)
            kw = kf * wk
            c_ref[bi, h] = decay * c_old + _tn_dot(kw.astype(cd), vc)
            n_ref[bi, h] = decay * n_old + jnp.sum(kw, axis=0, keepdims=True)
            m_ref[bi, h] = jnp.broadcast_to(m_new, (1, LANES))

            hs = (hh * lax.rsqrt(jnp.mean(hh * hh, axis=-1, keepdims=True) + EPS)
                  * gh_ref[:, h * dv:(h + 1) * dv])
            og = o_ref[bi, :, h * dv:(h + 1) * dv]
            y_ref[bi, :, h * dv:(h + 1) * dv] = (hs * jax.nn.sigmoid(og)).astype(y_ref.dtype)

    if unroll:
        for bi in range(seqs):
            one_seq(bi)
    else:
        def body(bi, carry):
            one_seq(bi)
            return carry
        lax.fori_loop(0, seqs, body, 0)


def mlstm_cell(qkv, og, gates, b_gate, g_head, c0, n0, m0, seq_len, seqs_per_step, layer, stacked):
    t = qkv.shape[0]
    n_in_layers, b, heads, dk, dv = c0.shape
    chunk = M_CHUNK if seq_len % M_CHUNK == 0 else seq_len
    nc = seq_len // chunk
    bt = seqs_per_step
    n4 = n0.reshape(n_in_layers, b, heads, 1, dk)
    m4 = jnp.broadcast_to(m0.reshape(n_in_layers, b, heads, 1, 1), (n_in_layers, b, heads, 1, LANES))
    bg = jnp.pad(b_gate.astype(F32), (0, LANES - 2 * heads)).reshape(1, LANES)
    y_dtype = BF if chunk >= 16 else F32
    tok = lambda i, c: (i, c, 0)
    st = lambda i, c: (layer, i, 0, 0, 0)
    st_out = lambda i, c: (layer, i, 0, 0, 0)
    first = isinstance(stacked, int)
    n_layers = stacked if first else stacked[0].shape[0]
    prev = [] if first else list(stacked)
    n_in = 8
    y, c_all, n_all, m_all = pl.pallas_call(
        functools.partial(_mlstm_kernel, heads=heads, dk=dk, dv=dv, chunk=chunk, seqs=bt,
                          unroll=bt <= 2),
        grid=(b // bt, nc),
        in_specs=[pl.BlockSpec((bt, chunk, qkv.shape[1]), tok),
                  pl.BlockSpec((bt, chunk, heads * dv), tok),
                  pl.BlockSpec((bt, chunk, LANES), tok),
                  pl.BlockSpec((1, LANES), lambda i, c: (0, 0)),
                  pl.BlockSpec((1, heads * dv), lambda i, c: (0, 0)),
                  pl.BlockSpec((1, bt, heads, dk, dv), st),
                  pl.BlockSpec((1, bt, heads, 1, dk), st),
                  pl.BlockSpec((1, bt, heads, 1, LANES), st)]
                 + [pl.BlockSpec(memory_space=pl.ANY)] * len(prev),
        out_specs=[pl.BlockSpec((bt, chunk, heads * dv), tok),
                   pl.BlockSpec((1, bt, heads, dk, dv), st_out),
                   pl.BlockSpec((1, bt, heads, 1, dk), st_out),
                   pl.BlockSpec((1, bt, heads, 1, LANES), st_out)],
        out_shape=[jax.ShapeDtypeStruct((b, seq_len, heads * dv), y_dtype),
                   jax.ShapeDtypeStruct((n_layers, b, heads, dk, dv), F32),
                   jax.ShapeDtypeStruct((n_layers, b, heads, 1, dk), F32),
                   jax.ShapeDtypeStruct((n_layers, b, heads, 1, LANES), F32)],
        input_output_aliases={n_in + i: 1 + i for i in range(len(prev))},
        compiler_params=_cparams(("parallel", "arbitrary")),
        name="mlstm_cell",
    )(qkv.reshape(b, seq_len, -1), og.reshape(b, seq_len, -1), gates.reshape(b, seq_len, LANES),
      bg, g_head.reshape(1, heads * dv).astype(F32), c0, n4, m4, *prev)
    return y.reshape(t, heads * dv), (c_all, n_all, m_all)


def _lambda(lam_ref, lam_init):
    lp = lam_ref[...]
    a = jnp.sum(lp[0:1] * lp[1:2], axis=-1, keepdims=True)
    b = jnp.sum(lp[2:3] * lp[3:4], axis=-1, keepdims=True)
    return jnp.exp(a) - jnp.exp(b) + lam_init


def _sub_norm(o, gsub_ref, lam_init):
    y = o * lax.rsqrt(jnp.mean(o * o, axis=-1, keepdims=True) + EPS) * gsub_ref[...]
    return y * (1.0 - lam_init)


def _flash_kernel(qi_ref, ki_ref, qt_ref, k_ref, vt_ref, bound_ref, lam_ref, gs_ref, o_ref,
                  qm_scr, m_scr, acc_scr, *, dh, tq, tk, lam_init, online):
    pair = pl.program_id(2)
    qi = qi_ref[pair]
    ki = ki_ref[pair]
    dv = LANES
    ext = 2 * SUBLANES

    @pl.when(ki == 0)
    def _():
        q = qt_ref[0] * (dh ** -0.5)
        row = lax.broadcasted_iota(jnp.int32, q.shape, 0)
        qm_scr[0] = jnp.where(row < dh, q, jnp.zeros_like(q))
        qm_scr[1] = jnp.where(row >= dh, q, jnp.zeros_like(q))
        m_scr[...] = jnp.full_like(m_scr, NEG_INF)
        acc_scr[...] = jnp.zeros_like(acc_scr)

    def absorb(masked):
        k = k_ref[...]
        ones_row = (lax.broadcasted_iota(jnp.int32, (ext, tk), 0) == 0).astype(BF)
        vx = jnp.concatenate([vt_ref[0], ones_row], axis=0)
        if masked and tq == tk:
            subs = [(0, tq // 2, 0, tk // 2), (tq // 2, tq, 0, tk)]
        else:
            subs = [(0, tq, 0, tk)]
        for q0, q1, k0, k1 in subs:
            if masked:
                key = ki * tk + k0 + lax.broadcasted_iota(jnp.int32, (k1 - k0, q1 - q0), 0)
                qry = qi * tq + q0 + lax.broadcasted_iota(jnp.int32, (k1 - k0, q1 - q0), 1)
                keep = qry >= key
            for c in range(2):
                s = _dot(k[k0:k1], qm_scr[c, :, q0:q1])
                if masked:
                    s = jnp.where(keep, s, NEG_INF)
                if online:
                    m_prev = m_scr[c, :, q0:q1]
                    m_new = jnp.maximum(m_prev, jnp.max(s, axis=0, keepdims=True))
                    p = jnp.exp(s - m_new)
                    acc_scr[c, :, q0:q1] = (jnp.exp(m_prev - m_new) * acc_scr[c, :, q0:q1]
                                            + _dot(vx[:, k0:k1], p.astype(BF)))
                    m_scr[c, :, q0:q1] = m_new
                else:
                    p = jnp.exp(s - bound_ref[...])
                    acc_scr[c, :, q0:q1] += _dot(vx[:, k0:k1], p.astype(BF))

    needs_mask = (ki + 1) * tk - 1 > qi * tq

    @pl.when(needs_mask)
    def _():
        absorb(True)

    @pl.when(jnp.logical_not(needs_mask))
    def _():
        absorb(False)

    @pl.when(ki == (qi * tq + tq - 1) // tk)
    def _():
        lam = _lambda(lam_ref, lam_init)
        a0 = acc_scr[0]
        a1 = acc_scr[1]
        o = a0[:dv] / a0[dv:dv + 1] - lam * (a1[:dv] / a1[dv:dv + 1])
        y = o * lax.rsqrt(jnp.mean(o * o, axis=0, keepdims=True) + EPS) * gs_ref[...]
        o_ref[...] = (y * (1.0 - lam_init)).T.astype(o_ref.dtype)


SAFE_SCORE_BOUND = 30.0


def flash_diff_attention(qt, k, vt, score_bound, lam_p, g_sub, heads, lam_init, tq, tk):
    t = k.shape[0]
    n_seq, _, seq = qt.shape
    dv = vt.shape[1] // heads
    dh = k.shape[1] // (2 * heads)
    assert 2 * dh == LANES and dv == LANES
    nq, nk = seq // tq, seq // tk
    pairs = [(i, j) for i in range(nq) for j in range((i * tq + tq - 1) // tk + 1)]
    qi_tab = jnp.asarray([p[0] for p in pairs], jnp.int32)
    ki_tab = jnp.asarray([p[1] for p in pairs], jnp.int32)

    def call(online):
        grid_spec = pltpu.PrefetchScalarGridSpec(
            num_scalar_prefetch=2,
            grid=(n_seq, heads, len(pairs)),
            in_specs=[pl.BlockSpec((1, LANES, tq), lambda b, h, p, qi, ki: (b, h, qi[p])),
                      pl.BlockSpec((tk, LANES), lambda b, h, p, qi, ki: (b * nk + ki[p], h)),
                      pl.BlockSpec((1, LANES, tk), lambda b, h, p, qi, ki: (b, h, ki[p])),
                      pl.BlockSpec((1, 1), lambda b, h, p, qi, ki: (0, 0)),
                      pl.BlockSpec(lam_p.shape, lambda b, h, p, qi, ki: (0, 0)),
                      pl.BlockSpec((dv, 1), lambda b, h, p, qi, ki: (0, 0))],
            out_specs=pl.BlockSpec((tq, LANES), lambda b, h, p, qi, ki: (b * nq + qi[p], h)),
            scratch_shapes=[pltpu.VMEM((2, LANES, tq), BF),
                            pltpu.VMEM((2, 1, tq), F32),
                            pltpu.VMEM((2, dv + 2 * SUBLANES, tq), F32)],
        )
        return pl.pallas_call(
            functools.partial(_flash_kernel, dh=dh, tq=tq, tk=tk, lam_init=lam_init, online=online),
            grid_spec=grid_spec,
            out_shape=jax.ShapeDtypeStruct((t, heads * dv), BF),
            compiler_params=_cparams(("parallel", "parallel", "arbitrary")),
            name="flash_online" if online else "flash_bounded",
        )(qi_tab, ki_tab, qt, k, vt, score_bound.reshape(1, 1).astype(F32), lam_p.astype(F32),
          g_sub.reshape(dv, 1).astype(F32))

    return lax.cond(score_bound < SAFE_SCORE_BOUND, lambda: call(False), lambda: call(True))


def _paged_kernel(pt_ref, q_ref, kn_ref, vn_ref, lam_ref, gs_ref, *rest,
                  heads, dh, n_new, pages, page, lam_init):
    del pt_ref
    kt_refs = rest[:pages]
    v_refs = rest[pages:2 * pages]
    o_ref, s_scr = rest[2 * pages:]
    width = heads * 2 * dh
    nrow = 2 * heads * n_new
    past = pages * page

    q = q_ref[...] * (dh ** -0.5)
    qt = jnp.concatenate([q] * (2 * heads), axis=0)
    r = lax.broadcasted_iota(jnp.int32, (nrow, width), 0)
    cblk = lax.broadcasted_iota(jnp.int32, (nrow, width), 1) // dh
    qbd = jnp.where(cblk == r // n_new, qt, 0.0).astype(BF)

    pad = jnp.zeros((page - n_new, width), F32)
    kn = jnp.concatenate([kn_ref[...], pad], axis=0).astype(BF)
    vn = jnp.concatenate([vn_ref[...], pad], axis=0).astype(BF)

    def softmax_part(page_ids, with_new):
        c0 = page_ids[0] * page
        c1 = (page_ids[-1] + 1) * page
        for p in page_ids:
            s_scr[:, p * page:(p + 1) * page] = _dot(qbd, kt_refs[p][0, 0].astype(BF))
        if with_new:
            s_new = _nt_dot(qbd, kn)
            tok = lax.broadcasted_iota(jnp.int32, s_new.shape, 0) % n_new
            key = lax.broadcasted_iota(jnp.int32, s_new.shape, 1)
            s_scr[:, past:] = jnp.where(key <= tok, s_new, NEG_INF)
            c1 = past + page
        s = s_scr[:, c0:c1]
        m = jnp.max(s, axis=-1, keepdims=True)
        pr = jnp.exp(s - m)
        l = jnp.sum(pr, axis=-1, keepdims=True)
        s_scr[:, c0:c1] = pr
        acc = _dot(s_scr[:, past:].astype(BF), vn) if with_new else jnp.zeros((nrow, width), F32)
        for p in page_ids:
            v_wide = jnp.concatenate(
                [v_refs[p][0, 0, pl.ds(h, page, stride=heads), :] for h in range(heads)], axis=1)
            acc += _dot(s_scr[:, p * page:(p + 1) * page].astype(BF), v_wide.astype(BF))
        return acc, l, m

    half = max(pages // 2, 1)
    acc, l, m = softmax_part(list(range(half, pages)), True) if half < pages else (None, None, None)
    acc_a, l_a, m_a = softmax_part(list(range(half)), half == pages)
    if acc is None:
        acc, l = acc_a, l_a
    else:
        m_all = jnp.maximum(m, m_a)
        w, w_a = jnp.exp(m - m_all), jnp.exp(m_a - m_all)
        acc = w * acc + w_a * acc_a
        l = w * l + w_a * l_a

    lam = _lambda(lam_ref, lam_init)
    acc = acc * (1.0 / l)
    for h in range(heads):
        r0 = h * 2 * n_new
        cols = slice(h * 2 * dh, (h + 1) * 2 * dh)
        o = acc[r0:r0 + n_new, cols] - lam * acc[r0 + n_new:r0 + 2 * n_new, cols]
        o_ref[:, cols] = _sub_norm(o, gs_ref, lam_init)


def paged_diff_attention(q, k_new, v_new, cache_kt, cache_v, layer, page_table, lam_p, g_sub,
                         heads, lam_init):
    t, width = q.shape
    b, n_pages = page_table.shape
    n_new = t // b
    dh = width // (2 * heads)
    page = cache_kt.shape[3]
    assert n_new == SUBLANES and 2 * dh == LANES and page == LANES
    nrow = 2 * heads * n_new
    tok = pl.BlockSpec((n_new, width), lambda i, pt: (i, 0))

    def page_spec(p, shape):
        return pl.BlockSpec((1, 1) + shape, lambda i, pt: (layer, pt[i, p], 0, 0))

    grid_spec = pltpu.PrefetchScalarGridSpec(
        num_scalar_prefetch=1,
        grid=(b,),
        in_specs=[tok, tok, tok,
                  pl.BlockSpec(lam_p.shape, lambda i, pt: (0, 0)),
                  pl.BlockSpec((1, 2 * dh), lambda i, pt: (0, 0))]
                 + [page_spec(p, (width, page)) for p in range(n_pages)]
                 + [page_spec(p, (page * heads, 2 * dh)) for p in range(n_pages)],
        out_specs=tok,
        scratch_shapes=[pltpu.VMEM((nrow, (n_pages + 1) * page), F32)],
    )
    return pl.pallas_call(
        functools.partial(_paged_kernel, heads=heads, dh=dh, n_new=n_new, pages=n_pages,
                          page=page, lam_init=lam_init),
        grid_spec=grid_spec,
        out_shape=jax.ShapeDtypeStruct((t, width), F32),
        compiler_params=_cparams(("parallel",)),
        name="paged_diff_attention",
    )(page_table, q, k_new, v_new, lam_p.astype(F32), g_sub.reshape(1, 2 * dh).astype(F32),
      *([cache_kt] * n_pages), *([cache_v] * n_pages))


def _trunk(x, rows_per_seq, mods, state, paged, weights, cfg):
    (g_norm_mix, g_norm_ffn, w_in_m, b_gate_m, g_head_m, w_out_m,
     w_in_d, g_q_d, g_k_d, lam_d, g_sub_d, w_out_d, w_gu_f, w_down_f,
     w_router, b_router, w_gu_e, w_down_e) = weights
    t, d = x.shape
    n_seq = t // rows_per_seq
    depth = g_norm_mix.shape[0]
    tm = cfg["tm"]
    tn_proj, tn_seg, tn_up = 1024, 256, 1408
    heads_m = b_gate_m.shape[1] // 2
    dk, dv = state[0].shape[3], state[0].shape[4]
    qk_w = heads_m * dk
    heads_a, dh = cfg["heads_a"], cfg["dh"]
    new_state, new_kv = state[0].shape[0], []
    kt_stack = None
    for i in range(depth):
        j = i // 2
        mod = mods[i]
        if i % 2 == 0:
            col_scale = jnp.concatenate([jnp.ones((qk_w,), F32), jnp.full((qk_w,), dk ** -0.5, F32),
                                         jnp.ones((heads_m * dv,), F32)]).reshape(1, -1)
            small = rows_per_seq % M_CHUNK != 0
            (qkv,) = norm_proj(x, rows_per_seq, mod, 1, 0, g_norm_mix[i], w_in_m[j], 0,
                               2 * qk_w + heads_m * dv, [F32 if small else BF], tm, tn_proj,
                               col_scale=col_scale, name="mlstm_qkv")
            (og,) = norm_proj(x, rows_per_seq, mod, 1, 0, g_norm_mix[i], w_in_m[j],
                              2 * qk_w + heads_m * dv, heads_m * dv, [F32], tm, tn_proj, name="mlstm_o")
            w_gate = jnp.pad(w_in_m[j][:, 2 * qk_w + 2 * heads_m * dv:], ((0, 0), (0, LANES - 2 * heads_m)))
            (gates,) = norm_proj(x, rows_per_seq, mod, 1, 0, g_norm_mix[i], w_gate, 0, LANES,
                                 [F32], tm, LANES, name="mlstm_gates")
            y, new_state = mlstm_cell(qkv, og, gates, b_gate_m[j], g_head_m[j], *state,
                                      rows_per_seq, cfg["mlstm_seqs"], j, new_state)
            w_out = w_out_m
        else:
            lam_init = 0.8 - 0.6 * math.exp(-0.3 * i)
            prompt = paged is None
            (q,) = norm_proj(x, rows_per_seq, mod, 1, 0, g_norm_mix[i], w_in_d[j], 0, d,
                             [BF if prompt else F32], tm, tn_seg, seg_gain=g_q_d[j],
                             transposed=(prompt,), name="attn_q")
            k32, kbf = norm_proj(x, rows_per_seq, mod, 1, 0, g_norm_mix[i], w_in_d[j], d, d,
                                 [F32, BF], tm, tn_seg, seg_gain=g_k_d[j], transposed=(prompt, False),
                                 stacked=(j, depth // 2, kt_stack), name="attn_k")
            v32, vbf = norm_proj(x, rows_per_seq, mod, 1, 0, g_norm_mix[i], w_in_d[j], 2 * d, d,
                                 [F32, BF], tm, tn_proj, transposed=(False, prompt), name="attn_v")
            if paged is None:
                bound = (dh ** 0.5) * jnp.max(jnp.abs(g_q_d[j])) * jnp.max(jnp.abs(g_k_d[j]))
                y = flash_diff_attention(q, kbf, vbf, bound, lam_d[j], g_sub_d[j], heads_a,
                                         lam_init, cfg["tq"], cfg["tk"])
                kt_stack = k32
            else:
                cache_kt, cache_v, page_table = paged
                y = paged_diff_attention(q, k32, v32, cache_kt, cache_v, j, page_table, lam_d[j],
                                         g_sub_d[j], heads_a, lam_init)
            new_kv.append((k32, v32))
            w_out = w_out_d
        x = down_residual(y, w_out, j, x, rows_per_seq, mod, 2, tm, tn_proj, name="mixer_out")
        if i % 2 == 0:
            act = swiglu_up(x, rows_per_seq, mod, 4, 3, g_norm_ffn[i], w_gu_f, j, tm, tn_up)
            x = down_residual(act, w_down_f, j, x, rows_per_seq, mod, 5, tm, tn_proj, name="ffn_down")
        else:
            n_exp = w_router.shape[2]
            h, dg, rank, counts = router_gates(x, rows_per_seq, mod, 4, 3, g_norm_ffn[i],
                                               w_router[j], b_router[j], tm)
            counts = counts[:, 0, :n_exp].astype(jnp.int32).reshape(-1)
            x = moe_ffn(h, dg, rank, counts, w_gu_e, w_down_e, j, x, rows_per_seq, mod, 5, tm,
                        cfg["moe_ff_chunks"], min(cfg["moe_block_rows"], tm // 2))
    return x, new_state, new_kv, kt_stack


def kernel(x_prompt, x_sample, c_prompt, c_sample, state_C, state_n, state_m, cache_k, cache_v, page_table, w_ada, b_ada, g_norm_mix, g_norm_ffn, w_in_m, b_gate_m, g_head_m, w_out_m, w_in_d, g_q_d, g_k_d, lam_d, g_sub_d, w_out_d, w_gu_f, w_down_f, w_router, b_router, w_gu_e, w_down_e):
    bp, seq, d = x_prompt.shape
    bs, dec_seq, _ = x_sample.shape
    depth = w_ada.shape[0]
    n_ml, _, heads_m, dk, dv = state_C.shape
    n_diff, n_pool, page, heads_a, _, dh = cache_k.shape
    weights = (g_norm_mix, g_norm_ffn, w_in_m.astype(BF), b_gate_m, g_head_m, w_out_m.astype(BF),
               w_in_d.astype(BF), g_q_d, g_k_d, lam_d, g_sub_d, w_out_d.astype(BF),
               w_gu_f.astype(BF), w_down_f.astype(BF),
               w_router, b_router, w_gu_e.astype(BF), w_down_e.astype(BF))

    n_c = bp + bs
    n_c_pad = -(-n_c // SUBLANES) * SUBLANES
    c_all = jnp.pad(jnp.concatenate([c_prompt, c_sample], axis=0), ((0, n_c_pad - n_c), (0, 0)))
    mod_all = ada_mod(c_all, w_ada, b_ada, tn=min(6 * d, 1536))
    mods_p = [mod_all[i, :bp].reshape(bp, 1, 6 * d) for i in range(depth)]
    mods_s = [mod_all[i, bp:n_c].reshape(bs, 1, 6 * d) for i in range(depth)]

    moe = dict(moe_ff_chunks=2, moe_block_rows=128)
    cfg_p = dict(tm=min(1024, seq), tq=min(1024, seq), tk=min(1024, seq), heads_a=heads_a, dh=dh,
                 mlstm_seqs=min(2, bp), **moe)
    cfg_s = dict(tm=min(1024, bs * dec_seq), heads_a=heads_a, dh=dh, mlstm_seqs=min(8, bs), **moe)

    zeros = (jnp.zeros((n_ml, bp, heads_m, dk, dv), F32), jnp.zeros((n_ml, bp, heads_m, dk), F32),
             jnp.zeros((n_ml, bp, heads_m), F32))
    y_p, st_p, kv_p, kt_p = _trunk(x_prompt.reshape(bp * seq, d), seq, mods_p, zeros, None, weights, cfg_p)
    paged = (jnp.transpose(cache_k, (0, 1, 3, 4, 5, 2)).reshape(n_diff, n_pool, heads_a * 2 * dh, page),
             cache_v.reshape(n_diff, n_pool, page * heads_a, 2 * dh), page_table)
    y_s, st_s, kv_s, _ = _trunk(x_sample.reshape(bs * dec_seq, d), dec_seq, mods_s,
                             (state_C, state_n, state_m), paged, weights, cfg_s)

    def stack(items, idx, shape):
        return jnp.stack([it[idx] for it in items]).reshape(shape)

    return (y_p.reshape(bp, seq, d), y_s.reshape(bs, dec_seq, d),
            st_p[0], st_p[1].reshape(n_ml, bp, heads_m, dk), st_p[2][:, :, :, 0, 0],
            st_s[0], st_s[1].reshape(n_ml, bs, heads_m, dk), st_s[2][:, :, :, 0, 0],
            jnp.transpose(kt_p.reshape(n_diff, bp, heads_a, 2, dh, seq), (0, 1, 5, 2, 3, 4)),
            stack(kv_p, 1, (n_diff, bp, seq, heads_a, 2 * dh)),
            stack(kv_s, 0, (n_diff, bs, dec_seq, heads_a, 2, dh)),
            stack(kv_s, 1, (n_diff, bs, dec_seq, heads_a, 2 * dh)))
```

```python
import functools
import math

import jax
import jax.numpy as jnp
from jax import lax
from jax.experimental import pallas as pl
from jax.experimental.pallas import tpu as pltpu

BF = jnp.bfloat16
F32 = jnp.float32
EPS = 1e-6
M_CHUNK = 128
LANES = 128
SUBLANES = 8
VMEM_LIMIT = 56 * 1024 * 1024
NEG_INF = float("-inf")


def _cparams(sem):
    return pltpu.CompilerParams(dimension_semantics=sem, vmem_limit_bytes=VMEM_LIMIT)


def _tile(n, pref):
    if n <= pref:
        return n
    t = pref - pref % LANES
    while n % t:
        t -= LANES
    return t


def _nt_dot(a, b, **kw):
    return lax.dot_general(a, b, (((1,), (1,)), ((), ())), preferred_element_type=F32, **kw)


def _tn_dot(a, b, **kw):
    return lax.dot_general(a, b, (((0,), (0,)), ((), ())), preferred_element_type=F32, **kw)


def _dot(a, b, **kw):
    return jnp.dot(a, b, preferred_element_type=F32, **kw)


def _ada_kernel(c_ref, w_ref, b_ref, o_ref):
    c = c_ref[...]
    a = (c * jax.nn.sigmoid(c)).astype(BF)
    o_ref[0] = _dot(a, w_ref[0].astype(BF)) + b_ref[0]


def ada_mod(c_all, w_ada, b_ada, tn):
    depth, d, n = w_ada.shape
    bp = c_all.shape[0]
    return pl.pallas_call(
        _ada_kernel,
        grid=(depth, n // tn),
        in_specs=[pl.BlockSpec((bp, d), lambda i, j: (0, 0)),
                  pl.BlockSpec((1, d, tn), lambda i, j: (i, 0, j)),
                  pl.BlockSpec((1, 1, tn), lambda i, j: (i, 0, j))],
        out_specs=pl.BlockSpec((1, bp, tn), lambda i, j: (i, 0, j)),
        out_shape=jax.ShapeDtypeStruct((depth, bp, n), F32),
        compiler_params=_cparams(("arbitrary", "arbitrary")),
        name="ada_mod",
    )(c_all, w_ada, b_ada.reshape(depth, 1, n))


def _mod_spec(rows_per_seq, tm, width, col_block):
    if rows_per_seq >= tm:
        npb = rows_per_seq // tm
        return pl.BlockSpec((1, 1, width), lambda *g: (g[0] // npb, 0, col_block(*g)))
    gb = tm // rows_per_seq
    return pl.BlockSpec((gb, 1, width), lambda *g: (g[0], 0, col_block(*g)))


def _rows(m_ref, tm):
    m = m_ref[...]
    gb, _, w = m.shape
    if gb == 1:
        return m[0]
    return jnp.broadcast_to(m, (gb, tm // gb, w)).reshape(tm, w)


def _modulated(x_ref, g_ref, sc_ref, sh_ref):
    x = x_ref[...]
    tm = x.shape[0]
    y = x * lax.rsqrt(jnp.mean(x * x, axis=-1, keepdims=True) + EPS) * g_ref[...]
    return y * (1.0 + _rows(sc_ref, tm)) + _rows(sh_ref, tm)


def _store_tiles(outs, y, transposed):
    yt = y.T if any(transposed) else None
    for o, tr in zip(outs, transposed):
        if tr:
            o[(0,) * (len(o.shape) - 2)] = yt.astype(o.dtype)
        else:
            o[...] = y.astype(o.dtype)


def _proj_kernel(x_ref, g_ref, sc_ref, sh_ref, w_ref, cs_ref, *rest, transposed, n_alias):
    outs, h_scr = rest[n_alias:-1], rest[-1]

    @pl.when(pl.program_id(1) == 0)
    def _():
        h_scr[...] = _modulated(x_ref, g_ref, sc_ref, sh_ref).astype(BF)

    acc = _dot(h_scr[...], w_ref[...].astype(BF)) * cs_ref[...]
    _store_tiles(outs, acc, transposed)


def _segnorm_kernel(x_ref, g_ref, sc_ref, sh_ref, w_ref, gv_ref, seg_ref, *rest, seg, transposed,
                    n_alias):
    outs, h_scr = rest[n_alias:-1], rest[-1]

    @pl.when(pl.program_id(1) == 0)
    def _():
        h_scr[...] = _modulated(x_ref, g_ref, sc_ref, sh_ref).astype(BF)

    acc = _dot(h_scr[...], w_ref[...].astype(BF))
    ssq = _dot((acc * acc).astype(BF), seg_ref[...])
    y = acc * lax.rsqrt(ssq * (1.0 / seg) + EPS) * gv_ref[...]
    _store_tiles(outs, y, transposed)


def norm_proj(x, rows_per_seq, mod, sc_chunk, sh_chunk, g, w, col_off, n, out_dtypes, tm, tn,
              col_scale=None, seg_gain=None, transposed=False, stacked=None, name="norm_proj"):
    t, d = x.shape
    prev = []
    if transposed is False:
        transposed = (False,) * len(out_dtypes)
    tn = _tile(math.gcd(n, col_off) if col_off else n, tn)
    x_spec = pl.BlockSpec((tm, d), lambda i, j: (i, 0))
    g_spec = pl.BlockSpec((1, d), lambda i, j: (0, 0))
    sc_spec = _mod_spec(rows_per_seq, tm, d, lambda i, j: sc_chunk)
    sh_spec = _mod_spec(rows_per_seq, tm, d, lambda i, j: sh_chunk)
    off = col_off // tn
    w_spec = pl.BlockSpec((d, tn), lambda i, j: (0, off + j))
    npb = max(rows_per_seq // tm, 1)
    out_specs, out_shape = [], []
    for dt, tr in zip(out_dtypes, transposed):
        if tr:
            out_specs.append(pl.BlockSpec((1, tn, tm), lambda i, j: (i // npb, j, i % npb)))
            out_shape.append(jax.ShapeDtypeStruct((t // rows_per_seq, n, rows_per_seq), dt))
        else:
            out_specs.append(pl.BlockSpec((tm, tn), lambda i, j: (i, j)))
            out_shape.append(jax.ShapeDtypeStruct((t, n), dt))
    if stacked is not None and transposed[0]:
        layer, n_layers, previous = stacked
        out_specs[0] = pl.BlockSpec((1, 1, tn, tm), lambda i, j: (layer, i // npb, j, i % npb))
        out_shape[0] = jax.ShapeDtypeStruct((n_layers,) + out_shape[0].shape, out_dtypes[0])
        prev = [] if previous is None else [previous]
    if seg_gain is None:
        if col_scale is None:
            col_scale = jnp.ones((1, n), F32)
        kern = functools.partial(_proj_kernel, transposed=transposed, n_alias=len(prev))
        extra = [col_scale]
        extra_specs = [pl.BlockSpec((1, tn), lambda i, j: (0, j))]
    else:
        seg = seg_gain.shape[0]
        kern = functools.partial(_segnorm_kernel, seg=seg, transposed=transposed, n_alias=len(prev))
        gv = jnp.tile(seg_gain.astype(F32), tn // seg).reshape(1, tn)
        ids = jnp.arange(tn) // seg
        seg_mat = (ids[:, None] == ids[None, :]).astype(BF)
        extra = [gv, seg_mat]
        extra_specs = [pl.BlockSpec((1, tn), lambda i, j: (0, 0)),
                       pl.BlockSpec((tn, tn), lambda i, j: (0, 0))]
    res = pl.pallas_call(
        kern,
        grid=(t // tm, n // tn),
        in_specs=[x_spec, g_spec, sc_spec, sh_spec, w_spec] + extra_specs
                 + [pl.BlockSpec(memory_space=pl.ANY)] * len(prev),
        out_specs=out_specs,
        out_shape=out_shape,
        scratch_shapes=[pltpu.VMEM((tm, d), BF)],
        input_output_aliases={5 + len(extra) + i: i for i in range(len(prev))},
        compiler_params=_cparams(("parallel", "arbitrary")),
        name=name,
    )(x, g.reshape(1, d), mod, mod, w, *extra, *prev)
    return res


def _router_kernel(x_ref, g_ref, sc_ref, sh_ref, w_ref, b_ref, tri_ref,
                   h_ref, o_ref, pos_ref, cnt_ref, *, n_experts):
    h = _modulated(x_ref, g_ref, sc_ref, sh_ref)
    h_ref[...] = h.astype(h_ref.dtype)
    logits = _dot(h, w_ref[...], precision=lax.Precision.HIGHEST) + b_ref[...]
    lane = lax.broadcasted_iota(jnp.int32, logits.shape, 1).astype(F32)
    big = float(LANES)
    l1 = jnp.where(lane < n_experts, logits, NEG_INF)
    m1 = jnp.max(l1, axis=-1, keepdims=True)
    i1 = jnp.min(jnp.where(l1 == m1, lane, big), axis=-1, keepdims=True)
    l2 = jnp.where(lane == i1, NEG_INF, l1)
    m2 = jnp.max(l2, axis=-1, keepdims=True)
    i2 = jnp.min(jnp.where(l2 == m2, lane, big), axis=-1, keepdims=True)
    e2 = jnp.exp(m2 - m1)
    w1 = 1.0 / (1.0 + e2)
    w2 = e2 / (1.0 + e2)
    gate = jnp.where(lane == i1, w1, 0.0) + jnp.where(lane == i2, w2, 0.0)
    o_ref[...] = gate
    routed = jnp.where(gate > 0.0, 1.0, 0.0)
    pos_ref[...] = _dot(tri_ref[...], routed.astype(BF))
    cnt_ref[0] = jnp.sum(routed, axis=0, keepdims=True)


def router_gates(x, rows_per_seq, mod, sc_chunk, sh_chunk, g, w_router, b_router, tm):
    t, d = x.shape
    e = w_router.shape[1]
    w_pad = jnp.pad(w_router, ((0, 0), (0, LANES - e)))
    b_pad = jnp.pad(b_router, (0, LANES - e)).reshape(1, LANES)
    idx = jnp.arange(tm)
    strict_lower = (idx[:, None] > idx[None, :]).astype(BF)
    tok = pl.BlockSpec((tm, LANES), lambda i: (i, 0))
    return pl.pallas_call(
        functools.partial(_router_kernel, n_experts=e),
        grid=(t // tm,),
        in_specs=[pl.BlockSpec((tm, d), lambda i: (i, 0)),
                  pl.BlockSpec((1, d), lambda i: (0, 0)),
                  _mod_spec(rows_per_seq, tm, d, lambda i: sc_chunk),
                  _mod_spec(rows_per_seq, tm, d, lambda i: sh_chunk),
                  pl.BlockSpec((d, LANES), lambda i: (0, 0)),
                  pl.BlockSpec((1, LANES), lambda i: (0, 0)),
                  pl.BlockSpec((tm, tm), lambda i: (0, 0))],
        out_specs=[pl.BlockSpec((tm, d), lambda i: (i, 0)), tok, tok,
                   pl.BlockSpec((1, 1, LANES), lambda i: (i, 0, 0))],
        out_shape=[jax.ShapeDtypeStruct((t, d), BF),
                   jax.ShapeDtypeStruct((t, LANES), F32),
                   jax.ShapeDtypeStruct((t, LANES), F32),
                   jax.ShapeDtypeStruct((t // tm, 1, LANES), F32)],
        compiler_params=_cparams(("parallel",)),
        name="router",
    )(x, g.reshape(1, d), mod, mod, w_pad, b_pad, strict_lower)


def _moe_kernel(cnt_ref, h_ref, dg_ref, pos_ref, wg_ref, wu_ref, wd_ref, x_ref, gt_ref, o_ref,
                xs_scr, y_scr, *, n_experts, br):
    ti, e, fc = pl.program_id(0), pl.program_id(1), pl.program_id(2)
    last_fc = pl.num_programs(2) - 1
    tm = h_ref.shape[0]

    @pl.when((e == 0) & (fc == 0))
    def _():
        o_ref[...] = jnp.zeros_like(o_ref)

    sel = lax.broadcasted_iota(jnp.int32, (tm, LANES), 1) == e
    ge = jnp.sum(jnp.where(sel, dg_ref[...], 0.0), axis=-1, keepdims=True)
    pos = jnp.sum(jnp.where(sel, pos_ref[...], 0.0), axis=-1, keepdims=True)
    routed = ge > 0.0
    n_blocks = (cnt_ref[ti * n_experts + e] + br - 1) // br

    def block(r0, rows_in_block):
        rows = pl.ds(r0, rows_in_block)
        slot = (r0 + lax.broadcasted_iota(jnp.int32, (tm, rows_in_block), 1)).astype(F32)
        onehot = jnp.where((pos == slot) & routed, 1.0, 0.0).astype(BF)

        @pl.when(fc == 0)
        def _():
            xs_scr[rows, :] = _tn_dot(onehot, h_ref[...]).astype(BF)

        xs = xs_scr[rows, :]
        a = _dot(xs, wg_ref[0, 0])
        u = _dot(xs, wu_ref[0, 0])
        y = _dot((a * jax.nn.sigmoid(a) * u).astype(BF), wd_ref[0, 0])

        @pl.when(fc == 0)
        def _():
            y_scr[rows, :] = y

        @pl.when(fc > 0)
        def _():
            y_scr[rows, :] += y

        @pl.when(fc == last_fc)
        def _():
            o_ref[...] += ge * _dot(onehot, y_scr[rows, :].astype(BF))

    def pair(i, carry):
        block(pl.multiple_of(i * 2 * br, 2 * br), 2 * br)
        return carry

    lax.fori_loop(0, n_blocks // 2, pair, 0)

    @pl.when(n_blocks % 2 == 1)
    def _():
        block(pl.multiple_of((n_blocks - 1) * br, br), br)

    @pl.when((e == n_experts - 1) & (fc == last_fc))
    def _():
        o_ref[...] = x_ref[...] + _rows(gt_ref, tm) * o_ref[...]


def moe_ffn(h, dense_gate, rank, counts, w_gu, w_down, layer, x, rows_per_seq, mod, gt_chunk, tm,
            ff_chunks, block_rows):
    t, d = x.shape
    _, e, _, f2 = w_gu.shape
    f = f2 // 2
    fcw = f // ff_chunks
    tok = pl.BlockSpec((tm, LANES), lambda i, k, c, cnt: (i, 0))
    grid_spec = pltpu.PrefetchScalarGridSpec(
        num_scalar_prefetch=1,
        grid=(t // tm, e, ff_chunks),
        in_specs=[pl.BlockSpec((tm, d), lambda i, k, c, cnt: (i, 0)), tok, tok,
                  pl.BlockSpec((1, 1, d, fcw), lambda i, k, c, cnt: (layer, k, 0, c)),
                  pl.BlockSpec((1, 1, d, fcw), lambda i, k, c, cnt: (layer, k, 0, ff_chunks + c)),
                  pl.BlockSpec((1, 1, fcw, d), lambda i, k, c, cnt: (layer, k, c, 0)),
                  pl.BlockSpec((tm, d), lambda i, k, c, cnt: (i, 0)),
                  _mod_spec(rows_per_seq, tm, d, lambda i, k, c, cnt: gt_chunk)],
        out_specs=pl.BlockSpec((tm, d), lambda i, k, c, cnt: (i, 0)),
        scratch_shapes=[pltpu.VMEM((tm, d), BF), pltpu.VMEM((tm, d), F32)],
    )
    return pl.pallas_call(
        functools.partial(_moe_kernel, n_experts=e, br=block_rows),
        grid_spec=grid_spec,
        out_shape=jax.ShapeDtypeStruct((t, d), F32),
        compiler_params=_cparams(("parallel", "arbitrary", "arbitrary")),
        name="moe_ffn",
    )(counts, h, dense_gate, rank, w_gu, w_gu, w_down, x, mod)


def _swiglu_up_kernel(x_ref, g_ref, sc_ref, sh_ref, wg_ref, wu_ref, o_ref, h_scr):
    @pl.when(pl.program_id(1) == 0)
    def _():
        h_scr[...] = _modulated(x_ref, g_ref, sc_ref, sh_ref).astype(BF)

    h = h_scr[...]
    a = _dot(h, wg_ref[0].astype(BF))
    u = _dot(h, wu_ref[0].astype(BF))
    o_ref[...] = (a * jax.nn.sigmoid(a) * u).astype(o_ref.dtype)


def swiglu_up(x, rows_per_seq, mod, sc_chunk, sh_chunk, g, w_gu, layer, tm, tn):
    t, d = x.shape
    f = w_gu.shape[2] // 2
    tn = _tile(f, tn)
    nf = f // tn
    return pl.pallas_call(
        _swiglu_up_kernel,
        grid=(t // tm, nf),
        in_specs=[pl.BlockSpec((tm, d), lambda i, j: (i, 0)),
                  pl.BlockSpec((1, d), lambda i, j: (0, 0)),
                  _mod_spec(rows_per_seq, tm, d, lambda i, j: sc_chunk),
                  _mod_spec(rows_per_seq, tm, d, lambda i, j: sh_chunk),
                  pl.BlockSpec((1, d, tn), lambda i, j: (layer, 0, j)),
                  pl.BlockSpec((1, d, tn), lambda i, j: (layer, 0, nf + j))],
        out_specs=pl.BlockSpec((tm, tn), lambda i, j: (i, j)),
        out_shape=jax.ShapeDtypeStruct((t, f), BF),
        scratch_shapes=[pltpu.VMEM((tm, d), BF)],
        compiler_params=_cparams(("parallel", "arbitrary")),
        name="swiglu_up",
    )(x, g.reshape(1, d), mod, mod, w_gu, w_gu)


def _down_kernel(y_ref, w_ref, x_ref, gt_ref, o_ref):
    p = _dot(y_ref[...].astype(BF), w_ref[0].astype(BF))
    o_ref[...] = x_ref[...] + _rows(gt_ref, x_ref.shape[0]) * p


def down_residual(y, w, layer, x, rows_per_seq, mod, gt_chunk, tm, tn, name="down_residual"):
    t, k = y.shape
    d = x.shape[1]
    tn = _tile(d, tn)
    nd = d // tn
    return pl.pallas_call(
        _down_kernel,
        grid=(t // tm, nd),
        in_specs=[pl.BlockSpec((tm, k), lambda i, j: (i, 0)),
                  pl.BlockSpec((1, k, tn), lambda i, j: (layer, 0, j)),
                  pl.BlockSpec((tm, tn), lambda i, j: (i, j)),
                  _mod_spec(rows_per_seq, tm, tn, lambda i, j: gt_chunk * nd + j)],
        out_specs=pl.BlockSpec((tm, tn), lambda i, j: (i, j)),
        out_shape=jax.ShapeDtypeStruct((t, d), F32),
        compiler_params=_cparams(("parallel", "arbitrary")),
        name=name,
    )(y, w, x, mod)


def _log_sigmoid(x):
    return jnp.minimum(x, 0.0) - jnp.log1p(jnp.exp(-jnp.abs(x)))


def _mlstm_kernel(qkv_ref, o_ref, gt_ref, bg_ref, gh_ref, c0_ref, n0_ref, m0_ref, *rest,
                  heads, dk, dv, chunk, seqs, unroll):
    y_ref, c_all, n_all, m_all = rest[-4:]
    c_ref, n_ref, m_ref = c_all.at[0], n_all.at[0], m_all.at[0]
    L = chunk
    hi = lax.Precision.HIGHEST
    cd = BF

    @pl.when(pl.program_id(1) == 0)
    def _():
        c_ref[...] = c0_ref[0]
        n_ref[...] = n0_ref[0]
        m_ref[...] = m0_ref[0]

    row = lax.broadcasted_iota(jnp.int32, (L, L), 0)
    col = lax.broadcasted_iota(jnp.int32, (L, L), 1)
    causal = row >= col
    tril = causal.astype(F32)
    sel = (lax.broadcasted_iota(jnp.int32, (SUBLANES, LANES), 0)
           == lax.broadcasted_iota(jnp.int32, (SUBLANES, LANES), 1)).astype(F32)

    def one_seq(bi, results):
        gates = gt_ref[bi] + bg_ref[...]
        lf = _log_sigmoid(gates)
        bcum = _dot(tril, lf, precision=hi)
        g_rows = _nt_dot(sel, gates, precision=hi)
        b_rows = _nt_dot(sel, bcum, precision=hi)
        for h in range(heads):
            q = qkv_ref[bi, :, h * dk:(h + 1) * dk]
            k = qkv_ref[bi, :, (heads + h) * dk:(heads + h + 1) * dk]
            v = qkv_ref[bi, :, 2 * heads * dk + h * dv:2 * heads * dk + (h + 1) * dv]
            qf, kf = q.astype(F32), k.astype(F32)
            qc, kc, vc = q.astype(cd), k.astype(cd), v.astype(cd)
            ig_col = gates[:, h:h + 1]
            ig_row = g_rows[h:h + 1, :]
            b_col = bcum[:, heads + h:heads + h + 1]
            b_row = b_rows[heads + h:heads + h + 1, :]
            b_end = bcum[L - 1:L, heads + h:heads + h + 1]
            c_old = c_ref[bi, h]
            n_old = n_ref[bi, h]
            m_old = m_ref[bi, h][:, :1]

            dlog = jnp.where(causal, b_col - b_row + ig_row, NEG_INF)
            gcar = b_col + m_old
            m_t = jnp.maximum(gcar, jnp.max(dlog, axis=-1, keepdims=True))
            s = _nt_dot(qc, kc) * jnp.exp(dlog - m_t)
            dec = jnp.exp(gcar - m_t)
            num = _dot(s.astype(cd), vc) + dec * _dot(qc, c_old.astype(cd))
            den = (jnp.sum(s, axis=-1, keepdims=True)
                   + dec * jnp.sum(qf * n_old, axis=-1, keepdims=True))
            hh = num / jnp.maximum(jnp.abs(den), jnp.exp(-m_t))

            a_row = b_end - b_row + ig_row
            a_col = b_end - b_col + ig_col
            m_new = jnp.maximum(b_end + m_old, jnp.max(a_row, axis=-1, keepdims=True))
            wk = jnp.exp(a_col - m_new)
            decay = jnp.exp(b_end + m_old - m_new)
            kw = kf * wk
            c_new = decay * c_old + _tn_dot(kw.astype(cd), vc)
            n_new = decay * n_old + jnp.sum(kw, axis=0, keepdims=True)

            hs = (hh * lax.rsqrt(jnp.mean(hh * hh, axis=-1, keepdims=True) + EPS)
                  * gh_ref[:, h * dv:(h + 1) * dv])
            og = o_ref[bi, :, h * dv:(h + 1) * dv]
            y = (hs * jax.nn.sigmoid(og)).astype(y_ref.dtype)
            results.append((bi, h, c_new, n_new, jnp.broadcast_to(m_new, (1, LANES)), y))

    def store(results):
        for bi, h, c_new, n_new, m_new, y in results:
            c_ref[bi, h] = c_new
            n_ref[bi, h] = n_new
            m_ref[bi, h] = m_new
            y_ref[bi, :, h * dv:(h + 1) * dv] = y

    if unroll:
        results = []
        for bi in range(seqs):
            one_seq(bi, results)
        store(results)
    else:
        def body(bi, carry):
            results = []
            one_seq(bi, results)
            store(results)
            return carry
        lax.fori_loop(0, seqs, body, 0)


def mlstm_cell(qkv, og, gates, b_gate, g_head, c0, n0, m0, seq_len, seqs_per_step, layer, stacked):
    t = qkv.shape[0]
    n_in_layers, b, heads, dk, dv = c0.shape
    chunk = M_CHUNK if seq_len % M_CHUNK == 0 else seq_len
    nc = seq_len // chunk
    bt = seqs_per_step
    n4 = n0.reshape(n_in_layers, b, heads, 1, dk)
    m4 = jnp.broadcast_to(m0.reshape(n_in_layers, b, heads, 1, 1), (n_in_layers, b, heads, 1, LANES))
    bg = jnp.pad(b_gate.astype(F32), (0, LANES - 2 * heads)).reshape(1, LANES)
    y_dtype = BF if chunk >= 16 else F32
    tok = lambda i, c: (i, c, 0)
    st = lambda i, c: (layer, i, 0, 0, 0)
    st_out = lambda i, c: (layer, i, 0, 0, 0)
    first = isinstance(stacked, int)
    n_layers = stacked if first else stacked[0].shape[0]
    prev = [] if first else list(stacked)
    n_in = 8
    y, c_all, n_all, m_all = pl.pallas_call(
        functools.partial(_mlstm_kernel, heads=heads, dk=dk, dv=dv, chunk=chunk, seqs=bt,
                          unroll=bt <= 8),
        grid=(b // bt, nc),
        in_specs=[pl.BlockSpec((bt, chunk, qkv.shape[1]), tok),
                  pl.BlockSpec((bt, chunk, heads * dv), tok),
                  pl.BlockSpec((bt, chunk, LANES), tok),
                  pl.BlockSpec((1, LANES), lambda i, c: (0, 0)),
                  pl.BlockSpec((1, heads * dv), lambda i, c: (0, 0)),
                  pl.BlockSpec((1, bt, heads, dk, dv), st),
                  pl.BlockSpec((1, bt, heads, 1, dk), st),
                  pl.BlockSpec((1, bt, heads, 1, LANES), st)]
                 + [pl.BlockSpec(memory_space=pl.ANY)] * len(prev),
        out_specs=[pl.BlockSpec((bt, chunk, heads * dv), tok),
                   pl.BlockSpec((1, bt, heads, dk, dv), st_out),
                   pl.BlockSpec((1, bt, heads, 1, dk), st_out),
                   pl.BlockSpec((1, bt, heads, 1, LANES), st_out)],
        out_shape=[jax.ShapeDtypeStruct((b, seq_len, heads * dv), y_dtype),
                   jax.ShapeDtypeStruct((n_layers, b, heads, dk, dv), F32),
                   jax.ShapeDtypeStruct((n_layers, b, heads, 1, dk), F32),
                   jax.ShapeDtypeStruct((n_layers, b, heads, 1, LANES), F32)],
        input_output_aliases={n_in + i: 1 + i for i in range(len(prev))},
        compiler_params=_cparams(("parallel", "arbitrary")),
        name="mlstm_cell",
    )(qkv.reshape(b, seq_len, -1), og.reshape(b, seq_len, -1), gates.reshape(b, seq_len, LANES),
      bg, g_head.reshape(1, heads * dv).astype(F32), c0, n4, m4, *prev)
    return y.reshape(t, heads * dv), (c_all, n_all, m_all)


def _lambda(lam_ref, lam_init):
    lp = lam_ref[...]
    a = jnp.sum(lp[0:1] * lp[1:2], axis=-1, keepdims=True)
    b = jnp.sum(lp[2:3] * lp[3:4], axis=-1, keepdims=True)
    return jnp.exp(a) - jnp.exp(b) + lam_init


def _sub_norm(o, gsub_ref, lam_init):
    y = o * lax.rsqrt(jnp.mean(o * o, axis=-1, keepdims=True) + EPS) * gsub_ref[...]
    return y * (1.0 - lam_init)


def _flash_kernel(qi_ref, ki_ref, qt_ref, k_ref, vt_ref, bound_ref, lam_ref, gs_ref, o_ref,
                  qm_scr, m_scr, acc_scr, *, dh, tq, tk, lam_init, online):
    pair = pl.program_id(2)
    qi = qi_ref[pair]
    ki = ki_ref[pair]
    dv = LANES
    ext = 2 * SUBLANES

    @pl.when(ki == 0)
    def _():
        q = qt_ref[0] * (dh ** -0.5)
        row = lax.broadcasted_iota(jnp.int32, q.shape, 0)
        qm_scr[0] = jnp.where(row < dh, q, jnp.zeros_like(q))
        qm_scr[1] = jnp.where(row >= dh, q, jnp.zeros_like(q))
        m_scr[...] = jnp.full_like(m_scr, NEG_INF)
        acc_scr[...] = jnp.zeros_like(acc_scr)

    def absorb(masked):
        k = k_ref[...]
        ones_row = (lax.broadcasted_iota(jnp.int32, (ext, tk), 0) == 0).astype(BF)
        vx = jnp.concatenate([vt_ref[0], ones_row], axis=0)
        if masked and tq == tk:
            subs = [(0, tq // 2, 0, tk // 2), (tq // 2, tq, 0, tk)]
        else:
            subs = [(0, tq, 0, tk)]
        for q0, q1, k0, k1 in subs:
            if masked:
                key = ki * tk + k0 + lax.broadcasted_iota(jnp.int32, (k1 - k0, q1 - q0), 0)
                qry = qi * tq + q0 + lax.broadcasted_iota(jnp.int32, (k1 - k0, q1 - q0), 1)
                keep = qry >= key
            for c in range(2):
                s = _dot(k[k0:k1], qm_scr[c, :, q0:q1])
                if masked:
                    s = jnp.where(keep, s, NEG_INF)
                if online:
                    m_prev = m_scr[c, :, q0:q1]
                    m_new = jnp.maximum(m_prev, jnp.max(s, axis=0, keepdims=True))
                    p = jnp.exp(s - m_new)
                    acc_scr[c, :, q0:q1] = (jnp.exp(m_prev - m_new) * acc_scr[c, :, q0:q1]
                                            + _dot(vx[:, k0:k1], p.astype(BF)))
                    m_scr[c, :, q0:q1] = m_new
                else:
                    p = jnp.exp(s - bound_ref[...])
                    acc_scr[c, :, q0:q1] += _dot(vx[:, k0:k1], p.astype(BF))

    needs_mask = (ki + 1) * tk - 1 > qi * tq

    @pl.when(needs_mask)
    def _():
        absorb(True)

    @pl.when(jnp.logical_not(needs_mask))
    def _():
        absorb(False)

    @pl.when(ki == (qi * tq + tq - 1) // tk)
    def _():
        lam = _lambda(lam_ref, lam_init)
        a0 = acc_scr[0]
        a1 = acc_scr[1]
        o = a0[:dv] / a0[dv:dv + 1] - lam * (a1[:dv] / a1[dv:dv + 1])
        y = o * lax.rsqrt(jnp.mean(o * o, axis=0, keepdims=True) + EPS) * gs_ref[...]
        o_ref[...] = (y * (1.0 - lam_init)).T.astype(o_ref.dtype)


SAFE_SCORE_BOUND = 30.0


def flash_diff_attention(qt, k, vt, score_bound, lam_p, g_sub, heads, lam_init, tq, tk):
    t = k.shape[0]
    n_seq, _, seq = qt.shape
    dv = vt.shape[1] // heads
    dh = k.shape[1] // (2 * heads)
    assert 2 * dh == LANES and dv == LANES
    nq, nk = seq // tq, seq // tk
    pairs = [(i, j) for i in range(nq) for j in range((i * tq + tq - 1) // tk + 1)]
    qi_tab = jnp.asarray([p[0] for p in pairs], jnp.int32)
    ki_tab = jnp.asarray([p[1] for p in pairs], jnp.int32)

    def call(online):
        grid_spec = pltpu.PrefetchScalarGridSpec(
            num_scalar_prefetch=2,
            grid=(n_seq, heads, len(pairs)),
            in_specs=[pl.BlockSpec((1, LANES, tq), lambda b, h, p, qi, ki: (b, h, qi[p])),
                      pl.BlockSpec((tk, LANES), lambda b, h, p, qi, ki: (b * nk + ki[p], h)),
                      pl.BlockSpec((1, LANES, tk), lambda b, h, p, qi, ki: (b, h, ki[p])),
                      pl.BlockSpec((1, 1), lambda b, h, p, qi, ki: (0, 0)),
                      pl.BlockSpec(lam_p.shape, lambda b, h, p, qi, ki: (0, 0)),
                      pl.BlockSpec((dv, 1), lambda b, h, p, qi, ki: (0, 0))],
            out_specs=pl.BlockSpec((tq, LANES), lambda b, h, p, qi, ki: (b * nq + qi[p], h)),
            scratch_shapes=[pltpu.VMEM((2, LANES, tq), BF),
                            pltpu.VMEM((2, 1, tq), F32),
                            pltpu.VMEM((2, dv + 2 * SUBLANES, tq), F32)],
        )
        return pl.pallas_call(
            functools.partial(_flash_kernel, dh=dh, tq=tq, tk=tk, lam_init=lam_init, online=online),
            grid_spec=grid_spec,
            out_shape=jax.ShapeDtypeStruct((t, heads * dv), BF),
            compiler_params=_cparams(("parallel", "parallel", "arbitrary")),
            name="flash_online" if online else "flash_bounded",
        )(qi_tab, ki_tab, qt, k, vt, score_bound.reshape(1, 1).astype(F32), lam_p.astype(F32),
          g_sub.reshape(dv, 1).astype(F32))

    return lax.cond(score_bound < SAFE_SCORE_BOUND, lambda: call(False), lambda: call(True))


def _paged_kernel(pt_ref, q_ref, kn_ref, vn_ref, lam_ref, gs_ref, *rest,
                  heads, dh, n_new, pages, page, lam_init):
    del pt_ref
    kt_refs = rest[:pages]
    v_refs = rest[pages:2 * pages]
    o_ref, s_scr = rest[2 * pages:]
    width = heads * 2 * dh
    nrow = 2 * heads * n_new
    past = pages * page

    q = q_ref[...] * (dh ** -0.5)
    qt = jnp.concatenate([q] * (2 * heads), axis=0)
    r = lax.broadcasted_iota(jnp.int32, (nrow, width), 0)
    cblk = lax.broadcasted_iota(jnp.int32, (nrow, width), 1) // dh
    qbd = jnp.where(cblk == r // n_new, qt, 0.0).astype(BF)

    pad = jnp.zeros((page - n_new, width), F32)
    kn = jnp.concatenate([kn_ref[...], pad], axis=0).astype(BF)
    vn = jnp.concatenate([vn_ref[...], pad], axis=0).astype(BF)

    def softmax_part(page_ids, with_new):
        c0 = page_ids[0] * page
        c1 = (page_ids[-1] + 1) * page
        for p in page_ids:
            s_scr[:, p * page:(p + 1) * page] = _dot(qbd, kt_refs[p][0, 0].astype(BF))
        if with_new:
            s_new = _nt_dot(qbd, kn)
            tok = lax.broadcasted_iota(jnp.int32, s_new.shape, 0) % n_new
            key = lax.broadcasted_iota(jnp.int32, s_new.shape, 1)
            s_scr[:, past:] = jnp.where(key <= tok, s_new, NEG_INF)
            c1 = past + page
        s = s_scr[:, c0:c1]
        m = jnp.max(s, axis=-1, keepdims=True)
        pr = jnp.exp(s - m)
        l = jnp.sum(pr, axis=-1, keepdims=True)
        s_scr[:, c0:c1] = pr
        acc = _dot(s_scr[:, past:].astype(BF), vn) if with_new else jnp.zeros((nrow, width), F32)
        for p in page_ids:
            v_wide = jnp.concatenate(
                [v_refs[p][0, 0, pl.ds(h, page, stride=heads), :] for h in range(heads)], axis=1)
            acc += _dot(s_scr[:, p * page:(p + 1) * page].astype(BF), v_wide.astype(BF))
        return acc, l, m

    half = max(pages // 2, 1)
    acc, l, m = softmax_part(list(range(half, pages)), True) if half < pages else (None, None, None)
    acc_a, l_a, m_a = softmax_part(list(range(half)), half == pages)
    if acc is None:
        acc, l = acc_a, l_a
    else:
        m_all = jnp.maximum(m, m_a)
        w, w_a = jnp.exp(m - m_all), jnp.exp(m_a - m_all)
        acc = w * acc + w_a * acc_a
        l = w * l + w_a * l_a

    lam = _lambda(lam_ref, lam_init)
    acc = acc * (1.0 / l)
    for h in range(heads):
        r0 = h * 2 * n_new
        cols = slice(h * 2 * dh, (h + 1) * 2 * dh)
        o = acc[r0:r0 + n_new, cols] - lam * acc[r0 + n_new:r0 + 2 * n_new, cols]
        o_ref[:, cols] = _sub_norm(o, gs_ref, lam_init)


def paged_diff_attention(q, k_new, v_new, cache_kt, cache_v, layer, page_table, lam_p, g_sub,
                         heads, lam_init):
    t, width = q.shape
    b, n_pages = page_table.shape
    n_new = t // b
    dh = width // (2 * heads)
    page = cache_kt.shape[3]
    assert n_new == SUBLANES and 2 * dh == LANES and page == LANES
    nrow = 2 * heads * n_new
    tok = pl.BlockSpec((n_new, width), lambda i, pt: (i, 0))

    def page_spec(p, shape):
        return pl.BlockSpec((1, 1) + shape, lambda i, pt: (layer, pt[i, p], 0, 0))

    grid_spec = pltpu.PrefetchScalarGridSpec(
        num_scalar_prefetch=1,
        grid=(b,),
        in_specs=[tok, tok, tok,
                  pl.BlockSpec(lam_p.shape, lambda i, pt: (0, 0)),
                  pl.BlockSpec((1, 2 * dh), lambda i, pt: (0, 0))]
                 + [page_spec(p, (width, page)) for p in range(n_pages)]
                 + [page_spec(p, (page * heads, 2 * dh)) for p in range(n_pages)],
        out_specs=tok,
        scratch_shapes=[pltpu.VMEM((nrow, (n_pages + 1) * page), F32)],
    )
    return pl.pallas_call(
        functools.partial(_paged_kernel, heads=heads, dh=dh, n_new=n_new, pages=n_pages,
                          page=page, lam_init=lam_init),
        grid_spec=grid_spec,
        out_shape=jax.ShapeDtypeStruct((t, width), F32),
        compiler_params=_cparams(("parallel",)),
        name="paged_diff_attention",
    )(page_table, q, k_new, v_new, lam_p.astype(F32), g_sub.reshape(1, 2 * dh).astype(F32),
      *([cache_kt] * n_pages), *([cache_v] * n_pages))


def _trunk(x, rows_per_seq, mods, state, paged, weights, cfg):
    (g_norm_mix, g_norm_ffn, w_in_m, b_gate_m, g_head_m, w_out_m,
     w_in_d, g_q_d, g_k_d, lam_d, g_sub_d, w_out_d, w_gu_f, w_down_f,
     w_router, b_router, w_gu_e, w_down_e) = weights
    t, d = x.shape
    n_seq = t // rows_per_seq
    depth = g_norm_mix.shape[0]
    tm = cfg["tm"]
    tn_proj, tn_seg, tn_up = 1024, 256, 1408
    heads_m = b_gate_m.shape[1] // 2
    dk, dv = state[0].shape[3], state[0].shape[4]
    qk_w = heads_m * dk
    heads_a, dh = cfg["heads_a"], cfg["dh"]
    new_state, new_kv = state[0].shape[0], []
    kt_stack = None
    for i in range(depth):
        j = i // 2
        mod = mods[i]
        if i % 2 == 0:
            col_scale = jnp.concatenate([jnp.ones((qk_w,), F32), jnp.full((qk_w,), dk ** -0.5, F32),
                                         jnp.ones((heads_m * dv,), F32)]).reshape(1, -1)
            small = rows_per_seq % M_CHUNK != 0
            (qkv,) = norm_proj(x, rows_per_seq, mod, 1, 0, g_norm_mix[i], w_in_m[j], 0,
                               2 * qk_w + heads_m * dv, [F32 if small else BF], tm, tn_proj,
                               col_scale=col_scale, name="mlstm_qkv")
            (og,) = norm_proj(x, rows_per_seq, mod, 1, 0, g_norm_mix[i], w_in_m[j],
                              2 * qk_w + heads_m * dv, heads_m * dv, [F32], tm, tn_proj, name="mlstm_o")
            w_gate = jnp.pad(w_in_m[j][:, 2 * qk_w + 2 * heads_m * dv:], ((0, 0), (0, LANES - 2 * heads_m)))
            (gates,) = norm_proj(x, rows_per_seq, mod, 1, 0, g_norm_mix[i], w_gate, 0, LANES,
                                 [F32], tm, LANES, name="mlstm_gates")
            y, new_state = mlstm_cell(qkv, og, gates, b_gate_m[j], g_head_m[j], *state,
                                      rows_per_seq, cfg["mlstm_seqs"], j, new_state)
            w_out = w_out_m
        else:
            lam_init = 0.8 - 0.6 * math.exp(-0.3 * i)
            prompt = paged is None
            (q,) = norm_proj(x, rows_per_seq, mod, 1, 0, g_norm_mix[i], w_in_d[j], 0, d,
                             [BF if prompt else F32], tm, tn_seg, seg_gain=g_q_d[j],
                             transposed=(prompt,), name="attn_q")
            k32, kbf = norm_proj(x, rows_per_seq, mod, 1, 0, g_norm_mix[i], w_in_d[j], d, d,
                                 [F32, BF], tm, tn_seg, seg_gain=g_k_d[j], transposed=(prompt, False),
                                 stacked=(j, depth // 2, kt_stack), name="attn_k")
            v32, vbf = norm_proj(x, rows_per_seq, mod, 1, 0, g_norm_mix[i], w_in_d[j], 2 * d, d,
                                 [F32, BF], tm, tn_proj, transposed=(False, prompt), name="attn_v")
            if paged is None:
                bound = (dh ** 0.5) * jnp.max(jnp.abs(g_q_d[j])) * jnp.max(jnp.abs(g_k_d[j]))
                y = flash_diff_attention(q, kbf, vbf, bound, lam_d[j], g_sub_d[j], heads_a,
                                         lam_init, cfg["tq"], cfg["tk"])
                kt_stack = k32
            else:
                cache_kt, cache_v, page_table = paged
                y = paged_diff_attention(q, k32, v32, cache_kt, cache_v, j, page_table, lam_d[j],
                                         g_sub_d[j], heads_a, lam_init)
            new_kv.append((k32, v32))
            w_out = w_out_d
        x = down_residual(y, w_out, j, x, rows_per_seq, mod, 2, tm, tn_proj, name="mixer_out")
        if i % 2 == 0:
            act = swiglu_up(x, rows_per_seq, mod, 4, 3, g_norm_ffn[i], w_gu_f, j, tm, tn_up)
            x = down_residual(act, w_down_f, j, x, rows_per_seq, mod, 5, tm, tn_proj, name="ffn_down")
        else:
            n_exp = w_router.shape[2]
            h, dg, rank, counts = router_gates(x, rows_per_seq, mod, 4, 3, g_norm_ffn[i],
                                               w_router[j], b_router[j], tm)
            counts = counts[:, 0, :n_exp].astype(jnp.int32).reshape(-1)
            x = moe_ffn(h, dg, rank, counts, w_gu_e, w_down_e, j, x, rows_per_seq, mod, 5, tm,
                        cfg["moe_ff_chunks"], min(cfg["moe_block_rows"], tm // 2))
    return x, new_state, new_kv, kt_stack


def kernel(x_prompt, x_sample, c_prompt, c_sample, state_C, state_n, state_m, cache_k, cache_v, page_table, w_ada, b_ada, g_norm_mix, g_norm_ffn, w_in_m, b_gate_m, g_head_m, w_out_m, w_in_d, g_q_d, g_k_d, lam_d, g_sub_d, w_out_d, w_gu_f, w_down_f, w_router, b_router, w_gu_e, w_down_e):
    bp, seq, d = x_prompt.shape
    bs, dec_seq, _ = x_sample.shape
    depth = w_ada.shape[0]
    n_ml, _, heads_m, dk, dv = state_C.shape
    n_diff, n_pool, page, heads_a, _, dh = cache_k.shape
    weights = (g_norm_mix, g_norm_ffn, w_in_m.astype(BF), b_gate_m, g_head_m, w_out_m.astype(BF),
               w_in_d.astype(BF), g_q_d, g_k_d, lam_d, g_sub_d, w_out_d.astype(BF),
               w_gu_f.astype(BF), w_down_f.astype(BF),
               w_router, b_router, w_gu_e.astype(BF), w_down_e.astype(BF))

    n_c = bp + bs
    n_c_pad = -(-n_c // SUBLANES) * SUBLANES
    c_all = jnp.pad(jnp.concatenate([c_prompt, c_sample], axis=0), ((0, n_c_pad - n_c), (0, 0)))
    mod_all = ada_mod(c_all, w_ada, b_ada, tn=min(6 * d, 1536))
    mods_p = [mod_all[i, :bp].reshape(bp, 1, 6 * d) for i in range(depth)]
    mods_s = [mod_all[i, bp:n_c].reshape(bs, 1, 6 * d) for i in range(depth)]

    moe = dict(moe_ff_chunks=2, moe_block_rows=128)
    cfg_p = dict(tm=min(1024, seq), tq=min(1024, seq), tk=min(1024, seq), heads_a=heads_a, dh=dh,
                 mlstm_seqs=min(2, bp), **moe)
    cfg_s = dict(tm=min(1024, bs * dec_seq), heads_a=heads_a, dh=dh, mlstm_seqs=min(8, bs), **moe)

    zeros = (jnp.zeros((n_ml, bp, heads_m, dk, dv), F32), jnp.zeros((n_ml, bp, heads_m, dk), F32),
             jnp.zeros((n_ml, bp, heads_m), F32))
    y_p, st_p, kv_p, kt_p = _trunk(x_prompt.reshape(bp * seq, d), seq, mods_p, zeros, None, weights, cfg_p)
    paged = (jnp.transpose(cache_k, (0, 1, 3, 4, 5, 2)).reshape(n_diff, n_pool, heads_a * 2 * dh, page),
             cache_v.reshape(n_diff, n_pool, page * heads_a, 2 * dh), page_table)
    y_s, st_s, kv_s, _ = _trunk(x_sample.reshape(bs * dec_seq, d), dec_seq, mods_s,
                             (state_C, state_n, state_m), paged, weights, cfg_s)

    def stack(items, idx, shape):
        return jnp.stack([it[idx] for it in items]).reshape(shape)

    return (y_p.reshape(bp, seq, d), y_s.reshape(bs, dec_seq, d),
            st_p[0], st_p[1].reshape(n_ml, bp, heads_m, dk), st_p[2][:, :, :, 0, 0],
            st_s[0], st_s[1].reshape(n_ml, bs, heads_m, dk), st_s[2][:, :, :, 0, 0],
            jnp.transpose(kt_p.reshape(n_diff, bp, heads_a, 2, dh, seq), (0, 1, 5, 2, 3, 4)),
            stack(kv_p, 1, (n_diff, bp, seq, heads_a, 2 * dh)),
            stack(kv_s, 0, (n_diff, bs, dec_seq, heads_a, 2, dh)),
            stack(kv_s, 1, (n_diff, bs, dec_seq, heads_a, 2 * dh)))
```

```python
import functools
import math

import jax
import jax.numpy as jnp
from jax import lax
from jax.experimental import pallas as pl
from jax.experimental.pallas import tpu as pltpu

BF = jnp.bfloat16
F32 = jnp.float32
EPS = 1e-6
M_CHUNK = 128
LANES = 128
SUBLANES = 8
VMEM_LIMIT = 56 * 1024 * 1024
NEG_INF = float("-inf")


def _cparams(sem):
    return pltpu.CompilerParams(dimension_semantics=sem, vmem_limit_bytes=VMEM_LIMIT)


def _tile(n, pref):
    if n <= pref:
        return n
    t = pref - pref % LANES
    while n % t:
        t -= LANES
    return t


def _nt_dot(a, b, **kw):
    return lax.dot_general(a, b, (((1,), (1,)), ((), ())), preferred_element_type=F32, **kw)


def _tn_dot(a, b, **kw):
    return lax.dot_general(a, b, (((0,), (0,)), ((), ())), preferred_element_type=F32, **kw)


def _dot(a, b, **kw):
    return jnp.dot(a, b, preferred_element_type=F32, **kw)


def _ada_kernel(c_ref, w_ref, b_ref, o_ref):
    c = c_ref[...]
    a = (c * jax.nn.sigmoid(c)).astype(BF)
    o_ref[0] = _dot(a, w_ref[0].astype(BF)) + b_ref[0]


def ada_mod(c_all, w_ada, b_ada, tn):
    depth, d, n = w_ada.shape
    bp = c_all.shape[0]
    return pl.pallas_call(
        _ada_kernel,
        grid=(depth, n // tn),
        in_specs=[pl.BlockSpec((bp, d), lambda i, j: (0, 0)),
                  pl.BlockSpec((1, d, tn), lambda i, j: (i, 0, j)),
                  pl.BlockSpec((1, 1, tn), lambda i, j: (i, 0, j))],
        out_specs=pl.BlockSpec((1, bp, tn), lambda i, j: (i, 0, j)),
        out_shape=jax.ShapeDtypeStruct((depth, bp, n), F32),
        compiler_params=_cparams(("arbitrary", "arbitrary")),
        name="ada_mod",
    )(c_all, w_ada, b_ada.reshape(depth, 1, n))


def _mod_spec(rows_per_seq, tm, width, col_block):
    if rows_per_seq >= tm:
        npb = rows_per_seq // tm
        return pl.BlockSpec((1, 1, width), lambda *g: (g[0] // npb, 0, col_block(*g)))
    gb = tm // rows_per_seq
    return pl.BlockSpec((gb, 1, width), lambda *g: (g[0], 0, col_block(*g)))


def _rows(m_ref, tm):
    m = m_ref[...]
    gb, _, w = m.shape
    if gb == 1:
        return m[0]
    return jnp.broadcast_to(m, (gb, tm // gb, w)).reshape(tm, w)


def _modulated(x_ref, g_ref, sc_ref, sh_ref):
    x = x_ref[...]
    tm = x.shape[0]
    y = x * lax.rsqrt(jnp.mean(x * x, axis=-1, keepdims=True) + EPS) * g_ref[...]
    return y * (1.0 + _rows(sc_ref, tm)) + _rows(sh_ref, tm)


def _store_tiles(outs, y, transposed, first_slot):
    yt = y.T if any(transposed) else None
    for idx, (o, tr) in enumerate(zip(outs, transposed)):
        if idx == 0 and first_slot is not None:
            for slot in range(o.shape[0]):
                o[slot, 0] = yt.astype(o.dtype) if slot == first_slot else jnp.zeros(yt.shape, o.dtype)
        elif tr:
            o[(0,) * (len(o.shape) - 2)] = yt.astype(o.dtype)
        else:
            o[...] = y.astype(o.dtype)


def _proj_kernel(x_ref, g_ref, sc_ref, sh_ref, w_ref, cs_ref, *rest, transposed, n_alias,
                 first_slot):
    outs, h_scr = rest[n_alias:-1], rest[-1]

    @pl.when(pl.program_id(1) == 0)
    def _():
        h_scr[...] = _modulated(x_ref, g_ref, sc_ref, sh_ref).astype(BF)

    acc = _dot(h_scr[...], w_ref[...].astype(BF)) * cs_ref[...]
    _store_tiles(outs, acc, transposed, first_slot)


def _segnorm_kernel(x_ref, g_ref, sc_ref, sh_ref, w_ref, gv_ref, seg_ref, *rest, seg, transposed,
                    n_alias, first_slot):
    outs, h_scr = rest[n_alias:-1], rest[-1]

    @pl.when(pl.program_id(1) == 0)
    def _():
        h_scr[...] = _modulated(x_ref, g_ref, sc_ref, sh_ref).astype(BF)

    acc = _dot(h_scr[...], w_ref[...].astype(BF))
    ssq = _dot((acc * acc).astype(BF), seg_ref[...])
    y = acc * lax.rsqrt(ssq * (1.0 / seg) + EPS) * gv_ref[...]
    _store_tiles(outs, y, transposed, first_slot)


def norm_proj(x, rows_per_seq, mod, sc_chunk, sh_chunk, g, w, col_off, n, out_dtypes, tm, tn,
              col_scale=None, seg_gain=None, transposed=False, stacked=None, name="norm_proj"):
    t, d = x.shape
    prev = []
    if transposed is False:
        transposed = (False,) * len(out_dtypes)
    tn = _tile(math.gcd(n, col_off) if col_off else n, tn)
    x_spec = pl.BlockSpec((tm, d), lambda i, j: (i, 0))
    g_spec = pl.BlockSpec((1, d), lambda i, j: (0, 0))
    sc_spec = _mod_spec(rows_per_seq, tm, d, lambda i, j: sc_chunk)
    sh_spec = _mod_spec(rows_per_seq, tm, d, lambda i, j: sh_chunk)
    off = col_off // tn
    w_spec = pl.BlockSpec((d, tn), lambda i, j: (0, off + j))
    npb = max(rows_per_seq // tm, 1)
    out_specs, out_shape = [], []
    for dt, tr in zip(out_dtypes, transposed):
        if tr:
            out_specs.append(pl.BlockSpec((1, tn, tm), lambda i, j: (i // npb, j, i % npb)))
            out_shape.append(jax.ShapeDtypeStruct((t // rows_per_seq, n, rows_per_seq), dt))
        else:
            out_specs.append(pl.BlockSpec((tm, tn), lambda i, j: (i, j)))
            out_shape.append(jax.ShapeDtypeStruct((t, n), dt))
    first_slot = None
    if stacked is not None and transposed[0]:
        layer, n_layers, previous = stacked
        if previous is None:
            first_slot = layer
            out_specs[0] = pl.BlockSpec((n_layers, 1, tn, tm), lambda i, j: (0, i // npb, j, i % npb))
        else:
            out_specs[0] = pl.BlockSpec((1, 1, tn, tm), lambda i, j: (layer, i // npb, j, i % npb))
            prev = [previous]
        out_shape[0] = jax.ShapeDtypeStruct((n_layers,) + out_shape[0].shape, out_dtypes[0])
    if seg_gain is None:
        if col_scale is None:
            col_scale = jnp.ones((1, n), F32)
        kern = functools.partial(_proj_kernel, transposed=transposed, n_alias=len(prev),
                                 first_slot=first_slot)
        extra = [col_scale]
        extra_specs = [pl.BlockSpec((1, tn), lambda i, j: (0, j))]
    else:
        seg = seg_gain.shape[0]
        kern = functools.partial(_segnorm_kernel, seg=seg, transposed=transposed, n_alias=len(prev),
                                 first_slot=first_slot)
        gv = jnp.tile(seg_gain.astype(F32), tn // seg).reshape(1, tn)
        ids = jnp.arange(tn) // seg
        seg_mat = (ids[:, None] == ids[None, :]).astype(BF)
        extra = [gv, seg_mat]
        extra_specs = [pl.BlockSpec((1, tn), lambda i, j: (0, 0)),
                       pl.BlockSpec((tn, tn), lambda i, j: (0, 0))]
    res = pl.pallas_call(
        kern,
        grid=(t // tm, n // tn),
        in_specs=[x_spec, g_spec, sc_spec, sh_spec, w_spec] + extra_specs
                 + [pl.BlockSpec(memory_space=pl.ANY)] * len(prev),
        out_specs=out_specs,
        out_shape=out_shape,
        scratch_shapes=[pltpu.VMEM((tm, d), BF)],
        input_output_aliases={5 + len(extra) + i: i for i in range(len(prev))},
        compiler_params=_cparams(("parallel", "arbitrary")),
        name=name,
    )(x, g.reshape(1, d), mod, mod, w, *extra, *prev)
    return res


def _router_kernel(x_ref, g_ref, sc_ref, sh_ref, w_ref, b_ref, tri_ref,
                   h_ref, o_ref, pos_ref, cnt_ref, *, n_experts):
    h = _modulated(x_ref, g_ref, sc_ref, sh_ref)
    h_ref[...] = h.astype(h_ref.dtype)
    logits = _dot(h, w_ref[...], precision=lax.Precision.HIGHEST) + b_ref[...]
    lane = lax.broadcasted_iota(jnp.int32, logits.shape, 1).astype(F32)
    big = float(LANES)
    l1 = jnp.where(lane < n_experts, logits, NEG_INF)
    m1 = jnp.max(l1, axis=-1, keepdims=True)
    i1 = jnp.min(jnp.where(l1 == m1, lane, big), axis=-1, keepdims=True)
    l2 = jnp.where(lane == i1, NEG_INF, l1)
    m2 = jnp.max(l2, axis=-1, keepdims=True)
    i2 = jnp.min(jnp.where(l2 == m2, lane, big), axis=-1, keepdims=True)
    e2 = jnp.exp(m2 - m1)
    w1 = 1.0 / (1.0 + e2)
    w2 = e2 / (1.0 + e2)
    gate = jnp.where(lane == i1, w1, 0.0) + jnp.where(lane == i2, w2, 0.0)
    o_ref[...] = gate
    routed = jnp.where(gate > 0.0, 1.0, 0.0)
    pos_ref[...] = _dot(tri_ref[...], routed.astype(BF))
    cnt_ref[0] = jnp.sum(routed, axis=0, keepdims=True)


def router_gates(x, rows_per_seq, mod, sc_chunk, sh_chunk, g, w_router, b_router, tm):
    t, d = x.shape
    e = w_router.shape[1]
    w_pad = jnp.pad(w_router, ((0, 0), (0, LANES - e)))
    b_pad = jnp.pad(b_router, (0, LANES - e)).reshape(1, LANES)
    idx = jnp.arange(tm)
    strict_lower = (idx[:, None] > idx[None, :]).astype(BF)
    tok = pl.BlockSpec((tm, LANES), lambda i: (i, 0))
    return pl.pallas_call(
        functools.partial(_router_kernel, n_experts=e),
        grid=(t // tm,),
        in_specs=[pl.BlockSpec((tm, d), lambda i: (i, 0)),
                  pl.BlockSpec((1, d), lambda i: (0, 0)),
                  _mod_spec(rows_per_seq, tm, d, lambda i: sc_chunk),
                  _mod_spec(rows_per_seq, tm, d, lambda i: sh_chunk),
                  pl.BlockSpec((d, LANES), lambda i: (0, 0)),
                  pl.BlockSpec((1, LANES), lambda i: (0, 0)),
                  pl.BlockSpec((tm, tm), lambda i: (0, 0))],
        out_specs=[pl.BlockSpec((tm, d), lambda i: (i, 0)), tok, tok,
                   pl.BlockSpec((1, 1, LANES), lambda i: (i, 0, 0))],
        out_shape=[jax.ShapeDtypeStruct((t, d), BF),
                   jax.ShapeDtypeStruct((t, LANES), F32),
                   jax.ShapeDtypeStruct((t, LANES), F32),
                   jax.ShapeDtypeStruct((t // tm, 1, LANES), F32)],
        compiler_params=_cparams(("parallel",)),
        name="router",
    )(x, g.reshape(1, d), mod, mod, w_pad, b_pad, strict_lower)


def _moe_kernel(cnt_ref, h_ref, dg_ref, pos_ref, wg_ref, wu_ref, wd_ref, x_ref, gt_ref, o_ref,
                xs_scr, y_scr, *, n_experts, br):
    ti, e, fc = pl.program_id(0), pl.program_id(1), pl.program_id(2)
    last_fc = pl.num_programs(2) - 1
    tm = h_ref.shape[0]

    @pl.when((e == 0) & (fc == 0))
    def _():
        o_ref[...] = jnp.zeros_like(o_ref)

    sel = lax.broadcasted_iota(jnp.int32, (tm, LANES), 1) == e
    ge = jnp.sum(jnp.where(sel, dg_ref[...], 0.0), axis=-1, keepdims=True)
    pos = jnp.sum(jnp.where(sel, pos_ref[...], 0.0), axis=-1, keepdims=True)
    routed = ge > 0.0
    n_blocks = (cnt_ref[ti * n_experts + e] + br - 1) // br

    def block(r0, rows_in_block):
        rows = pl.ds(r0, rows_in_block)
        slot = (r0 + lax.broadcasted_iota(jnp.int32, (tm, rows_in_block), 1)).astype(F32)
        onehot = jnp.where((pos == slot) & routed, 1.0, 0.0).astype(BF)

        @pl.when(fc == 0)
        def _():
            xs_scr[rows, :] = _tn_dot(onehot, h_ref[...]).astype(BF)

        xs = xs_scr[rows, :]
        a = _dot(xs, wg_ref[0, 0])
        u = _dot(xs, wu_ref[0, 0])
        y = _dot((a * jax.nn.sigmoid(a) * u).astype(BF), wd_ref[0, 0])

        @pl.when(fc == 0)
        def _():
            y_scr[rows, :] = y

        @pl.when(fc > 0)
        def _():
            y_scr[rows, :] += y

        @pl.when(fc == last_fc)
        def _():
            o_ref[...] += ge * _dot(onehot, y_scr[rows, :].astype(BF))

    def pair(i, carry):
        block(pl.multiple_of(i * 2 * br, 2 * br), 2 * br)
        return carry

    lax.fori_loop(0, n_blocks // 2, pair, 0)

    @pl.when(n_blocks % 2 == 1)
    def _():
        block(pl.multiple_of((n_blocks - 1) * br, br), br)

    @pl.when((e == n_experts - 1) & (fc == last_fc))
    def _():
        o_ref[...] = x_ref[...] + _rows(gt_ref, tm) * o_ref[...]


def moe_ffn(h, dense_gate, rank, counts, w_gu, w_down, layer, x, rows_per_seq, mod, gt_chunk, tm,
            ff_chunks, block_rows):
    t, d = x.shape
    _, e, _, f2 = w_gu.shape
    f = f2 // 2
    fcw = f // ff_chunks
    tok = pl.BlockSpec((tm, LANES), lambda i, k, c, cnt: (i, 0))
    grid_spec = pltpu.PrefetchScalarGridSpec(
        num_scalar_prefetch=1,
        grid=(t // tm, e, ff_chunks),
        in_specs=[pl.BlockSpec((tm, d), lambda i, k, c, cnt: (i, 0)), tok, tok,
                  pl.BlockSpec((1, 1, d, fcw), lambda i, k, c, cnt: (layer, k, 0, c)),
                  pl.BlockSpec((1, 1, d, fcw), lambda i, k, c, cnt: (layer, k, 0, ff_chunks + c)),
                  pl.BlockSpec((1, 1, fcw, d), lambda i, k, c, cnt: (layer, k, c, 0)),
                  pl.BlockSpec((tm, d), lambda i, k, c, cnt: (i, 0)),
                  _mod_spec(rows_per_seq, tm, d, lambda i, k, c, cnt: gt_chunk)],
        out_specs=pl.BlockSpec((tm, d), lambda i, k, c, cnt: (i, 0)),
        scratch_shapes=[pltpu.VMEM((tm, d), BF), pltpu.VMEM((tm, d), F32)],
    )
    return pl.pallas_call(
        functools.partial(_moe_kernel, n_experts=e, br=block_rows),
        grid_spec=grid_spec,
        out_shape=jax.ShapeDtypeStruct((t, d), F32),
        compiler_params=_cparams(("parallel", "arbitrary", "arbitrary")),
        name="moe_ffn",
    )(counts, h, dense_gate, rank, w_gu, w_gu, w_down, x, mod)


def _swiglu_up_kernel(x_ref, g_ref, sc_ref, sh_ref, wg_ref, wu_ref, o_ref, h_scr):
    @pl.when(pl.program_id(1) == 0)
    def _():
        h_scr[...] = _modulated(x_ref, g_ref, sc_ref, sh_ref).astype(BF)

    h = h_scr[...]
    a = _dot(h, wg_ref[0].astype(BF))
    u = _dot(h, wu_ref[0].astype(BF))
    o_ref[...] = (a * jax.nn.sigmoid(a) * u).astype(o_ref.dtype)


def swiglu_up(x, rows_per_seq, mod, sc_chunk, sh_chunk, g, w_gu, layer, tm, tn):
    t, d = x.shape
    f = w_gu.shape[2] // 2
    tn = _tile(f, tn)
    nf = f // tn
    return pl.pallas_call(
        _swiglu_up_kernel,
        grid=(t // tm, nf),
        in_specs=[pl.BlockSpec((tm, d), lambda i, j: (i, 0)),
                  pl.BlockSpec((1, d), lambda i, j: (0, 0)),
                  _mod_spec(rows_per_seq, tm, d, lambda i, j: sc_chunk),
                  _mod_spec(rows_per_seq, tm, d, lambda i, j: sh_chunk),
                  pl.BlockSpec((1, d, tn), lambda i, j: (layer, 0, j)),
                  pl.BlockSpec((1, d, tn), lambda i, j: (layer, 0, nf + j))],
        out_specs=pl.BlockSpec((tm, tn), lambda i, j: (i, j)),
        out_shape=jax.ShapeDtypeStruct((t, f), BF),
        scratch_shapes=[pltpu.VMEM((tm, d), BF)],
        compiler_params=_cparams(("parallel", "arbitrary")),
        name="swiglu_up",
    )(x, g.reshape(1, d), mod, mod, w_gu, w_gu)


def _down_kernel(y_ref, w_ref, x_ref, gt_ref, o_ref):
    p = _dot(y_ref[...].astype(BF), w_ref[0].astype(BF))
    o_ref[...] = x_ref[...] + _rows(gt_ref, x_ref.shape[0]) * p


def down_residual(y, w, layer, x, rows_per_seq, mod, gt_chunk, tm, tn, name="down_residual"):
    t, k = y.shape
    d = x.shape[1]
    tn = _tile(d, tn)
    nd = d // tn
    return pl.pallas_call(
        _down_kernel,
        grid=(t // tm, nd),
        in_specs=[pl.BlockSpec((tm, k), lambda i, j: (i, 0)),
                  pl.BlockSpec((1, k, tn), lambda i, j: (layer, 0, j)),
                  pl.BlockSpec((tm, tn), lambda i, j: (i, j)),
                  _mod_spec(rows_per_seq, tm, tn, lambda i, j: gt_chunk * nd + j)],
        out_specs=pl.BlockSpec((tm, tn), lambda i, j: (i, j)),
        out_shape=jax.ShapeDtypeStruct((t, d), F32),
        compiler_params=_cparams(("parallel", "arbitrary")),
        name=name,
    )(y, w, x, mod)


def _log_sigmoid(x):
    return jnp.minimum(x, 0.0) - jnp.log1p(jnp.exp(-jnp.abs(x)))


def _mlstm_kernel(qkv_ref, o_ref, gt_ref, bg_ref, gh_ref, c0_ref, n0_ref, m0_ref, *rest,
                  heads, dk, dv, chunk, seqs, unroll, slot):
    y_ref, c_all, n_all, m_all = rest[-4:]
    c_ref, n_ref, m_ref = c_all.at[slot], n_all.at[slot], m_all.at[slot]
    L = chunk
    hi = lax.Precision.HIGHEST
    cd = BF

    @pl.when(pl.program_id(1) == 0)
    def _():
        for other in range(c_all.shape[0]):
            if other != slot:
                c_all[other] = jnp.zeros(c_all.shape[1:], F32)
                n_all[other] = jnp.zeros(n_all.shape[1:], F32)
                m_all[other] = jnp.zeros(m_all.shape[1:], F32)
        c_ref[...] = c0_ref[0]
        n_ref[...] = n0_ref[0]
        m_ref[...] = m0_ref[0]

    row = lax.broadcasted_iota(jnp.int32, (L, L), 0)
    col = lax.broadcasted_iota(jnp.int32, (L, L), 1)
    causal = row >= col
    tril = causal.astype(F32)
    sel = (lax.broadcasted_iota(jnp.int32, (SUBLANES, LANES), 0)
           == lax.broadcasted_iota(jnp.int32, (SUBLANES, LANES), 1)).astype(F32)

    def one_seq(bi, results):
        gates = gt_ref[bi] + bg_ref[...]
        lf = _log_sigmoid(gates)
        bcum = _dot(tril, lf, precision=hi)
        g_rows = _nt_dot(sel, gates, precision=hi)
        b_rows = _nt_dot(sel, bcum, precision=hi)
        for h in range(heads):
            q = qkv_ref[bi, :, h * dk:(h + 1) * dk]
            k = qkv_ref[bi, :, (heads + h) * dk:(heads + h + 1) * dk]
            v = qkv_ref[bi, :, 2 * heads * dk + h * dv:2 * heads * dk + (h + 1) * dv]
            qf, kf = q.astype(F32), k.astype(F32)
            qc, kc, vc = q.astype(cd), k.astype(cd), v.astype(cd)
            ig_col = gates[:, h:h + 1]
            ig_row = g_rows[h:h + 1, :]
            b_col = bcum[:, heads + h:heads + h + 1]
            b_row = b_rows[heads + h:heads + h + 1, :]
            b_end = bcum[L - 1:L, heads + h:heads + h + 1]
            c_old = c_ref[bi, h]
            n_old = n_ref[bi, h]
            m_old = m_ref[bi, h][:, :1]

            dlog = jnp.where(causal, b_col - b_row + ig_row, NEG_INF)
            gcar = b_col + m_old
            m_t = jnp.maximum(gcar, jnp.max(dlog, axis=-1, keepdims=True))
            s = _nt_dot(qc, kc) * jnp.exp(dlog - m_t)
            dec = jnp.exp(gcar - m_t)
            num = _dot(s.astype(cd), vc) + dec * _dot(qc, c_old.astype(cd))
            den = (jnp.sum(s, axis=-1, keepdims=True)
                   + dec * jnp.sum(qf * n_old, axis=-1, keepdims=True))
            hh = num / jnp.maximum(jnp.abs(den), jnp.exp(-m_t))

            a_row = b_end - b_row + ig_row
            a_col = b_end - b_col + ig_col
            m_new = jnp.maximum(b_end + m_old, jnp.max(a_row, axis=-1, keepdims=True))
            wk = jnp.exp(a_col - m_new)
            decay = jnp.exp(b_end + m_old - m_new)
            kw = kf * wk
            c_new = decay * c_old + _tn_dot(kw.astype(cd), vc)
            n_new = decay * n_old + jnp.sum(kw, axis=0, keepdims=True)

            hs = (hh * lax.rsqrt(jnp.mean(hh * hh, axis=-1, keepdims=True) + EPS)
                  * gh_ref[:, h * dv:(h + 1) * dv])
            og = o_ref[bi, :, h * dv:(h + 1) * dv]
            y = (hs * jax.nn.sigmoid(og)).astype(y_ref.dtype)
            results.append((bi, h, c_new, n_new, jnp.broadcast_to(m_new, (1, LANES)), y))

    def store(results):
        for bi, h, c_new, n_new, m_new, y in results:
            c_ref[bi, h] = c_new
            n_ref[bi, h] = n_new
            m_ref[bi, h] = m_new
            y_ref[bi, :, h * dv:(h + 1) * dv] = y

    if unroll:
        results = []
        for bi in range(seqs):
            one_seq(bi, results)
        store(results)
    else:
        def body(bi, carry):
            results = []
            one_seq(bi, results)
            store(results)
            return carry
        lax.fori_loop(0, seqs, body, 0)


def mlstm_cell(qkv, og, gates, b_gate, g_head, c0, n0, m0, seq_len, seqs_per_step, layer, stacked):
    t = qkv.shape[0]
    n_in_layers, b, heads, dk, dv = c0.shape
    chunk = M_CHUNK if seq_len % M_CHUNK == 0 else seq_len
    nc = seq_len // chunk
    bt = seqs_per_step
    n4 = n0.reshape(n_in_layers, b, heads, 1, dk)
    m4 = jnp.broadcast_to(m0.reshape(n_in_layers, b, heads, 1, 1), (n_in_layers, b, heads, 1, LANES))
    bg = jnp.pad(b_gate.astype(F32), (0, LANES - 2 * heads)).reshape(1, LANES)
    y_dtype = BF if chunk >= 16 else F32
    tok = lambda i, c: (i, c, 0)
    st = lambda i, c: (layer, i, 0, 0, 0)
    first = isinstance(stacked, int)
    n_layers = stacked if first else stacked[0].shape[0]
    prev = [] if first else list(stacked)
    slots = n_layers if first else 1
    st_out = lambda i, c: (0 if first else layer, i, 0, 0, 0)
    n_in = 8
    y, c_all, n_all, m_all = pl.pallas_call(
        functools.partial(_mlstm_kernel, heads=heads, dk=dk, dv=dv, chunk=chunk, seqs=bt,
                          unroll=bt <= 8, slot=layer if first else 0),
        grid=(b // bt, nc),
        in_specs=[pl.BlockSpec((bt, chunk, qkv.shape[1]), tok),
                  pl.BlockSpec((bt, chunk, heads * dv), tok),
                  pl.BlockSpec((bt, chunk, LANES), tok),
                  pl.BlockSpec((1, LANES), lambda i, c: (0, 0)),
                  pl.BlockSpec((1, heads * dv), lambda i, c: (0, 0)),
                  pl.BlockSpec((1, bt, heads, dk, dv), st),
                  pl.BlockSpec((1, bt, heads, 1, dk), st),
                  pl.BlockSpec((1, bt, heads, 1, LANES), st)]
                 + [pl.BlockSpec(memory_space=pl.ANY)] * len(prev),
        out_specs=[pl.BlockSpec((bt, chunk, heads * dv), tok),
                   pl.BlockSpec((slots, bt, heads, dk, dv), st_out),
                   pl.BlockSpec((slots, bt, heads, 1, dk), st_out),
                   pl.BlockSpec((slots, bt, heads, 1, LANES), st_out)],
        out_shape=[jax.ShapeDtypeStruct((b, seq_len, heads * dv), y_dtype),
                   jax.ShapeDtypeStruct((n_layers, b, heads, dk, dv), F32),
                   jax.ShapeDtypeStruct((n_layers, b, heads, 1, dk), F32),
                   jax.ShapeDtypeStruct((n_layers, b, heads, 1, LANES), F32)],
        input_output_aliases={n_in + i: 1 + i for i in range(len(prev))},
        compiler_params=_cparams(("parallel", "arbitrary")),
        name="mlstm_cell",
    )(qkv.reshape(b, seq_len, -1), og.reshape(b, seq_len, -1), gates.reshape(b, seq_len, LANES),
      bg, g_head.reshape(1, heads * dv).astype(F32), c0, n4, m4, *prev)
    return y.reshape(t, heads * dv), (c_all, n_all, m_all)


def _lambda(lam_ref, lam_init):
    lp = lam_ref[...]
    a = jnp.sum(lp[0:1] * lp[1:2], axis=-1, keepdims=True)
    b = jnp.sum(lp[2:3] * lp[3:4], axis=-1, keepdims=True)
    return jnp.exp(a) - jnp.exp(b) + lam_init


def _sub_norm(o, gsub_ref, lam_init):
    y = o * lax.rsqrt(jnp.mean(o * o, axis=-1, keepdims=True) + EPS) * gsub_ref[...]
    return y * (1.0 - lam_init)


def _flash_kernel(qi_ref, ki_ref, qt_ref, k_ref, vt_ref, bound_ref, lam_ref, gs_ref, o_ref,
                  qm_scr, m_scr, acc_scr, *, dh, tq, tk, lam_init, online):
    pair = pl.program_id(2)
    qi = qi_ref[pair]
    ki = ki_ref[pair]
    dv = LANES
    ext = 2 * SUBLANES

    @pl.when(ki == 0)
    def _():
        q = qt_ref[0] * (dh ** -0.5)
        row = lax.broadcasted_iota(jnp.int32, q.shape, 0)
        qm_scr[0] = jnp.where(row < dh, q, jnp.zeros_like(q))
        qm_scr[1] = jnp.where(row >= dh, q, jnp.zeros_like(q))
        m_scr[...] = jnp.full_like(m_scr, NEG_INF)
        acc_scr[...] = jnp.zeros_like(acc_scr)

    def absorb(masked):
        k = k_ref[...]
        ones_row = (lax.broadcasted_iota(jnp.int32, (ext, tk), 0) == 0).astype(BF)
        vx = jnp.concatenate([vt_ref[0], ones_row], axis=0)
        if masked and tq == tk:
            subs = [(0, tq // 2, 0, tk // 2), (tq // 2, tq, 0, tk)]
        else:
            subs = [(0, tq, 0, tk)]
        for q0, q1, k0, k1 in subs:
            if masked:
                key = ki * tk + k0 + lax.broadcasted_iota(jnp.int32, (k1 - k0, q1 - q0), 0)
                qry = qi * tq + q0 + lax.broadcasted_iota(jnp.int32, (k1 - k0, q1 - q0), 1)
                keep = qry >= key
            for c in range(2):
                s = _dot(k[k0:k1], qm_scr[c, :, q0:q1])
                if masked:
                    s = jnp.where(keep, s, NEG_INF)
                if online:
                    m_prev = m_scr[c, :, q0:q1]
                    m_new = jnp.maximum(m_prev, jnp.max(s, axis=0, keepdims=True))
                    p = jnp.exp(s - m_new)
                    acc_scr[c, :, q0:q1] = (jnp.exp(m_prev - m_new) * acc_scr[c, :, q0:q1]
                                            + _dot(vx[:, k0:k1], p.astype(BF)))
                    m_scr[c, :, q0:q1] = m_new
                else:
                    p = jnp.exp(s - bound_ref[...])
                    acc_scr[c, :, q0:q1] += _dot(vx[:, k0:k1], p.astype(BF))

    needs_mask = (ki + 1) * tk - 1 > qi * tq

    @pl.when(needs_mask)
    def _():
        absorb(True)

    @pl.when(jnp.logical_not(needs_mask))
    def _():
        absorb(False)

    @pl.when(ki == (qi * tq + tq - 1) // tk)
    def _():
        lam = _lambda(lam_ref, lam_init)
        a0 = acc_scr[0]
        a1 = acc_scr[1]
        o = a0[:dv] / a0[dv:dv + 1] - lam * (a1[:dv] / a1[dv:dv + 1])
        y = o * lax.rsqrt(jnp.mean(o * o, axis=0, keepdims=True) + EPS) * gs_ref[...]
        o_ref[...] = (y * (1.0 - lam_init)).T.astype(o_ref.dtype)


SAFE_SCORE_BOUND = 30.0


def flash_diff_attention(qt, k, vt, score_bound, lam_p, g_sub, heads, lam_init, tq, tk):
    t = k.shape[0]
    n_seq, _, seq = qt.shape
    dv = vt.shape[1] // heads
    dh = k.shape[1] // (2 * heads)
    assert 2 * dh == LANES and dv == LANES
    nq, nk = seq // tq, seq // tk
    pairs = [(i, j) for i in range(nq) for j in range((i * tq + tq - 1) // tk + 1)]
    qi_tab = jnp.asarray([p[0] for p in pairs], jnp.int32)
    ki_tab = jnp.asarray([p[1] for p in pairs], jnp.int32)

    def call(online):
        grid_spec = pltpu.PrefetchScalarGridSpec(
            num_scalar_prefetch=2,
            grid=(n_seq, heads, len(pairs)),
            in_specs=[pl.BlockSpec((1, LANES, tq), lambda b, h, p, qi, ki: (b, h, qi[p])),
                      pl.BlockSpec((tk, LANES), lambda b, h, p, qi, ki: (b * nk + ki[p], h)),
                      pl.BlockSpec((1, LANES, tk), lambda b, h, p, qi, ki: (b, h, ki[p])),
                      pl.BlockSpec((1, 1), lambda b, h, p, qi, ki: (0, 0)),
                      pl.BlockSpec(lam_p.shape, lambda b, h, p, qi, ki: (0, 0)),
                      pl.BlockSpec((dv, 1), lambda b, h, p, qi, ki: (0, 0))],
            out_specs=pl.BlockSpec((tq, LANES), lambda b, h, p, qi, ki: (b * nq + qi[p], h)),
            scratch_shapes=[pltpu.VMEM((2, LANES, tq), BF),
                            pltpu.VMEM((2, 1, tq), F32),
                            pltpu.VMEM((2, dv + 2 * SUBLANES, tq), F32)],
        )
        return pl.pallas_call(
            functools.partial(_flash_kernel, dh=dh, tq=tq, tk=tk, lam_init=lam_init, online=online),
            grid_spec=grid_spec,
            out_shape=jax.ShapeDtypeStruct((t, heads * dv), BF),
            compiler_params=_cparams(("parallel", "parallel", "arbitrary")),
            name="flash_online" if online else "flash_bounded",
        )(qi_tab, ki_tab, qt, k, vt, score_bound.reshape(1, 1).astype(F32), lam_p.astype(F32),
          g_sub.reshape(dv, 1).astype(F32))

    return lax.cond(score_bound < SAFE_SCORE_BOUND, lambda: call(False), lambda: call(True))


def _paged_kernel(pt_ref, q_ref, kn_ref, vn_ref, lam_ref, gs_ref, *rest,
                  heads, dh, n_new, pages, page, lam_init):
    del pt_ref
    kt_refs = rest[:pages]
    v_refs = rest[pages:2 * pages]
    o_ref, s_scr = rest[2 * pages:]
    width = heads * 2 * dh
    nrow = 2 * heads * n_new
    past = pages * page

    q = q_ref[...] * (dh ** -0.5)
    qt = jnp.concatenate([q] * (2 * heads), axis=0)
    r = lax.broadcasted_iota(jnp.int32, (nrow, width), 0)
    cblk = lax.broadcasted_iota(jnp.int32, (nrow, width), 1) // dh
    qbd = jnp.where(cblk == r // n_new, qt, 0.0).astype(BF)

    pad = jnp.zeros((page - n_new, width), F32)
    kn = jnp.concatenate([kn_ref[...], pad], axis=0).astype(BF)
    vn = jnp.concatenate([vn_ref[...], pad], axis=0).astype(BF)

    def softmax_part(page_ids, with_new):
        c0 = page_ids[0] * page
        c1 = (page_ids[-1] + 1) * page
        for p in page_ids:
            s_scr[:, p * page:(p + 1) * page] = _dot(qbd, kt_refs[p][0, 0].astype(BF))
        if with_new:
            s_new = _nt_dot(qbd, kn)
            tok = lax.broadcasted_iota(jnp.int32, s_new.shape, 0) % n_new
            key = lax.broadcasted_iota(jnp.int32, s_new.shape, 1)
            s_scr[:, past:] = jnp.where(key <= tok, s_new, NEG_INF)
            c1 = past + page
        s = s_scr[:, c0:c1]
        m = jnp.max(s, axis=-1, keepdims=True)
        pr = jnp.exp(s - m)
        l = jnp.sum(pr, axis=-1, keepdims=True)
        s_scr[:, c0:c1] = pr
        acc = _dot(s_scr[:, past:].astype(BF), vn) if with_new else jnp.zeros((nrow, width), F32)
        for p in page_ids:
            v_wide = jnp.concatenate(
                [v_refs[p][0, 0, pl.ds(h, page, stride=heads), :] for h in range(heads)], axis=1)
            acc += _dot(s_scr[:, p * page:(p + 1) * page].astype(BF), v_wide.astype(BF))
        return acc, l, m

    half = max(pages // 2, 1)
    acc, l, m = softmax_part(list(range(half, pages)), True) if half < pages else (None, None, None)
    acc_a, l_a, m_a = softmax_part(list(range(half)), half == pages)
    if acc is None:
        acc, l = acc_a, l_a
    else:
        m_all = jnp.maximum(m, m_a)
        w, w_a = jnp.exp(m - m_all), jnp.exp(m_a - m_all)
        acc = w * acc + w_a * acc_a
        l = w * l + w_a * l_a

    lam = _lambda(lam_ref, lam_init)
    acc = acc * (1.0 / l)
    for h in range(heads):
        r0 = h * 2 * n_new
        cols = slice(h * 2 * dh, (h + 1) * 2 * dh)
        o = acc[r0:r0 + n_new, cols] - lam * acc[r0 + n_new:r0 + 2 * n_new, cols]
        o_ref[:, cols] = _sub_norm(o, gs_ref, lam_init)


def paged_diff_attention(q, k_new, v_new, cache_kt, cache_v, layer, page_table, lam_p, g_sub,
                         heads, lam_init):
    t, width = q.shape
    b, n_pages = page_table.shape
    n_new = t // b
    dh = width // (2 * heads)
    page = cache_kt.shape[3]
    assert n_new == SUBLANES and 2 * dh == LANES and page == LANES
    nrow = 2 * heads * n_new
    tok = pl.BlockSpec((n_new, width), lambda i, pt: (i, 0))

    def page_spec(p, shape):
        return pl.BlockSpec((1, 1) + shape, lambda i, pt: (layer, pt[i, p], 0, 0))

    grid_spec = pltpu.PrefetchScalarGridSpec(
        num_scalar_prefetch=1,
        grid=(b,),
        in_specs=[tok, tok, tok,
                  pl.BlockSpec(lam_p.shape, lambda i, pt: (0, 0)),
                  pl.BlockSpec((1, 2 * dh), lambda i, pt: (0, 0))]
                 + [page_spec(p, (width, page)) for p in range(n_pages)]
                 + [page_spec(p, (page * heads, 2 * dh)) for p in range(n_pages)],
        out_specs=tok,
        scratch_shapes=[pltpu.VMEM((nrow, (n_pages + 1) * page), F32)],
    )
    return pl.pallas_call(
        functools.partial(_paged_kernel, heads=heads, dh=dh, n_new=n_new, pages=n_pages,
                          page=page, lam_init=lam_init),
        grid_spec=grid_spec,
        out_shape=jax.ShapeDtypeStruct((t, width), F32),
        compiler_params=_cparams(("parallel",)),
        name="paged_diff_attention",
    )(page_table, q, k_new, v_new, lam_p.astype(F32), g_sub.reshape(1, 2 * dh).astype(F32),
      *([cache_kt] * n_pages), *([cache_v] * n_pages))


def _trunk(x, rows_per_seq, mods, state, paged, weights, cfg):
    (g_norm_mix, g_norm_ffn, w_in_m, b_gate_m, g_head_m, w_out_m,
     w_in_d, g_q_d, g_k_d, lam_d, g_sub_d, w_out_d, w_gu_f, w_down_f,
     w_router, b_router, w_gu_e, w_down_e) = weights
    t, d = x.shape
    n_seq = t // rows_per_seq
    depth = g_norm_mix.shape[0]
    tm = cfg["tm"]
    tn_proj, tn_seg, tn_up = 1024, 256, 1408
    heads_m = b_gate_m.shape[1] // 2
    dk, dv = state[0].shape[3], state[0].shape[4]
    qk_w = heads_m * dk
    heads_a, dh = cfg["heads_a"], cfg["dh"]
    new_state, new_kv = state[0].shape[0], []
    kt_stack = None
    for i in range(depth):
        j = i // 2
        mod = mods[i]
        if i % 2 == 0:
            col_scale = jnp.concatenate([jnp.ones((qk_w,), F32), jnp.full((qk_w,), dk ** -0.5, F32),
                                         jnp.ones((heads_m * dv,), F32)]).reshape(1, -1)
            small = rows_per_seq % M_CHUNK != 0
            (qkv,) = norm_proj(x, rows_per_seq, mod, 1, 0, g_norm_mix[i], w_in_m[j], 0,
                               2 * qk_w + heads_m * dv, [F32 if small else BF], tm, tn_proj,
                               col_scale=col_scale, name="mlstm_qkv")
            (og,) = norm_proj(x, rows_per_seq, mod, 1, 0, g_norm_mix[i], w_in_m[j],
                              2 * qk_w + heads_m * dv, heads_m * dv, [F32], tm, tn_proj, name="mlstm_o")
            w_gate = jnp.pad(w_in_m[j][:, 2 * qk_w + 2 * heads_m * dv:], ((0, 0), (0, LANES - 2 * heads_m)))
            (gates,) = norm_proj(x, rows_per_seq, mod, 1, 0, g_norm_mix[i], w_gate, 0, LANES,
                                 [F32], tm, LANES, name="mlstm_gates")
            y, new_state = mlstm_cell(qkv, og, gates, b_gate_m[j], g_head_m[j], *state,
                                      rows_per_seq, cfg["mlstm_seqs"], j, new_state)
            w_out = w_out_m
        else:
            lam_init = 0.8 - 0.6 * math.exp(-0.3 * i)
            prompt = paged is None
            (q,) = norm_proj(x, rows_per_seq, mod, 1, 0, g_norm_mix[i], w_in_d[j], 0, d,
                             [BF if prompt else F32], tm, tn_seg, seg_gain=g_q_d[j],
                             transposed=(prompt,), name="attn_q")
            k32, kbf = norm_proj(x, rows_per_seq, mod, 1, 0, g_norm_mix[i], w_in_d[j], d, d,
                                 [F32, BF], tm, tn_seg, seg_gain=g_k_d[j], transposed=(prompt, False),
                                 stacked=(j, depth // 2, kt_stack), name="attn_k")
            v32, vbf = norm_proj(x, rows_per_seq, mod, 1, 0, g_norm_mix[i], w_in_d[j], 2 * d, d,
                                 [F32, BF], tm, tn_proj, transposed=(False, prompt), name="attn_v")
            if paged is None:
                bound = (dh ** 0.5) * jnp.max(jnp.abs(g_q_d[j])) * jnp.max(jnp.abs(g_k_d[j]))
                y = flash_diff_attention(q, kbf, vbf, bound, lam_d[j], g_sub_d[j], heads_a,
                                         lam_init, cfg["tq"], cfg["tk"])
                kt_stack = k32
            else:
                cache_kt, cache_v, page_table = paged
                y = paged_diff_attention(q, k32, v32, cache_kt, cache_v, j, page_table, lam_d[j],
                                         g_sub_d[j], heads_a, lam_init)
            new_kv.append((k32, v32))
            w_out = w_out_d
        x = down_residual(y, w_out, j, x, rows_per_seq, mod, 2, tm, tn_proj, name="mixer_out")
        if i % 2 == 0:
            act = swiglu_up(x, rows_per_seq, mod, 4, 3, g_norm_ffn[i], w_gu_f, j, tm, tn_up)
            x = down_residual(act, w_down_f, j, x, rows_per_seq, mod, 5, tm, tn_proj, name="ffn_down")
        else:
            n_exp = w_router.shape[2]
            h, dg, rank, counts = router_gates(x, rows_per_seq, mod, 4, 3, g_norm_ffn[i],
                                               w_router[j], b_router[j], tm)
            counts = counts[:, 0, :n_exp].astype(jnp.int32).reshape(-1)
            x = moe_ffn(h, dg, rank, counts, w_gu_e, w_down_e, j, x, rows_per_seq, mod, 5, tm,
                        cfg["moe_ff_chunks"], min(cfg["moe_block_rows"], tm // 2))
    return x, new_state, new_kv, kt_stack


def kernel(x_prompt, x_sample, c_prompt, c_sample, state_C, state_n, state_m, cache_k, cache_v, page_table, w_ada, b_ada, g_norm_mix, g_norm_ffn, w_in_m, b_gate_m, g_head_m, w_out_m, w_in_d, g_q_d, g_k_d, lam_d, g_sub_d, w_out_d, w_gu_f, w_down_f, w_router, b_router, w_gu_e, w_down_e):
    bp, seq, d = x_prompt.shape
    bs, dec_seq, _ = x_sample.shape
    depth = w_ada.shape[0]
    n_ml, _, heads_m, dk, dv = state_C.shape
    n_diff, n_pool, page, heads_a, _, dh = cache_k.shape
    weights = (g_norm_mix, g_norm_ffn, w_in_m.astype(BF), b_gate_m, g_head_m, w_out_m.astype(BF),
               w_in_d.astype(BF), g_q_d, g_k_d, lam_d, g_sub_d, w_out_d.astype(BF),
               w_gu_f.astype(BF), w_down_f.astype(BF),
               w_router, b_router, w_gu_e.astype(BF), w_down_e.astype(BF))

    n_c = bp + bs
    n_c_pad = -(-n_c // SUBLANES) * SUBLANES
    c_all = jnp.pad(jnp.concatenate([c_prompt, c_sample], axis=0), ((0, n_c_pad - n_c), (0, 0)))
    mod_all = ada_mod(c_all, w_ada, b_ada, tn=min(6 * d, 1536))
    mods_p = [mod_all[i, :bp].reshape(bp, 1, 6 * d) for i in range(depth)]
    mods_s = [mod_all[i, bp:n_c].reshape(bs, 1, 6 * d) for i in range(depth)]

    moe = dict(moe_ff_chunks=2, moe_block_rows=128)
    cfg_p = dict(tm=min(1024, seq), tq=min(1024, seq), tk=min(1024, seq), heads_a=heads_a, dh=dh,
                 mlstm_seqs=min(2, bp), **moe)
    cfg_s = dict(tm=min(1024, bs * dec_seq), heads_a=heads_a, dh=dh, mlstm_seqs=min(8, bs), **moe)

    zeros = (jnp.zeros((n_ml, bp, heads_m, dk, dv), F32), jnp.zeros((n_ml, bp, heads_m, dk), F32),
             jnp.zeros((n_ml, bp, heads_m), F32))
    y_p, st_p, kv_p, kt_p = _trunk(x_prompt.reshape(bp * seq, d), seq, mods_p, zeros, None, weights, cfg_p)
    paged = (jnp.transpose(cache_k, (0, 1, 3, 4, 5, 2)).reshape(n_diff, n_pool, heads_a * 2 * dh, page),
             cache_v.reshape(n_diff, n_pool, page * heads_a, 2 * dh), page_table)
    y_s, st_s, kv_s, _ = _trunk(x_sample.reshape(bs * dec_seq, d), dec_seq, mods_s,
                             (state_C, state_n, state_m), paged, weights, cfg_s)

    def stack(items, idx, shape):
        return jnp.stack([it[idx] for it in items]).reshape(shape)

    return (y_p.reshape(bp, seq, d), y_s.reshape(bs, dec_seq, d),
            st_p[0], st_p[1].reshape(n_ml, bp, heads_m, dk), st_p[2][:, :, :, 0, 0],
            st_s[0], st_s[1].reshape(n_ml, bs, heads_m, dk), st_s[2][:, :, :, 0, 0],
            jnp.transpose(kt_p.reshape(n_diff, bp, heads_a, 2, dh, seq), (0, 1, 5, 2, 3, 4)),
            stack(kv_p, 1, (n_diff, bp, seq, heads_a, 2 * dh)),
            stack(kv_s, 0, (n_diff, bs, dec_seq, heads_a, 2, dh)),
            stack(kv_s, 1, (n_diff, bs, dec_seq, heads_a, 2 * dh)))
```

```python
import functools
import math

import jax
import jax.numpy as jnp
from jax import lax
from jax.experimental import pallas as pl
from jax.experimental.pallas import tpu as pltpu

BF = jnp.bfloat16
F32 = jnp.float32
EPS = 1e-6
M_CHUNK = 128
LANES = 128
SUBLANES = 8
VMEM_LIMIT = 56 * 1024 * 1024
NEG_INF = float("-inf")


def _cparams(sem):
    return pltpu.CompilerParams(dimension_semantics=sem, vmem_limit_bytes=VMEM_LIMIT)


def _tile(n, pref):
    if n <= pref:
        return n
    t = pref - pref % LANES
    while n % t:
        t -= LANES
    return t


def _nt_dot(a, b, **kw):
    return lax.dot_general(a, b, (((1,), (1,)), ((), ())), preferred_element_type=F32, **kw)


def _tn_dot(a, b, **kw):
    return lax.dot_general(a, b, (((0,), (0,)), ((), ())), preferred_element_type=F32, **kw)


def _dot(a, b, **kw):
    return jnp.dot(a, b, preferred_element_type=F32, **kw)


def _ada_kernel(c_ref, w_ref, b_ref, o_ref):
    c = c_ref[...]
    a = (c * jax.nn.sigmoid(c)).astype(BF)
    o_ref[0] = _dot(a, w_ref[0].astype(BF)) + b_ref[0]


def ada_mod(c_all, w_ada, b_ada, tn):
    depth, d, n = w_ada.shape
    bp = c_all.shape[0]
    return pl.pallas_call(
        _ada_kernel,
        grid=(depth, n // tn),
        in_specs=[pl.BlockSpec((bp, d), lambda i, j: (0, 0)),
                  pl.BlockSpec((1, d, tn), lambda i, j: (i, 0, j)),
                  pl.BlockSpec((1, 1, tn), lambda i, j: (i, 0, j))],
        out_specs=pl.BlockSpec((1, bp, tn), lambda i, j: (i, 0, j)),
        out_shape=jax.ShapeDtypeStruct((depth, bp, n), F32),
        compiler_params=_cparams(("arbitrary", "arbitrary")),
        name="ada_mod",
    )(c_all, w_ada, b_ada.reshape(depth, 1, n))


def _mod_spec(rows_per_seq, tm, width, col_block):
    if rows_per_seq >= tm:
        npb = rows_per_seq // tm
        return pl.BlockSpec((1, 1, width), lambda *g: (g[0] // npb, 0, col_block(*g)))
    gb = tm // rows_per_seq
    return pl.BlockSpec((gb, 1, width), lambda *g: (g[0], 0, col_block(*g)))


def _token_rows(m, tm):
    gb, _, w = m.shape
    if gb == 1:
        return m[0]
    return jnp.broadcast_to(m, (gb, tm // gb, w)).reshape(tm, w)


def _rows(m_ref, tm):
    return _token_rows(m_ref[...], tm)


def _modulated(x_ref, g_ref, sc_ref, sh_ref):
    x = x_ref[...]
    tm = x.shape[0]
    gain = g_ref[...] * (1.0 + sc_ref[...])
    y = x * lax.rsqrt(jnp.mean(x * x, axis=-1, keepdims=True) + EPS)
    return y * _token_rows(gain, tm) + _rows(sh_ref, tm)


def _store_tiles(outs, y, transposed, first_slot):
    yt = y.T if any(transposed) else None
    for idx, (o, tr) in enumerate(zip(outs, transposed)):
        if idx == 0 and first_slot is not None:
            for slot in range(o.shape[0]):
                o[slot, 0] = yt.astype(o.dtype) if slot == first_slot else jnp.zeros(yt.shape, o.dtype)
        elif tr:
            o[(0,) * (len(o.shape) - 2)] = yt.astype(o.dtype)
        else:
            o[...] = y.astype(o.dtype)


def _proj_kernel(x_ref, g_ref, sc_ref, sh_ref, w_ref, cs_ref, *rest, transposed, n_alias,
                 first_slot):
    outs, h_scr = rest[n_alias:-1], rest[-1]

    @pl.when(pl.program_id(1) == 0)
    def _():
        h_scr[...] = _modulated(x_ref, g_ref, sc_ref, sh_ref).astype(BF)

    acc = _dot(h_scr[...], w_ref[...].astype(BF)) * cs_ref[...]
    _store_tiles(outs, acc, transposed, first_slot)


def _segnorm_kernel(x_ref, g_ref, sc_ref, sh_ref, w_ref, gv_ref, seg_ref, *rest, seg, transposed,
                    n_alias, first_slot):
    outs, h_scr = rest[n_alias:-1], rest[-1]

    @pl.when(pl.program_id(1) == 0)
    def _():
        h_scr[...] = _modulated(x_ref, g_ref, sc_ref, sh_ref).astype(BF)

    acc = _dot(h_scr[...], w_ref[...].astype(BF))
    ssq = _dot((acc * acc).astype(BF), seg_ref[...])
    y = acc * lax.rsqrt(ssq * (1.0 / seg) + EPS) * gv_ref[...]
    _store_tiles(outs, y, transposed, first_slot)


def norm_proj(x, rows_per_seq, mod, sc_chunk, sh_chunk, g, w, col_off, n, out_dtypes, tm, tn,
              col_scale=None, seg_gain=None, transposed=False, stacked=None, name="norm_proj"):
    t, d = x.shape
    prev = []
    if transposed is False:
        transposed = (False,) * len(out_dtypes)
    tn = _tile(math.gcd(n, col_off) if col_off else n, tn)
    x_spec = pl.BlockSpec((tm, d), lambda i, j: (i, 0))
    g_spec = pl.BlockSpec((1, d), lambda i, j: (0, 0))
    sc_spec = _mod_spec(rows_per_seq, tm, d, lambda i, j: sc_chunk)
    sh_spec = _mod_spec(rows_per_seq, tm, d, lambda i, j: sh_chunk)
    off = col_off // tn
    w_spec = pl.BlockSpec((d, tn), lambda i, j: (0, off + j))
    npb = max(rows_per_seq // tm, 1)
    out_specs, out_shape = [], []
    for dt, tr in zip(out_dtypes, transposed):
        if tr:
            out_specs.append(pl.BlockSpec((1, tn, tm), lambda i, j: (i // npb, j, i % npb)))
            out_shape.append(jax.ShapeDtypeStruct((t // rows_per_seq, n, rows_per_seq), dt))
        else:
            out_specs.append(pl.BlockSpec((tm, tn), lambda i, j: (i, j)))
            out_shape.append(jax.ShapeDtypeStruct((t, n), dt))
    first_slot = None
    if stacked is not None and transposed[0]:
        layer, n_layers, previous = stacked
        if previous is None:
            first_slot = layer
            out_specs[0] = pl.BlockSpec((n_layers, 1, tn, tm), lambda i, j: (0, i // npb, j, i % npb))
        else:
            out_specs[0] = pl.BlockSpec((1, 1, tn, tm), lambda i, j: (layer, i // npb, j, i % npb))
            prev = [previous]
        out_shape[0] = jax.ShapeDtypeStruct((n_layers,) + out_shape[0].shape, out_dtypes[0])
    if seg_gain is None:
        if col_scale is None:
            col_scale = jnp.ones((1, n), F32)
        kern = functools.partial(_proj_kernel, transposed=transposed, n_alias=len(prev),
                                 first_slot=first_slot)
        extra = [col_scale]
        extra_specs = [pl.BlockSpec((1, tn), lambda i, j: (0, j))]
    else:
        seg = seg_gain.shape[0]
        kern = functools.partial(_segnorm_kernel, seg=seg, transposed=transposed, n_alias=len(prev),
                                 first_slot=first_slot)
        gv = jnp.tile(seg_gain.astype(F32), tn // seg).reshape(1, tn)
        ids = jnp.arange(tn) // seg
        seg_mat = (ids[:, None] == ids[None, :]).astype(BF)
        extra = [gv, seg_mat]
        extra_specs = [pl.BlockSpec((1, tn), lambda i, j: (0, 0)),
                       pl.BlockSpec((tn, tn), lambda i, j: (0, 0))]
    res = pl.pallas_call(
        kern,
        grid=(t // tm, n // tn),
        in_specs=[x_spec, g_spec, sc_spec, sh_spec, w_spec] + extra_specs
                 + [pl.BlockSpec(memory_space=pl.ANY)] * len(prev),
        out_specs=out_specs,
        out_shape=out_shape,
        scratch_shapes=[pltpu.VMEM((tm, d), BF)],
        input_output_aliases={5 + len(extra) + i: i for i in range(len(prev))},
        compiler_params=_cparams(("parallel", "arbitrary")),
        name=name,
    )(x, g.reshape(1, d), mod, mod, w, *extra, *prev)
    return res


def _router_kernel(x_ref, g_ref, sc_ref, sh_ref, w_ref, b_ref, tri_ref,
                   h_ref, o_ref, pos_ref, cnt_ref, *, n_experts):
    h = _modulated(x_ref, g_ref, sc_ref, sh_ref)
    h_ref[...] = h.astype(h_ref.dtype)
    w = w_ref[...]
    h_hi, w_hi = h.astype(BF), w.astype(BF)
    h_lo = (h - h_hi.astype(F32)).astype(BF)
    w_lo = (w - w_hi.astype(F32)).astype(BF)
    logits = _dot(h_hi, w_hi) + (_dot(h_hi, w_lo) + _dot(h_lo, w_hi)) + b_ref[...]
    lane = lax.broadcasted_iota(jnp.int32, logits.shape, 1).astype(F32)
    big = float(LANES)
    l1 = jnp.where(lane < n_experts, logits, NEG_INF)
    m1 = jnp.max(l1, axis=-1, keepdims=True)
    i1 = jnp.min(jnp.where(l1 == m1, lane, big), axis=-1, keepdims=True)
    l2 = jnp.where(lane == i1, NEG_INF, l1)
    m2 = jnp.max(l2, axis=-1, keepdims=True)
    i2 = jnp.min(jnp.where(l2 == m2, lane, big), axis=-1, keepdims=True)
    e2 = jnp.exp(m2 - m1)
    w1 = 1.0 / (1.0 + e2)
    w2 = e2 / (1.0 + e2)
    gate = jnp.where(lane == i1, w1, 0.0) + jnp.where(lane == i2, w2, 0.0)
    o_ref[...] = gate
    routed = jnp.where(gate > 0.0, 1.0, 0.0)
    pos_ref[...] = _dot(tri_ref[...], routed.astype(BF))
    cnt_ref[0] = jnp.sum(routed, axis=0, keepdims=True)


def router_gates(x, rows_per_seq, mod, sc_chunk, sh_chunk, g, w_router, b_router, tm):
    t, d = x.shape
    e = w_router.shape[1]
    w_pad = jnp.pad(w_router, ((0, 0), (0, LANES - e)))
    b_pad = jnp.pad(b_router, (0, LANES - e)).reshape(1, LANES)
    idx = jnp.arange(tm)
    strict_lower = (idx[:, None] > idx[None, :]).astype(BF)
    tok = pl.BlockSpec((tm, LANES), lambda i: (i, 0))
    return pl.pallas_call(
        functools.partial(_router_kernel, n_experts=e),
        grid=(t // tm,),
        in_specs=[pl.BlockSpec((tm, d), lambda i: (i, 0)),
                  pl.BlockSpec((1, d), lambda i: (0, 0)),
                  _mod_spec(rows_per_seq, tm, d, lambda i: sc_chunk),
                  _mod_spec(rows_per_seq, tm, d, lambda i: sh_chunk),
                  pl.BlockSpec((d, LANES), lambda i: (0, 0)),
                  pl.BlockSpec((1, LANES), lambda i: (0, 0)),
                  pl.BlockSpec((tm, tm), lambda i: (0, 0))],
        out_specs=[pl.BlockSpec((tm, d), lambda i: (i, 0)), tok, tok,
                   pl.BlockSpec((1, 1, LANES), lambda i: (i, 0, 0))],
        out_shape=[jax.ShapeDtypeStruct((t, d), BF),
                   jax.ShapeDtypeStruct((t, LANES), F32),
                   jax.ShapeDtypeStruct((t, LANES), F32),
                   jax.ShapeDtypeStruct((t // tm, 1, LANES), F32)],
        compiler_params=_cparams(("parallel",)),
        name="router",
    )(x, g.reshape(1, d), mod, mod, w_pad, b_pad, strict_lower)


def _moe_kernel(cnt_ref, h_ref, dg_ref, pos_ref, wg_ref, wu_ref, wd_ref, x_ref, gt_ref, o_ref,
                xs_scr, y_scr, *, n_experts, br):
    ti, e, fc = pl.program_id(0), pl.program_id(1), pl.program_id(2)
    last_fc = pl.num_programs(2) - 1
    tm = h_ref.shape[0]

    @pl.when((e == 0) & (fc == 0))
    def _():
        o_ref[...] = jnp.zeros_like(o_ref)

    sel = lax.broadcasted_iota(jnp.int32, (tm, LANES), 1) == e
    ge = jnp.sum(jnp.where(sel, dg_ref[...], 0.0), axis=-1, keepdims=True)
    pos = jnp.sum(jnp.where(sel, pos_ref[...], 0.0), axis=-1, keepdims=True)
    routed = ge > 0.0
    n_blocks = (cnt_ref[ti * n_experts + e] + br - 1) // br

    def block(r0, rows_in_block):
        rows = pl.ds(r0, rows_in_block)
        slot = (r0 + lax.broadcasted_iota(jnp.int32, (tm, rows_in_block), 1)).astype(F32)
        onehot = jnp.where((pos == slot) & routed, 1.0, 0.0).astype(BF)

        @pl.when(fc == 0)
        def _():
            xs_scr[rows, :] = _tn_dot(onehot, h_ref[...]).astype(BF)

        xs = xs_scr[rows, :]
        a = _dot(xs, wg_ref[0, 0])
        u = _dot(xs, wu_ref[0, 0])
        y = _dot((a * jax.nn.sigmoid(a) * u).astype(BF), wd_ref[0, 0])

        @pl.when(fc == 0)
        def _():
            y_scr[rows, :] = y

        @pl.when(fc > 0)
        def _():
            y_scr[rows, :] += y

        @pl.when(fc == last_fc)
        def _():
            o_ref[...] += ge * _dot(onehot, y_scr[rows, :].astype(BF))

    def pair(i, carry):
        block(pl.multiple_of(i * 2 * br, 2 * br), 2 * br)
        return carry

    lax.fori_loop(0, n_blocks // 2, pair, 0)

    @pl.when(n_blocks % 2 == 1)
    def _():
        block(pl.multiple_of((n_blocks - 1) * br, br), br)

    @pl.when((e == n_experts - 1) & (fc == last_fc))
    def _():
        o_ref[...] = x_ref[...] + _rows(gt_ref, tm) * o_ref[...]


def moe_ffn(h, dense_gate, rank, counts, w_gu, w_down, layer, x, rows_per_seq, mod, gt_chunk, tm,
            ff_chunks, block_rows):
    t, d = x.shape
    _, e, _, f2 = w_gu.shape
    f = f2 // 2
    fcw = f // ff_chunks
    tok = pl.BlockSpec((tm, LANES), lambda i, k, c, cnt: (i, 0))
    grid_spec = pltpu.PrefetchScalarGridSpec(
        num_scalar_prefetch=1,
        grid=(t // tm, e, ff_chunks),
        in_specs=[pl.BlockSpec((tm, d), lambda i, k, c, cnt: (i, 0)), tok, tok,
                  pl.BlockSpec((1, 1, d, fcw), lambda i, k, c, cnt: (layer, k, 0, c)),
                  pl.BlockSpec((1, 1, d, fcw), lambda i, k, c, cnt: (layer, k, 0, ff_chunks + c)),
                  pl.BlockSpec((1, 1, fcw, d), lambda i, k, c, cnt: (layer, k, c, 0)),
                  pl.BlockSpec((tm, d), lambda i, k, c, cnt: (i, 0)),
                  _mod_spec(rows_per_seq, tm, d, lambda i, k, c, cnt: gt_chunk)],
        out_specs=pl.BlockSpec((tm, d), lambda i, k, c, cnt: (i, 0)),
        scratch_shapes=[pltpu.VMEM((tm, d), BF), pltpu.VMEM((tm, d), F32)],
    )
    return pl.pallas_call(
        functools.partial(_moe_kernel, n_experts=e, br=block_rows),
        grid_spec=grid_spec,
        out_shape=jax.ShapeDtypeStruct((t, d), F32),
        compiler_params=_cparams(("parallel", "arbitrary", "arbitrary")),
        name="moe_ffn",
    )(counts, h, dense_gate, rank, w_gu, w_gu, w_down, x, mod)


def _swiglu_up_kernel(x_ref, g_ref, sc_ref, sh_ref, wg_ref, wu_ref, o_ref, h_scr):
    @pl.when(pl.program_id(1) == 0)
    def _():
        h_scr[...] = _modulated(x_ref, g_ref, sc_ref, sh_ref).astype(BF)

    h = h_scr[...]
    a = _dot(h, wg_ref[0].astype(BF))
    u = _dot(h, wu_ref[0].astype(BF))
    o_ref[...] = (a * jax.nn.sigmoid(a) * u).astype(o_ref.dtype)


def swiglu_up(x, rows_per_seq, mod, sc_chunk, sh_chunk, g, w_gu, layer, tm, tn):
    t, d = x.shape
    f = w_gu.shape[2] // 2
    tn = _tile(f, tn)
    nf = f // tn
    return pl.pallas_call(
        _swiglu_up_kernel,
        grid=(t // tm, nf),
        in_specs=[pl.BlockSpec((tm, d), lambda i, j: (i, 0)),
                  pl.BlockSpec((1, d), lambda i, j: (0, 0)),
                  _mod_spec(rows_per_seq, tm, d, lambda i, j: sc_chunk),
                  _mod_spec(rows_per_seq, tm, d, lambda i, j: sh_chunk),
                  pl.BlockSpec((1, d, tn), lambda i, j: (layer, 0, j)),
                  pl.BlockSpec((1, d, tn), lambda i, j: (layer, 0, nf + j))],
        out_specs=pl.BlockSpec((tm, tn), lambda i, j: (i, j)),
        out_shape=jax.ShapeDtypeStruct((t, f), BF),
        scratch_shapes=[pltpu.VMEM((tm, d), BF)],
        compiler_params=_cparams(("parallel", "arbitrary")),
        name="swiglu_up",
    )(x, g.reshape(1, d), mod, mod, w_gu, w_gu)


def _down_kernel(y_ref, w_ref, x_ref, gt_ref, o_ref):
    p = _dot(y_ref[...].astype(BF), w_ref[0].astype(BF))
    o_ref[...] = x_ref[...] + _rows(gt_ref, x_ref.shape[0]) * p


def down_residual(y, w, layer, x, rows_per_seq, mod, gt_chunk, tm, tn, name="down_residual"):
    t, k = y.shape
    d = x.shape[1]
    tn = _tile(d, tn)
    nd = d // tn
    return pl.pallas_call(
        _down_kernel,
        grid=(t // tm, nd),
        in_specs=[pl.BlockSpec((tm, k), lambda i, j: (i, 0)),
                  pl.BlockSpec((1, k, tn), lambda i, j: (layer, 0, j)),
                  pl.BlockSpec((tm, tn), lambda i, j: (i, j)),
                  _mod_spec(rows_per_seq, tm, tn, lambda i, j: gt_chunk * nd + j)],
        out_specs=pl.BlockSpec((tm, tn), lambda i, j: (i, j)),
        out_shape=jax.ShapeDtypeStruct((t, d), F32),
        compiler_params=_cparams(("parallel", "arbitrary")),
        name=name,
    )(y, w, x, mod)


def _log_sigmoid(x):
    return jnp.minimum(x, 0.0) - jnp.log1p(jnp.exp(-jnp.abs(x)))


def _mlstm_kernel(qkv_ref, o_ref, gt_ref, bg_ref, gh_ref, c0_ref, n0_ref, m0_ref, *rest,
                  heads, dk, dv, chunk, seqs, unroll, slot):
    y_ref, c_all, n_all, m_all = rest[-4:]
    c_ref, n_ref, m_ref = c_all.at[slot], n_all.at[slot], m_all.at[slot]
    L = chunk
    hi = lax.Precision.HIGHEST
    cd = BF

    @pl.when(pl.program_id(1) == 0)
    def _():
        for other in range(c_all.shape[0]):
            if other != slot:
                c_all[other] = jnp.zeros(c_all.shape[1:], F32)
                n_all[other] = jnp.zeros(n_all.shape[1:], F32)
                m_all[other] = jnp.zeros(m_all.shape[1:], F32)
        c_ref[...] = c0_ref[0]
        n_ref[...] = n0_ref[0]
        m_ref[...] = m0_ref[0]

    row = lax.broadcasted_iota(jnp.int32, (L, L), 0)
    col = lax.broadcasted_iota(jnp.int32, (L, L), 1)
    causal = row >= col
    tril = causal.astype(F32)
    sel = (lax.broadcasted_iota(jnp.int32, (SUBLANES, LANES), 0)
           == lax.broadcasted_iota(jnp.int32, (SUBLANES, LANES), 1)).astype(F32)

    def one_seq(bi, results):
        gates = gt_ref[bi] + bg_ref[...]
        lf = _log_sigmoid(gates)
        bcum = _dot(tril, lf, precision=hi)
        g_rows = _nt_dot(sel, gates, precision=hi)
        b_rows = _nt_dot(sel, bcum, precision=hi)
        for h in range(heads):
            q = qkv_ref[bi, :, h * dk:(h + 1) * dk]
            k = qkv_ref[bi, :, (heads + h) * dk:(heads + h + 1) * dk]
            v = qkv_ref[bi, :, 2 * heads * dk + h * dv:2 * heads * dk + (h + 1) * dv]
            qf, kf = q.astype(F32), k.astype(F32)
            qc, kc, vc = q.astype(cd), k.astype(cd), v.astype(cd)
            ig_col = gates[:, h:h + 1]
            ig_row = g_rows[h:h + 1, :]
            b_col = bcum[:, heads + h:heads + h + 1]
            b_row = b_rows[heads + h:heads + h + 1, :]
            b_end = bcum[L - 1:L, heads + h:heads + h + 1]
            c_old = c_ref[bi, h]
            n_old = n_ref[bi, h]
            m_old = m_ref[bi, h][:, :1]

            dlog = jnp.where(causal, b_col - b_row + ig_row, NEG_INF)
            gcar = b_col + m_old
            m_t = jnp.maximum(gcar, jnp.max(dlog, axis=-1, keepdims=True))
            s = _nt_dot(qc, kc) * jnp.exp(dlog - m_t)
            dec = jnp.exp(gcar - m_t)
            num = _dot(s.astype(cd), vc) + dec * _dot(qc, c_old.astype(cd))
            den = (jnp.sum(s, axis=-1, keepdims=True)
                   + dec * jnp.sum(qf * n_old, axis=-1, keepdims=True))
            hh = num / jnp.maximum(jnp.abs(den), jnp.exp(-m_t))

            a_row = b_end - b_row + ig_row
            a_col = b_end - b_col + ig_col
            m_new = jnp.maximum(b_end + m_old, jnp.max(a_row, axis=-1, keepdims=True))
            wk = jnp.exp(a_col - m_new)
            decay = jnp.exp(b_end + m_old - m_new)
            kw = kf * wk
            c_new = decay * c_old + _tn_dot(kw.astype(cd), vc)
            n_new = decay * n_old + jnp.sum(kw, axis=0, keepdims=True)

            hs = (hh * lax.rsqrt(jnp.mean(hh * hh, axis=-1, keepdims=True) + EPS)
                  * gh_ref[:, h * dv:(h + 1) * dv])
            og = o_ref[bi, :, h * dv:(h + 1) * dv]
            y = (hs * jax.nn.sigmoid(og)).astype(y_ref.dtype)
            results.append((bi, h, c_new, n_new, jnp.broadcast_to(m_new, (1, LANES)), y))

    def store(results):
        for bi, h, c_new, n_new, m_new, y in results:
            c_ref[bi, h] = c_new
            n_ref[bi, h] = n_new
            m_ref[bi, h] = m_new
            y_ref[bi, :, h * dv:(h + 1) * dv] = y

    if unroll:
        results = []
        for bi in range(seqs):
            one_seq(bi, results)
        store(results)
    else:
        def body(bi, carry):
            results = []
            one_seq(bi, results)
            store(results)
            return carry
        lax.fori_loop(0, seqs, body, 0)


def mlstm_cell(qkv, og, gates, b_gate, g_head, c0, n0, m0, seq_len, seqs_per_step, layer, stacked):
    t = qkv.shape[0]
    n_in_layers, b, heads, dk, dv = c0.shape
    chunk = M_CHUNK if seq_len % M_CHUNK == 0 else seq_len
    nc = seq_len // chunk
    bt = seqs_per_step
    n4 = n0.reshape(n_in_layers, b, heads, 1, dk)
    m4 = jnp.broadcast_to(m0.reshape(n_in_layers, b, heads, 1, 1), (n_in_layers, b, heads, 1, LANES))
    bg = jnp.pad(b_gate.astype(F32), (0, LANES - 2 * heads)).reshape(1, LANES)
    y_dtype = BF if chunk >= 16 else F32
    tok = lambda i, c: (i, c, 0)
    st = lambda i, c: (layer, i, 0, 0, 0)
    first = isinstance(stacked, int)
    n_layers = stacked if first else stacked[0].shape[0]
    prev = [] if first else list(stacked)
    slots = n_layers if first else 1
    st_out = lambda i, c: (0 if first else layer, i, 0, 0, 0)
    n_in = 8
    y, c_all, n_all, m_all = pl.pallas_call(
        functools.partial(_mlstm_kernel, heads=heads, dk=dk, dv=dv, chunk=chunk, seqs=bt,
                          unroll=bt <= 8, slot=layer if first else 0),
        grid=(b // bt, nc),
        in_specs=[pl.BlockSpec((bt, chunk, qkv.shape[1]), tok),
                  pl.BlockSpec((bt, chunk, heads * dv), tok),
                  pl.BlockSpec((bt, chunk, LANES), tok),
                  pl.BlockSpec((1, LANES), lambda i, c: (0, 0)),
                  pl.BlockSpec((1, heads * dv), lambda i, c: (0, 0)),
                  pl.BlockSpec((1, bt, heads, dk, dv), st),
                  pl.BlockSpec((1, bt, heads, 1, dk), st),
                  pl.BlockSpec((1, bt, heads, 1, LANES), st)]
                 + [pl.BlockSpec(memory_space=pl.ANY)] * len(prev),
        out_specs=[pl.BlockSpec((bt, chunk, heads * dv), tok),
                   pl.BlockSpec((slots, bt, heads, dk, dv), st_out),
                   pl.BlockSpec((slots, bt, heads, 1, dk), st_out),
                   pl.BlockSpec((slots, bt, heads, 1, LANES), st_out)],
        out_shape=[jax.ShapeDtypeStruct((b, seq_len, heads * dv), y_dtype),
                   jax.ShapeDtypeStruct((n_layers, b, heads, dk, dv), F32),
                   jax.ShapeDtypeStruct((n_layers, b, heads, 1, dk), F32),
                   jax.ShapeDtypeStruct((n_layers, b, heads, 1, LANES), F32)],
        input_output_aliases={n_in + i: 1 + i for i in range(len(prev))},
        compiler_params=_cparams(("parallel", "arbitrary")),
        name="mlstm_cell",
    )(qkv.reshape(b, seq_len, -1), og.reshape(b, seq_len, -1), gates.reshape(b, seq_len, LANES),
      bg, g_head.reshape(1, heads * dv).astype(F32), c0, n4, m4, *prev)
    return y.reshape(t, heads * dv), (c_all, n_all, m_all)


def _lambda(lam_ref, lam_init):
    lp = lam_ref[...]
    a = jnp.sum(lp[0:1] * lp[1:2], axis=-1, keepdims=True)
    b = jnp.sum(lp[2:3] * lp[3:4], axis=-1, keepdims=True)
    return jnp.exp(a) - jnp.exp(b) + lam_init


def _sub_norm(o, gsub_ref, lam_init):
    y = o * lax.rsqrt(jnp.mean(o * o, axis=-1, keepdims=True) + EPS) * gsub_ref[...]
    return y * (1.0 - lam_init)


def _flash_kernel(qi_ref, ki_ref, qt_ref, k_ref, vt_ref, bound_ref, lam_ref, gs_ref, o_ref,
                  qm_scr, m_scr, acc_scr, *, dh, tq, tk, lam_init, online):
    pair = pl.program_id(2)
    qi = qi_ref[pair]
    ki = ki_ref[pair]
    dv = LANES
    ext = 2 * SUBLANES

    @pl.when(ki == 0)
    def _():
        q = qt_ref[0] * (dh ** -0.5)
        row = lax.broadcasted_iota(jnp.int32, q.shape, 0)
        qm_scr[0] = jnp.where(row < dh, q, jnp.zeros_like(q))
        qm_scr[1] = jnp.where(row >= dh, q, jnp.zeros_like(q))
        m_scr[...] = jnp.full_like(m_scr, NEG_INF)
        acc_scr[...] = jnp.zeros_like(acc_scr)

    def absorb(masked):
        k = k_ref[...]
        ones_row = (lax.broadcasted_iota(jnp.int32, (ext, tk), 0) == 0).astype(BF)
        vx = jnp.concatenate([vt_ref[0], ones_row], axis=0)
        if masked and tq == tk:
            subs = [(0, tq // 2, 0, tk // 2), (tq // 2, tq, 0, tk)]
        else:
            subs = [(0, tq, 0, tk)]
        for q0, q1, k0, k1 in subs:
            if masked:
                key = ki * tk + k0 + lax.broadcasted_iota(jnp.int32, (k1 - k0, q1 - q0), 0)
                qry = qi * tq + q0 + lax.broadcasted_iota(jnp.int32, (k1 - k0, q1 - q0), 1)
                keep = qry >= key
            for c in range(2):
                s = _dot(k[k0:k1], qm_scr[c, :, q0:q1])
                if masked:
                    s = jnp.where(keep, s, NEG_INF)
                if online:
                    m_prev = m_scr[c, :, q0:q1]
                    m_new = jnp.maximum(m_prev, jnp.max(s, axis=0, keepdims=True))
                    p = jnp.exp(s - m_new)
                    acc_scr[c, :, q0:q1] = (jnp.exp(m_prev - m_new) * acc_scr[c, :, q0:q1]
                                            + _dot(vx[:, k0:k1], p.astype(BF)))
                    m_scr[c, :, q0:q1] = m_new
                else:
                    p = jnp.exp(s - bound_ref[...])
                    acc_scr[c, :, q0:q1] += _dot(vx[:, k0:k1], p.astype(BF))

    needs_mask = (ki + 1) * tk - 1 > qi * tq

    @pl.when(needs_mask)
    def _():
        absorb(True)

    @pl.when(jnp.logical_not(needs_mask))
    def _():
        absorb(False)

    @pl.when(ki == (qi * tq + tq - 1) // tk)
    def _():
        lam = _lambda(lam_ref, lam_init)
        a0 = acc_scr[0]
        a1 = acc_scr[1]
        o = a0[:dv] / a0[dv:dv + 1] - lam * (a1[:dv] / a1[dv:dv + 1])
        y = o * lax.rsqrt(jnp.mean(o * o, axis=0, keepdims=True) + EPS) * gs_ref[...]
        o_ref[...] = (y * (1.0 - lam_init)).T.astype(o_ref.dtype)


SAFE_SCORE_BOUND = 30.0


def flash_diff_attention(qt, k, vt, score_bound, lam_p, g_sub, heads, lam_init, tq, tk):
    t = k.shape[0]
    n_seq, _, seq = qt.shape
    dv = vt.shape[1] // heads
    dh = k.shape[1] // (2 * heads)
    assert 2 * dh == LANES and dv == LANES
    nq, nk = seq // tq, seq // tk
    pairs = [(i, j) for i in range(nq) for j in range((i * tq + tq - 1) // tk + 1)]
    qi_tab = jnp.asarray([p[0] for p in pairs], jnp.int32)
    ki_tab = jnp.asarray([p[1] for p in pairs], jnp.int32)

    def call(online):
        grid_spec = pltpu.PrefetchScalarGridSpec(
            num_scalar_prefetch=2,
            grid=(n_seq, heads, len(pairs)),
            in_specs=[pl.BlockSpec((1, LANES, tq), lambda b, h, p, qi, ki: (b, h, qi[p])),
                      pl.BlockSpec((tk, LANES), lambda b, h, p, qi, ki: (b * nk + ki[p], h)),
                      pl.BlockSpec((1, LANES, tk), lambda b, h, p, qi, ki: (b, h, ki[p])),
                      pl.BlockSpec((1, 1), lambda b, h, p, qi, ki: (0, 0)),
                      pl.BlockSpec(lam_p.shape, lambda b, h, p, qi, ki: (0, 0)),
                      pl.BlockSpec((dv, 1), lambda b, h, p, qi, ki: (0, 0))],
            out_specs=pl.BlockSpec((tq, LANES), lambda b, h, p, qi, ki: (b * nq + qi[p], h)),
            scratch_shapes=[pltpu.VMEM((2, LANES, tq), BF),
                            pltpu.VMEM((2, 1, tq), F32),
                            pltpu.VMEM((2, dv + 2 * SUBLANES, tq), F32)],
        )
        return pl.pallas_call(
            functools.partial(_flash_kernel, dh=dh, tq=tq, tk=tk, lam_init=lam_init, online=online),
            grid_spec=grid_spec,
            out_shape=jax.ShapeDtypeStruct((t, heads * dv), BF),
            compiler_params=_cparams(("parallel", "parallel", "arbitrary")),
            name="flash_online" if online else "flash_bounded",
        )(qi_tab, ki_tab, qt, k, vt, score_bound.reshape(1, 1).astype(F32), lam_p.astype(F32),
          g_sub.reshape(dv, 1).astype(F32))

    return lax.cond(score_bound < SAFE_SCORE_BOUND, lambda: call(False), lambda: call(True))


def _paged_kernel(pt_ref, q_ref, kn_ref, vn_ref, lam_ref, gs_ref, *rest,
                  heads, dh, n_new, pages, page, lam_init):
    del pt_ref
    kt_refs = rest[:pages]
    v_refs = rest[pages:2 * pages]
    o_ref, s_scr = rest[2 * pages:]
    width = heads * 2 * dh
    nrow = 2 * heads * n_new
    past = pages * page

    q = q_ref[...] * (dh ** -0.5)
    qt = jnp.concatenate([q] * (2 * heads), axis=0)
    r = lax.broadcasted_iota(jnp.int32, (nrow, width), 0)
    cblk = lax.broadcasted_iota(jnp.int32, (nrow, width), 1) // dh
    qbd = jnp.where(cblk == r // n_new, qt, 0.0).astype(BF)

    pad = jnp.zeros((page - n_new, width), F32)
    kn = jnp.concatenate([kn_ref[...], pad], axis=0).astype(BF)
    vn = jnp.concatenate([vn_ref[...], pad], axis=0).astype(BF)

    def softmax_part(page_ids, with_new):
        c0 = page_ids[0] * page
        c1 = (page_ids[-1] + 1) * page
        for p in page_ids:
            s_scr[:, p * page:(p + 1) * page] = _dot(qbd, kt_refs[p][0, 0].astype(BF))
        if with_new:
            s_new = _nt_dot(qbd, kn)
            tok = lax.broadcasted_iota(jnp.int32, s_new.shape, 0) % n_new
            key = lax.broadcasted_iota(jnp.int32, s_new.shape, 1)
            s_scr[:, past:] = jnp.where(key <= tok, s_new, NEG_INF)
            c1 = past + page
        s = s_scr[:, c0:c1]
        m = jnp.max(s, axis=-1, keepdims=True)
        pr = jnp.exp(s - m)
        l = jnp.sum(pr, axis=-1, keepdims=True)
        s_scr[:, c0:c1] = pr
        acc = _dot(s_scr[:, past:].astype(BF), vn) if with_new else jnp.zeros((nrow, width), F32)
        for p in page_ids:
            v_wide = jnp.concatenate(
                [v_refs[p][0, 0, pl.ds(h, page, stride=heads), :] for h in range(heads)], axis=1)
            acc += _dot(s_scr[:, p * page:(p + 1) * page].astype(BF), v_wide.astype(BF))
        return acc, l, m

    half = max(pages // 2, 1)
    acc, l, m = softmax_part(list(range(half, pages)), True) if half < pages else (None, None, None)
    acc_a, l_a, m_a = softmax_part(list(range(half)), half == pages)
    if acc is None:
        acc, l = acc_a, l_a
    else:
        m_all = jnp.maximum(m, m_a)
        w, w_a = jnp.exp(m - m_all), jnp.exp(m_a - m_all)
        acc = w * acc + w_a * acc_a
        l = w * l + w_a * l_a

    lam = _lambda(lam_ref, lam_init)
    acc = acc * (1.0 / l)
    for h in range(heads):
        r0 = h * 2 * n_new
        cols = slice(h * 2 * dh, (h + 1) * 2 * dh)
        o = acc[r0:r0 + n_new, cols] - lam * acc[r0 + n_new:r0 + 2 * n_new, cols]
        o_ref[:, cols] = _sub_norm(o, gs_ref, lam_init)


def paged_diff_attention(q, k_new, v_new, cache_kt, cache_v, layer, page_table, lam_p, g_sub,
                         heads, lam_init):
    t, width = q.shape
    b, n_pages = page_table.shape
    n_new = t // b
    dh = width // (2 * heads)
    page = cache_kt.shape[3]
    assert n_new == SUBLANES and 2 * dh == LANES and page == LANES
    nrow = 2 * heads * n_new
    tok = pl.BlockSpec((n_new, width), lambda i, pt: (i, 0))

    def page_spec(p, shape):
        return pl.BlockSpec((1, 1) + shape, lambda i, pt: (layer, pt[i, p], 0, 0))

    grid_spec = pltpu.PrefetchScalarGridSpec(
        num_scalar_prefetch=1,
        grid=(b,),
        in_specs=[tok, tok, tok,
                  pl.BlockSpec(lam_p.shape, lambda i, pt: (0, 0)),
                  pl.BlockSpec((1, 2 * dh), lambda i, pt: (0, 0))]
                 + [page_spec(p, (width, page)) for p in range(n_pages)]
                 + [page_spec(p, (page * heads, 2 * dh)) for p in range(n_pages)],
        out_specs=tok,
        scratch_shapes=[pltpu.VMEM((nrow, (n_pages + 1) * page), F32)],
    )
    return pl.pallas_call(
        functools.partial(_paged_kernel, heads=heads, dh=dh, n_new=n_new, pages=n_pages,
                          page=page, lam_init=lam_init),
        grid_spec=grid_spec,
        out_shape=jax.ShapeDtypeStruct((t, width), F32),
        compiler_params=_cparams(("parallel",)),
        name="paged_diff_attention",
    )(page_table, q, k_new, v_new, lam_p.astype(F32), g_sub.reshape(1, 2 * dh).astype(F32),
      *([cache_kt] * n_pages), *([cache_v] * n_pages))


def _trunk(x, rows_per_seq, mods, state, paged, weights, cfg):
    (g_norm_mix, g_norm_ffn, w_in_m, b_gate_m, g_head_m, w_out_m,
     w_in_d, g_q_d, g_k_d, lam_d, g_sub_d, w_out_d, w_gu_f, w_down_f,
     w_router, b_router, w_gu_e, w_down_e) = weights
    t, d = x.shape
    n_seq = t // rows_per_seq
    depth = g_norm_mix.shape[0]
    tm = cfg["tm"]
    tn_proj, tn_seg, tn_up = 1024, 512, 1408
    heads_m = b_gate_m.shape[1] // 2
    dk, dv = state[0].shape[3], state[0].shape[4]
    qk_w = heads_m * dk
    heads_a, dh = cfg["heads_a"], cfg["dh"]
    new_state, new_kv = state[0].shape[0], []
    kt_stack = None
    for i in range(depth):
        j = i // 2
        mod = mods[i]
        if i % 2 == 0:
            col_scale = jnp.concatenate([jnp.ones((qk_w,), F32), jnp.full((qk_w,), dk ** -0.5, F32),
                                         jnp.ones((heads_m * dv,), F32)]).reshape(1, -1)
            small = rows_per_seq % M_CHUNK != 0
            (qkv,) = norm_proj(x, rows_per_seq, mod, 1, 0, g_norm_mix[i], w_in_m[j], 0,
                               2 * qk_w + heads_m * dv, [F32 if small else BF], tm, tn_proj,
                               col_scale=col_scale, name="mlstm_qkv")
            (og,) = norm_proj(x, rows_per_seq, mod, 1, 0, g_norm_mix[i], w_in_m[j],
                              2 * qk_w + heads_m * dv, heads_m * dv, [F32], tm, tn_proj, name="mlstm_o")
            w_gate = jnp.pad(w_in_m[j][:, 2 * qk_w + 2 * heads_m * dv:], ((0, 0), (0, LANES - 2 * heads_m)))
            (gates,) = norm_proj(x, rows_per_seq, mod, 1, 0, g_norm_mix[i], w_gate, 0, LANES,
                                 [F32], tm, LANES, name="mlstm_gates")
            y, new_state = mlstm_cell(qkv, og, gates, b_gate_m[j], g_head_m[j], *state,
                                      rows_per_seq, cfg["mlstm_seqs"], j, new_state)
            w_out = w_out_m
        else:
            lam_init = 0.8 - 0.6 * math.exp(-0.3 * i)
            prompt = paged is None
            (q,) = norm_proj(x, rows_per_seq, mod, 1, 0, g_norm_mix[i], w_in_d[j], 0, d,
                             [BF if prompt else F32], tm, tn_seg, seg_gain=g_q_d[j],
                             transposed=(prompt,), name="attn_q")
            k32, kbf = norm_proj(x, rows_per_seq, mod, 1, 0, g_norm_mix[i], w_in_d[j], d, d,
                                 [F32, BF], tm, tn_seg, seg_gain=g_k_d[j], transposed=(prompt, False),
                                 stacked=(j, depth // 2, kt_stack), name="attn_k")
            v32, vbf = norm_proj(x, rows_per_seq, mod, 1, 0, g_norm_mix[i], w_in_d[j], 2 * d, d,
                                 [F32, BF], tm, tn_proj, transposed=(False, prompt), name="attn_v")
            if paged is None:
                bound = (dh ** 0.5) * jnp.max(jnp.abs(g_q_d[j])) * jnp.max(jnp.abs(g_k_d[j]))
                y = flash_diff_attention(q, kbf, vbf, bound, lam_d[j], g_sub_d[j], heads_a,
                                         lam_init, cfg["tq"], cfg["tk"])
                kt_stack = k32
            else:
                cache_kt, cache_v, page_table = paged
                y = paged_diff_attention(q, k32, v32, cache_kt, cache_v, j, page_table, lam_d[j],
                                         g_sub_d[j], heads_a, lam_init)
            new_kv.append((k32, v32))
            w_out = w_out_d
        x = down_residual(y, w_out, j, x, rows_per_seq, mod, 2, tm, tn_proj, name="mixer_out")
        if i % 2 == 0:
            act = swiglu_up(x, rows_per_seq, mod, 4, 3, g_norm_ffn[i], w_gu_f, j, tm, tn_up)
            x = down_residual(act, w_down_f, j, x, rows_per_seq, mod, 5, tm, tn_proj, name="ffn_down")
        else:
            n_exp = w_router.shape[2]
            h, dg, rank, counts = router_gates(x, rows_per_seq, mod, 4, 3, g_norm_ffn[i],
                                               w_router[j], b_router[j], tm)
            counts = counts[:, 0, :n_exp].astype(jnp.int32).reshape(-1)
            x = moe_ffn(h, dg, rank, counts, w_gu_e, w_down_e, j, x, rows_per_seq, mod, 5, tm,
                        cfg["moe_ff_chunks"], min(cfg["moe_block_rows"], tm // 2))
    return x, new_state, new_kv, kt_stack


def kernel(x_prompt, x_sample, c_prompt, c_sample, state_C, state_n, state_m, cache_k, cache_v, page_table, w_ada, b_ada, g_norm_mix, g_norm_ffn, w_in_m, b_gate_m, g_head_m, w_out_m, w_in_d, g_q_d, g_k_d, lam_d, g_sub_d, w_out_d, w_gu_f, w_down_f, w_router, b_router, w_gu_e, w_down_e):
    bp, seq, d = x_prompt.shape
    bs, dec_seq, _ = x_sample.shape
    depth = w_ada.shape[0]
    n_ml, _, heads_m, dk, dv = state_C.shape
    n_diff, n_pool, page, heads_a, _, dh = cache_k.shape
    weights = (g_norm_mix, g_norm_ffn, w_in_m.astype(BF), b_gate_m, g_head_m, w_out_m.astype(BF),
               w_in_d.astype(BF), g_q_d, g_k_d, lam_d, g_sub_d, w_out_d.astype(BF),
               w_gu_f.astype(BF), w_down_f.astype(BF),
               w_router, b_router, w_gu_e.astype(BF), w_down_e.astype(BF))

    n_c = bp + bs
    n_c_pad = -(-n_c // SUBLANES) * SUBLANES
    c_all = jnp.pad(jnp.concatenate([c_prompt, c_sample], axis=0), ((0, n_c_pad - n_c), (0, 0)))
    mod_all = ada_mod(c_all, w_ada, b_ada, tn=min(6 * d, 1536))
    mods_p = [mod_all[i, :bp].reshape(bp, 1, 6 * d) for i in range(depth)]
    mods_s = [mod_all[i, bp:n_c].reshape(bs, 1, 6 * d) for i in range(depth)]

    moe = dict(moe_ff_chunks=2, moe_block_rows=128)
    cfg_p = dict(tm=min(1024, seq), tq=min(1024, seq), tk=min(1024, seq), heads_a=heads_a, dh=dh,
                 mlstm_seqs=min(2, bp), **moe)
    cfg_s = dict(tm=min(1024, bs * dec_seq), heads_a=heads_a, dh=dh, mlstm_seqs=min(8, bs), **moe)

    zeros = (jnp.zeros((n_ml, bp, heads_m, dk, dv), F32), jnp.zeros((n_ml, bp, heads_m, dk), F32),
             jnp.zeros((n_ml, bp, heads_m), F32))
    y_p, st_p, kv_p, kt_p = _trunk(x_prompt.reshape(bp * seq, d), seq, mods_p, zeros, None, weights, cfg_p)
    paged = (jnp.transpose(cache_k, (0, 1, 3, 4, 5, 2)).reshape(n_diff, n_pool, heads_a * 2 * dh, page),
             cache_v.reshape(n_diff, n_pool, page * heads_a, 2 * dh), page_table)
    y_s, st_s, kv_s, _ = _trunk(x_sample.reshape(bs * dec_seq, d), dec_seq, mods_s,
                             (state_C, state_n, state_m), paged, weights, cfg_s)

    def stack(items, idx, shape):
        return jnp.stack([it[idx] for it in items]).reshape(shape)

    return (y_p.reshape(bp, seq, d), y_s.reshape(bs, dec_seq, d),
            st_p[0], st_p[1].reshape(n_ml, bp, heads_m, dk), st_p[2][:, :, :, 0, 0],
            st_s[0], st_s[1].reshape(n_ml, bs, heads_m, dk), st_s[2][:, :, :, 0, 0],
            jnp.transpose(kt_p.reshape(n_diff, bp, heads_a, 2, dh, seq), (0, 1, 5, 2, 3, 4)),
            stack(kv_p, 1, (n_diff, bp, seq, heads_a, 2 * dh)),
            stack(kv_s, 0, (n_diff, bs, dec_seq, heads_a, 2, dh)),
            stack(kv_s, 1, (n_diff, bs, dec_seq, heads_a, 2 * dh)))
```
